```python
import jax, jax.numpy as jnp
from jax import lax
import numpy as np

D_MODEL = 2048
BATCH = 4
SEQ = 2048
DEPTH = 1

CHUNK = 64
HG_HEADS = 8
HG_DK = 128
HG_DV = 128
HG_WIDTH = HG_HEADS * HG_DV
AT_HEADS = 16
AT_DH = 64
AT_WIDTH = AT_HEADS * AT_DH
LEFT_CHUNKS = 8
BAND = (LEFT_CHUNKS + 1) * CHUNK
REL_CLIP = 256
N_REL = 2 * REL_CLIP + 1
D_FF = 4 * D_MODEL
N_BRANCH = 2
EPS = 1e-6
D_IN = 4 * HG_WIDTH + 3 * AT_WIDTH + N_BRANCH * D_MODEL
SPLIT_POINTS = (HG_WIDTH, 2 * HG_WIDTH, 3 * HG_WIDTH, 4 * HG_WIDTH,
                4 * HG_WIDTH + AT_WIDTH, 4 * HG_WIDTH + 2 * AT_WIDTH, 4 * HG_WIDTH + 3 * AT_WIDTH)

kernel_name = "hybrid_hgrn2_chunkattn_gated_block"


def rms_norm(x, w):
    xf = x.astype(jnp.float32)
    y = xf * lax.rsqrt(jnp.mean(xf * xf, axis=-1, keepdims=True) + EPS)
    return (y * w.astype(jnp.float32)).astype(x.dtype)


def hgrn2_scan(q, k, v, log_g):
    B, H, T, DK = q.shape
    DV = v.shape[-1]
    nc = T // CHUNK

    def to_chunks(a):
        return a.reshape(B, H, nc, CHUNK, a.shape[-1]).transpose(2, 0, 1, 3, 4)

    causal = jnp.tril(jnp.ones((CHUNK, CHUNK), dtype=bool))

    def step(S, inp):
        qc, kc, vc, gc = inp
        b = jnp.cumsum(gc, axis=2)
        o_inter = jnp.einsum('bhtk,bhkv->bhtv', qc * jnp.exp(b), S)
        diff = b[:, :, :, None, :] - b[:, :, None, :, :]
        decay = jnp.exp(jnp.where(causal[:, :, None], diff, -jnp.inf))
        scores = jnp.einsum('bhtk,bhtsk,bhsk->bhts', qc, decay, kc)
        o_intra = jnp.einsum('bhts,bhsv->bhtv', scores, vc)
        b_last = b[:, :, -1:, :]
        S_new = jnp.exp(b_last[:, :, 0, :])[..., None] * S + jnp.einsum(
            'bhsk,bhsv->bhkv', kc * jnp.exp(b_last - b), vc)
        return S_new, o_inter + o_intra

    S0 = jnp.zeros((B, H, DK, DV), jnp.float32)
    _, o = lax.scan(step, S0, (to_chunks(q), to_chunks(k), to_chunks(v), to_chunks(log_g)))
    return o.transpose(1, 2, 0, 3, 4).reshape(B, H, T, DV)


def chunk_band_attention(q, k, v, rel_bias):
    B, H, T, Dh = q.shape
    nc = T // CHUNK
    pad = LEFT_CHUNKS * CHUNK
    kp = jnp.pad(k, ((0, 0), (0, 0), (pad, 0), (0, 0)))
    vp = jnp.pad(v, ((0, 0), (0, 0), (pad, 0), (0, 0)))
    idx = (jnp.arange(nc) * CHUNK)[:, None] + jnp.arange(BAND)[None, :]
    kb = kp[:, :, idx, :]
    vb = vp[:, :, idx, :]
    qc = q.reshape(B, H, nc, CHUNK, Dh)
    valid = idx >= pad
    t = jnp.arange(CHUNK)
    j = jnp.arange(BAND)
    rel = t[:, None] + pad - j[None, :]
    rel_idx = jnp.clip(rel, -REL_CLIP, REL_CLIP) + REL_CLIP
    bias = rel_bias[:, rel_idx].astype(jnp.float32)
    s = jnp.einsum('bhnqd,bhnkd->bhnqk', qc, kb).astype(jnp.float32) * (Dh ** -0.5) + bias[:, None]
    s = jnp.where(valid[:, None, :], s, -jnp.inf)
    p = jax.nn.softmax(s, axis=-1).astype(v.dtype)
    o = jnp.einsum('bhnqk,bhnkd->bhnqd', p, vb)
    return o.reshape(B, H, T, Dh)


def mixer_block(u, w_in, lb, hg_norm_w, rel_bias, w_branch_a, w_branch_b, w_out):
    B, T, _ = u.shape
    z = u @ w_in
    hq, hf, hi, hg, aq, ak, av, gates = jnp.split(z, SPLIT_POINTS, axis=-1)

    def to_heads(a, h):
        return a.reshape(B, T, h, -1).transpose(0, 2, 1, 3)

    f = jax.nn.sigmoid(hf.astype(jnp.float32))
    g = lb + (1.0 - lb) * f
    log_g = jnp.log(g)
    kk = 1.0 - g
    q = jax.nn.silu(hq.astype(jnp.float32)) * (HG_DK ** -0.5)
    o = hgrn2_scan(to_heads(q, HG_HEADS), to_heads(kk, HG_HEADS),
                   to_heads(hi.astype(jnp.float32), HG_HEADS), to_heads(log_g, HG_HEADS))
    o = o.transpose(0, 2, 1, 3)
    o = rms_norm(o, hg_norm_w) * jax.nn.silu(hg.reshape(B, T, HG_HEADS, HG_DV).astype(jnp.float32))
    y_a = o.reshape(B, T, HG_WIDTH).astype(u.dtype)

    ya = chunk_band_attention(to_heads(aq, AT_HEADS), to_heads(ak, AT_HEADS),
                              to_heads(av, AT_HEADS), rel_bias)
    y_b = ya.transpose(0, 2, 1, 3).reshape(B, T, AT_WIDTH)

    gate_a, gate_b = jnp.split(jax.nn.sigmoid(gates), 2, axis=-1)
    merged = gate_a * (y_a @ w_branch_a) + gate_b * (y_b @ w_branch_b)
    return merged @ w_out


def setup_inputs(seed: int = 0) -> dict:
    key = jax.random.key(seed)
    ks = jax.random.split(key, 13)
    f32 = jnp.float32
    x = jax.random.normal(ks[0], (BATCH, SEQ, D_MODEL), f32)
    w_in = jax.random.normal(ks[1], (DEPTH, D_MODEL, D_IN), f32) * D_MODEL ** -0.5
    lb_logits = jax.random.normal(ks[2], (DEPTH + 1, HG_WIDTH), f32)
    hg_norm_w = 1.0 + 0.02 * jax.random.normal(ks[3], (DEPTH, HG_DV), f32)
    rel_bias = 0.1 * jax.random.normal(ks[4], (DEPTH, AT_HEADS, N_REL), f32)
    w_branch_a = jax.random.normal(ks[5], (DEPTH, HG_WIDTH, D_MODEL), f32) * HG_WIDTH ** -0.5
    w_branch_b = jax.random.normal(ks[6], (DEPTH, AT_WIDTH, D_MODEL), f32) * AT_WIDTH ** -0.5
    w_out = jax.random.normal(ks[7], (DEPTH, D_MODEL, D_MODEL), f32) * D_MODEL ** -0.5
    norm_mix_w = 1.0 + 0.02 * jax.random.normal(ks[8], (DEPTH, D_MODEL), f32)
    norm_mlp_w = 1.0 + 0.02 * jax.random.normal(ks[9], (DEPTH, D_MODEL), f32)
    w_up = jax.random.normal(ks[10], (DEPTH, D_MODEL, D_FF), f32) * D_MODEL ** -0.5
    w_down = jax.random.normal(ks[11], (DEPTH, D_FF, D_MODEL), f32) * D_FF ** -0.5
    norm_final_w = 1.0 + 0.02 * jax.random.normal(ks[12], (D_MODEL,), f32)
    return {"x": x, "w_in": w_in, "lb_logits": lb_logits, "hg_norm_w": hg_norm_w,
            "rel_bias": rel_bias, "w_branch_a": w_branch_a, "w_branch_b": w_branch_b,
            "w_out": w_out, "norm_mix_w": norm_mix_w, "norm_mlp_w": norm_mlp_w,
            "w_up": w_up, "w_down": w_down, "norm_final_w": norm_final_w}


def reference(x, w_in, lb_logits, hg_norm_w, rel_bias, w_branch_a, w_branch_b, w_out,
              norm_mix_w, norm_mlp_w, w_up, w_down, norm_final_w):
    lb_all = jnp.cumsum(jax.nn.softmax(lb_logits.astype(jnp.float32), axis=0), axis=0)
    h = x
    for l in range(DEPTH):
        h = h + mixer_block(rms_norm(h, norm_mix_w[l]), w_in[l], lb_all[l], hg_norm_w[l],
                            rel_bias[l], w_branch_a[l], w_branch_b[l], w_out[l])
        u = rms_norm(h, norm_mlp_w[l])
        h = h + jnp.square(jax.nn.relu(u @ w_up[l])) @ w_down[l]
    return rms_norm(h, norm_final_w)
```

```python
import functools

import jax
import jax.numpy as jnp
import numpy as np
from jax import lax
from jax.experimental import pallas as pl
from jax.experimental.pallas import tpu as pltpu

D_MODEL = 2048
CHUNK = 64
HG_HEADS = 8
HG_DK = 128
HG_DV = 128
HG_WIDTH = HG_HEADS * HG_DV
AT_HEADS = 16
AT_DH = 64
AT_WIDTH = AT_HEADS * AT_DH
LEFT_CHUNKS = 8
REL_CLIP = 256
EPS = 1e-6
D_IN = 4 * HG_WIDTH + 3 * AT_WIDTH + 2 * D_MODEL

LANES = 128
N_LEVELS = 6
ATT_TQ = 4 * CHUNK
ATT_PIECES = 3

BF16 = jnp.bfloat16
F32 = jnp.float32

NN = (((1,), (0,)), ((), ()))
NT = (((1,), (1,)), ((), ()))
TN = (((0,), (0,)), ((), ()))


def _dot(a, b, dims=NN):
    return lax.dot_general(a, b, dims, preferred_element_type=F32)


def _rms(xf, w):
    return xf * lax.rsqrt(jnp.mean(xf * xf, axis=-1, keepdims=True) + EPS) * w


def _in_proj_kernel(x_ref, nw_ref, w_ref, z_ref, u_ref):
    @pl.when(pl.program_id(1) == 0)
    def _():
        u_ref[...] = _rms(x_ref[...], nw_ref[...]).astype(BF16)

    z_ref[...] = _dot(u_ref[...], w_ref[...]).astype(z_ref.dtype)


def _in_proj(x2, norm_w, w_bf, *, tm=512, tn=1024):
    m, k = x2.shape
    n = w_bf.shape[1]
    return pl.pallas_call(
        _in_proj_kernel,
        out_shape=jax.ShapeDtypeStruct((m, n), F32),
        grid=(m // tm, n // tn),
        in_specs=[
            pl.BlockSpec((tm, k), lambda i, j: (i, 0)),
            pl.BlockSpec((1, k), lambda i, j: (0, 0)),
            pl.BlockSpec((k, tn), lambda i, j: (0, j)),
        ],
        out_specs=pl.BlockSpec((tm, tn), lambda i, j: (i, j)),
        scratch_shapes=[pltpu.VMEM((tm, k), BF16)],
        compiler_params=pltpu.CompilerParams(
            dimension_semantics=("parallel", "arbitrary"),
            vmem_limit_bytes=48 * 1024 * 1024),
        name="in_proj",
    )(x2, norm_w.reshape(1, k), w_bf)


def _hgrn_tables():
    t = np.arange(CHUNK)[:, None]
    s = np.arange(CHUNK)[None, :]
    mats, masks = [], []
    for l in range(N_LEVELS):
        h = 1 << l
        start = (t // (2 * h)) * (2 * h)
        ref = start + h - 1
        is_q = (t - start) >= h
        w = np.where(is_q, (s > ref) & (s <= t), (s > t) & (s <= ref))
        mats.append(w)
        s_start = (s // (2 * h)) * (2 * h)
        masks.append((start == s_start) & is_q & ((s - s_start) < h))
    mats.append(s <= t)
    mats.append(s > t)
    masks.append(t == s)
    w = np.concatenate(mats, axis=0).astype(np.float32)
    w2 = np.concatenate([w, w], axis=1)
    return w2, np.stack(masks).astype(np.float32)


def _hgrn_kernel(hq_ref, hf_ref, hi_ref, hg_ref, lbl_ref, nw_ref, wexp_ref, mask_ref,
                 y_ref, st_ref, *, n_chunks):
    @pl.when(pl.program_id(1) == 0)
    def _():
        st_ref[...] = jnp.zeros_like(st_ref)

    logits = lbl_ref[...]
    l0, l1 = logits[0:1, :], logits[1:2, :]
    mx = jnp.maximum(l0, l1)
    e0, e1 = jnp.exp(l0 - mx), jnp.exp(l1 - mx)
    lb = e0 / (e0 + e1)
    nw = nw_ref[...]
    wexp = wexp_ref[...]
    n_exp = (N_LEVELS + 2) * CHUNK

    def chunk_body(c, carry):
        r = pl.ds(pl.multiple_of(c * CHUNK, CHUNK), CHUNK)
        f = jax.nn.sigmoid(hf_ref[r, :])
        g = lb + (1.0 - lb) * f
        lg = jnp.log(g)
        kk = 1.0 - g
        q = jax.nn.silu(hq_ref[r, :]) * (HG_DK ** -0.5)
        lg_hi = lg.astype(BF16)
        lg_lo = (lg - lg_hi.astype(F32)).astype(BF16)
        ex = jnp.exp(_dot(wexp, jnp.concatenate([lg_hi, lg_lo], axis=0)))
        for h in range(HG_HEADS):
            sl = slice(h * HG_DK, (h + 1) * HG_DK)
            qh, kh = q[:, sl], kk[:, sl]
            vh = hi_ref[r, sl].astype(BF16)
            sc = mask_ref[N_LEVELS] * _dot(qh.astype(BF16), kh.astype(BF16), NT)
            for l in range(N_LEVELS):
                xl = ex[l * CHUNK:(l + 1) * CHUNK, sl]
                sc = sc + mask_ref[l] * _dot((qh * xl).astype(BF16), (kh * xl).astype(BF16), NT)
            eb = ex[N_LEVELS * CHUNK:(N_LEVELS + 1) * CHUNK, sl]
            er = ex[(N_LEVELS + 1) * CHUNK:n_exp, sl]
            st = st_ref[h]
            o = _dot(sc.astype(BF16), vh) + _dot((qh * eb).astype(BF16), st.astype(BF16), NT)
            d_last = eb[CHUNK - 1:CHUNK, :]
            st_ref[h] = d_last * st + _dot(vh, (kh * er).astype(BF16), TN)
            y = _rms(o, nw) * jax.nn.silu(hg_ref[r, sl])
            y_ref[r, sl] = y.astype(y_ref.dtype)
        return carry

    lax.fori_loop(0, n_chunks, chunk_body, 0)


def _hgrn(z, lb_logits, hg_norm_w, batch, seq, *, tb=256):
    wexp, masks = _hgrn_tables()
    nblk = seq // tb
    wcol = HG_WIDTH

    def zspec(col):
        return pl.BlockSpec((tb, wcol), lambda b, i, col=col: (b * nblk + i, col))

    return pl.pallas_call(
        functools.partial(_hgrn_kernel, n_chunks=tb // CHUNK),
        out_shape=jax.ShapeDtypeStruct((batch * seq, HG_WIDTH), BF16),
        grid=(batch, nblk),
        in_specs=[
            zspec(0), zspec(1), zspec(2), zspec(3),
            pl.BlockSpec((2, HG_WIDTH), lambda b, i: (0, 0)),
            pl.BlockSpec((1, HG_DV), lambda b, i: (0, 0)),
            pl.BlockSpec(wexp.shape, lambda b, i: (0, 0)),
            pl.BlockSpec(masks.shape, lambda b, i: (0, 0, 0)),
        ],
        out_specs=pl.BlockSpec((tb, HG_WIDTH), lambda b, i: (b * nblk + i, 0)),
        scratch_shapes=[pltpu.VMEM((HG_HEADS, HG_DV, HG_DK), F32)],
        compiler_params=pltpu.CompilerParams(
            dimension_semantics=("parallel", "arbitrary"),
            vmem_limit_bytes=48 * 1024 * 1024),
        name="hgrn2",
    )(z, z, z, z, lb_logits, hg_norm_w.reshape(1, HG_DV),
      jnp.asarray(wexp, BF16), jnp.asarray(masks))


def _attn_bias(rel_bias):
    tq = np.arange(ATT_TQ)[:, None]
    jk = np.arange(ATT_PIECES * ATT_TQ)[None, :]
    rel = tq + (ATT_PIECES - 1) * ATT_TQ - jk
    idx = np.clip(rel, -REL_CLIP, REL_CLIP) + REL_CLIP
    qc, kc = tq // CHUNK, jk // CHUNK
    lo = (ATT_PIECES - 1) * (ATT_TQ // CHUNK) - LEFT_CHUNKS
    valid = (kc >= qc + lo) & (kc <= qc + lo + LEFT_CHUNKS)
    return jnp.where(valid[None], rel_bias[:, idx].astype(F32), -jnp.inf)


def _attn_kernel(q_ref, k_ref, v_ref, bias_ref, o_ref, *, seq):
    lane = lax.broadcasted_iota(jnp.int32, (ATT_TQ, LANES), 1)
    head0 = lane < AT_DH
    for i in range(seq // ATT_TQ):
        q = q_ref[i * ATT_TQ:(i + 1) * ATT_TQ, :] * (AT_DH ** -0.5)
        pieces = [j for j in range(ATT_PIECES) if i - (ATT_PIECES - 1) + j >= 0]
        ks = [k_ref[(i - (ATT_PIECES - 1) + j) * ATT_TQ:(i - (ATT_PIECES - 2) + j) * ATT_TQ, :]
              .astype(BF16) for j in pieces]
        vs = [v_ref[(i - (ATT_PIECES - 1) + j) * ATT_TQ:(i - (ATT_PIECES - 2) + j) * ATT_TQ, :]
              .astype(BF16) for j in pieces]
        outs = []
        for hh in range(2):
            sel = head0 if hh == 0 else jnp.logical_not(head0)
            qm = jnp.where(sel, q, 0.0).astype(BF16)
            ss = [_dot(qm, kj, NT) + bias_ref[hh, :, j * ATT_TQ:(j + 1) * ATT_TQ]
                  for j, kj in zip(pieces, ks)]
            mx = functools.reduce(jnp.maximum, [jnp.max(s, axis=-1, keepdims=True) for s in ss])
            ps = [jnp.exp(s - mx) for s in ss]
            den = functools.reduce(jnp.add, [jnp.sum(p, axis=-1, keepdims=True) for p in ps])
            acc = functools.reduce(jnp.add, [_dot(p.astype(BF16), vj) for p, vj in zip(ps, vs)])
            outs.append(acc / den)
        o_ref[i * ATT_TQ:(i + 1) * ATT_TQ, :] = jnp.where(head0, outs[0], outs[1]).astype(o_ref.dtype)


def _attn(z, rel_bias, batch, seq):
    bias = _attn_bias(rel_bias)
    col0 = 4 * HG_WIDTH // LANES
    ncol = AT_WIDTH // LANES

    def zspec(which):
        return pl.BlockSpec((seq, LANES), lambda hp, b, which=which: (b, col0 + which * ncol + hp))

    return pl.pallas_call(
        functools.partial(_attn_kernel, seq=seq),
        out_shape=jax.ShapeDtypeStruct((batch * seq, AT_WIDTH), BF16),
        grid=(AT_HEADS // 2, batch),
        in_specs=[
            zspec(0), zspec(1), zspec(2),
            pl.BlockSpec((2, ATT_TQ, ATT_PIECES * ATT_TQ), lambda hp, b: (hp, 0, 0)),
        ],
        out_specs=pl.BlockSpec((seq, LANES), lambda hp, b: (b, hp)),
        compiler_params=pltpu.CompilerParams(
            dimension_semantics=("parallel", "parallel"),
            vmem_limit_bytes=48 * 1024 * 1024),
        name="band_attn",
    )(z, z, z, bias)


def _merge_kernel(ya_ref, yb_ref, wa_ref, wb_ref, ga_ref, gb_ref, o_ref):
    pa = _dot(ya_ref[...], wa_ref[...])
    pb = _dot(yb_ref[...], wb_ref[...])
    o_ref[...] = (jax.nn.sigmoid(ga_ref[...]) * pa + jax.nn.sigmoid(gb_ref[...]) * pb).astype(o_ref.dtype)


def _merge(ya, yb, wa_bf, wb_bf, z, *, tm=512, tn=1024):
    m = ya.shape[0]
    n = wa_bf.shape[1]
    gcol = (4 * HG_WIDTH + 3 * AT_WIDTH) // tn
    return pl.pallas_call(
        _merge_kernel,
        out_shape=jax.ShapeDtypeStruct((m, n), BF16),
        grid=(m // tm, n // tn),
        in_specs=[
            pl.BlockSpec((tm, HG_WIDTH), lambda i, j: (i, 0)),
            pl.BlockSpec((tm, AT_WIDTH), lambda i, j: (i, 0)),
            pl.BlockSpec((HG_WIDTH, tn), lambda i, j: (0, j)),
            pl.BlockSpec((AT_WIDTH, tn), lambda i, j: (0, j)),
            pl.BlockSpec((tm, tn), lambda i, j: (i, gcol + j)),
            pl.BlockSpec((tm, tn), lambda i, j: (i, gcol + n // tn + j)),
        ],
        out_specs=pl.BlockSpec((tm, tn), lambda i, j: (i, j)),
        compiler_params=pltpu.CompilerParams(
            dimension_semantics=("parallel", "arbitrary"),
            vmem_limit_bytes=48 * 1024 * 1024),
        name="merge",
    )(ya, yb, wa_bf, wb_bf, z, z)


def _out_proj_kernel(m_ref, w_ref, x_ref, o_ref):
    o_ref[...] = x_ref[...] + _dot(m_ref[...], w_ref[...])


def _out_proj(merged, w_bf, x2, *, tm=512, tn=1024):
    m, k = merged.shape
    n = w_bf.shape[1]
    return pl.pallas_call(
        _out_proj_kernel,
        out_shape=jax.ShapeDtypeStruct((m, n), F32),
        grid=(m // tm, n // tn),
        in_specs=[
            pl.BlockSpec((tm, k), lambda i, j: (i, 0)),
            pl.BlockSpec((k, tn), lambda i, j: (0, j)),
            pl.BlockSpec((tm, tn), lambda i, j: (i, j)),
        ],
        out_specs=pl.BlockSpec((tm, tn), lambda i, j: (i, j)),
        compiler_params=pltpu.CompilerParams(
            dimension_semantics=("parallel", "arbitrary"),
            vmem_limit_bytes=48 * 1024 * 1024),
        name="out_proj",
    )(merged, w_bf, x2)


def _mlp_kernel(h_ref, nw_ref, wu_ref, wd_ref, fw_ref, o_ref, u_ref):
    f = pl.program_id(1)

    @pl.when(f == 0)
    def _():
        u_ref[...] = _rms(h_ref[...], nw_ref[...]).astype(BF16)
        o_ref[...] = jnp.zeros_like(o_ref)

    a = jnp.maximum(_dot(u_ref[...], wu_ref[...]), 0.0)
    o_ref[...] += _dot((a * a).astype(BF16), wd_ref[...])

    @pl.when(f == pl.num_programs(1) - 1)
    def _():
        o_ref[...] = _rms(h_ref[...] + o_ref[...], fw_ref[...])


def _mlp(h, norm_w, wu_bf, wd_bf, final_w, *, tm=512, tf=512):
    m, d = h.shape
    dff = wu_bf.shape[1]
    return pl.pallas_call(
        _mlp_kernel,
        out_shape=jax.ShapeDtypeStruct((m, d), F32),
        grid=(m // tm, dff // tf),
        in_specs=[
            pl.BlockSpec((tm, d), lambda i, f: (i, 0)),
            pl.BlockSpec((1, d), lambda i, f: (0, 0)),
            pl.BlockSpec((d, tf), lambda i, f: (0, f)),
            pl.BlockSpec((tf, d), lambda i, f: (f, 0)),
            pl.BlockSpec((1, d), lambda i, f: (0, 0)),
        ],
        out_specs=pl.BlockSpec((tm, d), lambda i, f: (i, 0)),
        scratch_shapes=[pltpu.VMEM((tm, d), BF16)],
        compiler_params=pltpu.CompilerParams(
            dimension_semantics=("parallel", "arbitrary"),
            vmem_limit_bytes=48 * 1024 * 1024),
        name="mlp",
    )(h, norm_w.reshape(1, d), wu_bf, wd_bf, final_w.reshape(1, d))


def kernel(x, w_in, lb_logits, hg_norm_w, rel_bias, w_branch_a, w_branch_b, w_out,
           norm_mix_w, norm_mlp_w, w_up, w_down, norm_final_w):
    batch, seq, d = x.shape
    assert d == D_MODEL and seq % ATT_TQ == 0 and w_in.shape[0] == 1
    x2 = x.reshape(batch * seq, d)
    z = _in_proj(x2, norm_mix_w[0], w_in[0].astype(BF16))
    ya = _hgrn(z, lb_logits, hg_norm_w[0], batch, seq)
    yb = _attn(z, rel_bias[0], batch, seq)
    merged = _merge(ya, yb, w_branch_a[0].astype(BF16), w_branch_b[0].astype(BF16), z)
    h = _out_proj(merged, w_out[0].astype(BF16), x2)
    out = _mlp(h, norm_mlp_w[0], w_up[0].astype(BF16), w_down[0].astype(BF16), norm_final_w)
    return out.reshape(batch, seq, d)
```

```python
import functools

import jax
import jax.numpy as jnp
import numpy as np
from jax import lax
from jax.experimental import pallas as pl
from jax.experimental.pallas import tpu as pltpu

D_MODEL = 2048
CHUNK = 64
HG_HEADS = 8
HG_DK = 128
HG_DV = 128
HG_WIDTH = HG_HEADS * HG_DV
AT_HEADS = 16
AT_DH = 64
AT_WIDTH = AT_HEADS * AT_DH
LEFT_CHUNKS = 8
REL_CLIP = 256
EPS = 1e-6
D_IN = 4 * HG_WIDTH + 3 * AT_WIDTH + 2 * D_MODEL

LANES = 128
N_LEVELS = 6
ATT_TQ = 4 * CHUNK
ATT_PIECES = 3

BF16 = jnp.bfloat16
F32 = jnp.float32
VMEM_LIMIT = 56 * 1024 * 1024

NN = (((1,), (0,)), ((), ()))
NT = (((1,), (1,)), ((), ()))
TN = (((0,), (0,)), ((), ()))


def _dot(a, b, dims=NN):
    return lax.dot_general(a, b, dims, preferred_element_type=F32)


def _rms(xf, w):
    return xf * lax.rsqrt(jnp.mean(xf * xf, axis=-1, keepdims=True) + EPS) * w


def _in_proj_kernel(x_ref, nw_ref, w_ref, z_ref, hf_ref, u_ref, *, hf_lo, hf_hi):
    j = pl.program_id(1)

    @pl.when(j == 0)
    def _():
        u_ref[...] = _rms(x_ref[...], nw_ref[...]).astype(BF16)

    acc = _dot(u_ref[...], w_ref[...])
    z_ref[...] = acc.astype(z_ref.dtype)

    @pl.when((j >= hf_lo) & (j < hf_hi))
    def _():
        hf_ref[...] = acc


def _in_proj(x2, norm_w, w_bf, *, tm=1024, tn=512):
    m, k = x2.shape
    n = w_bf.shape[1]
    hf_lo, hf_hi = HG_WIDTH // tn, 2 * HG_WIDTH // tn
    return pl.pallas_call(
        functools.partial(_in_proj_kernel, hf_lo=hf_lo, hf_hi=hf_hi),
        out_shape=(jax.ShapeDtypeStruct((m, n), BF16),
                   jax.ShapeDtypeStruct((m, HG_WIDTH), F32)),
        grid=(m // tm, n // tn),
        in_specs=[
            pl.BlockSpec((tm, k), lambda i, j: (i, 0)),
            pl.BlockSpec((1, k), lambda i, j: (0, 0)),
            pl.BlockSpec((k, tn), lambda i, j: (0, j)),
        ],
        out_specs=(
            pl.BlockSpec((tm, tn), lambda i, j: (i, j)),
            pl.BlockSpec((tm, tn), lambda i, j: (i, jnp.clip(j - hf_lo, 0, hf_hi - hf_lo - 1))),
        ),
        scratch_shapes=[pltpu.VMEM((tm, k), BF16)],
        compiler_params=pltpu.CompilerParams(
            dimension_semantics=("parallel", "arbitrary"),
            vmem_limit_bytes=VMEM_LIMIT),
        name="in_proj",
    )(x2, norm_w.reshape(1, k), w_bf)


def _hgrn_tables():
    t = np.arange(CHUNK)[:, None]
    s = np.arange(CHUNK)[None, :]
    mats, masks = [], []
    for l in range(N_LEVELS):
        h = 1 << l
        start = (t // (2 * h)) * (2 * h)
        ref = start + h - 1
        is_q = (t - start) >= h
        w = np.where(is_q, (s > ref) & (s <= t), (s > t) & (s <= ref))
        mats.append(w)
        s_start = (s // (2 * h)) * (2 * h)
        masks.append((start == s_start) & is_q & ((s - s_start) < h))
    mats.append(s <= t)
    mats.append(s > t)
    masks.append(t == s)
    w = np.concatenate(mats, axis=0).astype(np.float32)
    w2 = np.concatenate([w, w], axis=1)
    return w2, np.stack(masks).astype(np.float32)


def _hgrn_kernel(hq_ref, hf_ref, hi_ref, hg_ref, lbl_ref, nw_ref, wexp_ref, mask_ref,
                 y_ref, st_ref, *, n_chunks):
    @pl.when(pl.program_id(1) == 0)
    def _():
        st_ref[...] = jnp.zeros_like(st_ref)

    logits = lbl_ref[...]
    l0, l1 = logits[0:1, :], logits[1:2, :]
    mx = jnp.maximum(l0, l1)
    e0, e1 = jnp.exp(l0 - mx), jnp.exp(l1 - mx)
    lb = e0 / (e0 + e1)
    nw = nw_ref[...]
    wexp = wexp_ref[...]
    n_exp = (N_LEVELS + 2) * CHUNK

    def chunk_body(c, carry):
        r = pl.ds(pl.multiple_of(c * CHUNK, CHUNK), CHUNK)
        f = jax.nn.sigmoid(hf_ref[r, :])
        g = lb + (1.0 - lb) * f
        lg = jnp.log(g)
        kk = 1.0 - g
        q = jax.nn.silu(hq_ref[r, :].astype(F32)) * (HG_DK ** -0.5)
        lg_hi = lg.astype(BF16)
        lg_lo = (lg - lg_hi.astype(F32)).astype(BF16)
        ex = jnp.exp(_dot(wexp, jnp.concatenate([lg_hi, lg_lo], axis=0)))
        for h in range(HG_HEADS):
            sl = slice(h * HG_DK, (h + 1) * HG_DK)
            qh, kh = q[:, sl], kk[:, sl]
            vh = hi_ref[r, sl]
            sc = mask_ref[N_LEVELS] * _dot(qh.astype(BF16), kh.astype(BF16), NT)
            for l in range(N_LEVELS):
                xl = ex[l * CHUNK:(l + 1) * CHUNK, sl]
                sc = sc + mask_ref[l] * _dot((qh * xl).astype(BF16), (kh * xl).astype(BF16), NT)
            eb = ex[N_LEVELS * CHUNK:(N_LEVELS + 1) * CHUNK, sl]
            er = ex[(N_LEVELS + 1) * CHUNK:n_exp, sl]
            st = st_ref[h]
            o = _dot(sc.astype(BF16), vh) + _dot((qh * eb).astype(BF16), st.astype(BF16), NT)
            d_last = eb[CHUNK - 1:CHUNK, :]
            st_ref[h] = d_last * st + _dot(vh, (kh * er).astype(BF16), TN)
            y = _rms(o, nw) * jax.nn.silu(hg_ref[r, sl].astype(F32))
            y_ref[r, sl] = y.astype(y_ref.dtype)
        return carry

    lax.fori_loop(0, n_chunks, chunk_body, 0)


def _hgrn(z, hf, lb_logits, hg_norm_w, batch, seq, *, tb=256):
    wexp, masks = _hgrn_tables()
    nblk = seq // tb
    wcol = HG_WIDTH

    def zspec(col):
        return pl.BlockSpec((tb, wcol), lambda b, i, col=col: (b * nblk + i, col))

    return pl.pallas_call(
        functools.partial(_hgrn_kernel, n_chunks=tb // CHUNK),
        out_shape=jax.ShapeDtypeStruct((batch * seq, HG_WIDTH), BF16),
        grid=(batch, nblk),
        in_specs=[
            zspec(0), zspec(0), zspec(2), zspec(3),
            pl.BlockSpec((2, HG_WIDTH), lambda b, i: (0, 0)),
            pl.BlockSpec((1, HG_DV), lambda b, i: (0, 0)),
            pl.BlockSpec(wexp.shape, lambda b, i: (0, 0)),
            pl.BlockSpec(masks.shape, lambda b, i: (0, 0, 0)),
        ],
        out_specs=pl.BlockSpec((tb, HG_WIDTH), lambda b, i: (b * nblk + i, 0)),
        scratch_shapes=[pltpu.VMEM((HG_HEADS, HG_DV, HG_DK), F32)],
        compiler_params=pltpu.CompilerParams(
            dimension_semantics=("parallel", "arbitrary"),
            vmem_limit_bytes=VMEM_LIMIT),
        name="hgrn2",
    )(z, hf, z, z, lb_logits, hg_norm_w.reshape(1, HG_DV),
      jnp.asarray(wexp, BF16), jnp.asarray(masks))


ATT_TK = ATT_PIECES * ATT_TQ
ATT_ROLL = 1024
assert ATT_TQ - 1 + ATT_TK <= ATT_ROLL and ATT_TK - REL_CLIP == 2 * REL_CLIP


def _rel_table(rel_bias):
    rev = rel_bias[:, ::-1].astype(F32)
    edge = jnp.broadcast_to(rev[:, :1], (rel_bias.shape[0], REL_CLIP))
    g = jnp.concatenate([edge, rev[:, :2 * REL_CLIP], edge], axis=1)
    return g.reshape(AT_HEADS // 2, 2, ATT_ROLL)


def _attn_kernel(q_ref, k_ref, v_ref, g_ref, o_ref, bias_ref, *, seq):
    @pl.when(pl.program_id(1) == 0)
    def _():
        tq = lax.broadcasted_iota(jnp.int32, (ATT_TQ, ATT_TK), 0) // CHUNK
        kc = lax.broadcasted_iota(jnp.int32, (ATT_TQ, ATT_TK), 1) // CHUNK
        lo = (ATT_PIECES - 1) * (ATT_TQ // CHUNK) - LEFT_CHUNKS
        in_band = (kc >= tq + lo) & (kc <= tq + lo + LEFT_CHUNKS)
        g = g_ref[0]
        for hh in range(2):
            tbl = jnp.broadcast_to(g[hh:hh + 1, :], (ATT_TQ, ATT_ROLL))
            toep = pltpu.roll(tbl, 0, 1, stride=1, stride_axis=0)[:, :ATT_TK]
            bias_ref[hh] = jnp.where(in_band, toep, -jnp.inf)

    lane = lax.broadcasted_iota(jnp.int32, (ATT_TQ, LANES), 1)
    head0 = lane < AT_DH
    for i in range(seq // ATT_TQ):
        q = q_ref[i * ATT_TQ:(i + 1) * ATT_TQ, :] * (AT_DH ** -0.5)
        pieces = [j for j in range(ATT_PIECES) if i - (ATT_PIECES - 1) + j >= 0]
        ks = [k_ref[(i - (ATT_PIECES - 1) + j) * ATT_TQ:(i - (ATT_PIECES - 2) + j) * ATT_TQ, :]
              for j in pieces]
        vs = [v_ref[(i - (ATT_PIECES - 1) + j) * ATT_TQ:(i - (ATT_PIECES - 2) + j) * ATT_TQ, :]
              for j in pieces]
        outs = []
        for hh in range(2):
            sel = head0 if hh == 0 else jnp.logical_not(head0)
            qm = jnp.where(sel, q, jnp.zeros_like(q))
            ss = [_dot(qm, kj, NT) + bias_ref[hh, :, j * ATT_TQ:(j + 1) * ATT_TQ]
                  for j, kj in zip(pieces, ks)]
            mx = functools.reduce(jnp.maximum, [jnp.max(s, axis=-1, keepdims=True) for s in ss])
            ps = [jnp.exp(s - mx) for s in ss]
            den = functools.reduce(jnp.add, [jnp.sum(p, axis=-1, keepdims=True) for p in ps])
            acc = functools.reduce(jnp.add, [_dot(p.astype(BF16), vj) for p, vj in zip(ps, vs)])
            outs.append(acc / den)
        o_ref[i * ATT_TQ:(i + 1) * ATT_TQ, :] = jnp.where(head0, outs[0], outs[1]).astype(o_ref.dtype)


def _attn(z, rel_bias, batch, seq):
    g = _rel_table(rel_bias)
    col0 = 4 * HG_WIDTH // LANES
    ncol = AT_WIDTH // LANES

    def zspec(which):
        return pl.BlockSpec((seq, LANES), lambda hp, b, which=which: (b, col0 + which * ncol + hp))

    return pl.pallas_call(
        functools.partial(_attn_kernel, seq=seq),
        out_shape=jax.ShapeDtypeStruct((batch * seq, AT_WIDTH), BF16),
        grid=(AT_HEADS // 2, batch),
        in_specs=[
            zspec(0), zspec(1), zspec(2),
            pl.BlockSpec((1, 2, ATT_ROLL), lambda hp, b: (hp, 0, 0)),
        ],
        out_specs=pl.BlockSpec((seq, LANES), lambda hp, b: (b, hp)),
        scratch_shapes=[pltpu.VMEM((2, ATT_TQ, ATT_TK), F32)],
        compiler_params=pltpu.CompilerParams(
            dimension_semantics=("parallel", "arbitrary"),
            vmem_limit_bytes=VMEM_LIMIT),
        name="band_attn",
    )(z, z, z, g)


def _merge_kernel(ya_ref, yb_ref, wa_ref, wb_ref, ga_ref, gb_ref, o_ref):
    pa = _dot(ya_ref[...], wa_ref[...])
    pb = _dot(yb_ref[...], wb_ref[...])
    ga = jax.nn.sigmoid(ga_ref[...].astype(F32))
    gb = jax.nn.sigmoid(gb_ref[...].astype(F32))
    o_ref[...] = (ga * pa + gb * pb).astype(o_ref.dtype)


def _merge(ya, yb, wa_bf, wb_bf, z, *, tm=1024, tn=1024):
    m = ya.shape[0]
    n = wa_bf.shape[1]
    gcol = (4 * HG_WIDTH + 3 * AT_WIDTH) // tn
    return pl.pallas_call(
        _merge_kernel,
        out_shape=jax.ShapeDtypeStruct((m, n), BF16),
        grid=(m // tm, n // tn),
        in_specs=[
            pl.BlockSpec((tm, HG_WIDTH), lambda i, j: (i, 0)),
            pl.BlockSpec((tm, AT_WIDTH), lambda i, j: (i, 0)),
            pl.BlockSpec((HG_WIDTH, tn), lambda i, j: (0, j)),
            pl.BlockSpec((AT_WIDTH, tn), lambda i, j: (0, j)),
            pl.BlockSpec((tm, tn), lambda i, j: (i, gcol + j)),
            pl.BlockSpec((tm, tn), lambda i, j: (i, gcol + n // tn + j)),
        ],
        out_specs=pl.BlockSpec((tm, tn), lambda i, j: (i, j)),
        compiler_params=pltpu.CompilerParams(
            dimension_semantics=("parallel", "arbitrary"),
            vmem_limit_bytes=VMEM_LIMIT),
        name="merge",
    )(ya, yb, wa_bf, wb_bf, z, z)


def _out_proj_kernel(m_ref, w_ref, x_ref, o_ref):
    o_ref[...] = x_ref[...] + _dot(m_ref[...], w_ref[...])


def _out_proj(merged, w_bf, x2, *, tm=1024, tn=1024):
    m, k = merged.shape
    n = w_bf.shape[1]
    return pl.pallas_call(
        _out_proj_kernel,
        out_shape=jax.ShapeDtypeStruct((m, n), F32),
        grid=(m // tm, n // tn),
        in_specs=[
            pl.BlockSpec((tm, k), lambda i, j: (i, 0)),
            pl.BlockSpec((k, tn), lambda i, j: (0, j)),
            pl.BlockSpec((tm, tn), lambda i, j: (i, j)),
        ],
        out_specs=pl.BlockSpec((tm, tn), lambda i, j: (i, j)),
        compiler_params=pltpu.CompilerParams(
            dimension_semantics=("parallel", "arbitrary"),
            vmem_limit_bytes=VMEM_LIMIT),
        name="out_proj",
    )(merged, w_bf, x2)


def _mlp_kernel(h_ref, nw_ref, wu_ref, wd_ref, fw_ref, o_ref, u_ref):
    f = pl.program_id(1)

    @pl.when(f == 0)
    def _():
        u_ref[...] = _rms(h_ref[...], nw_ref[...]).astype(BF16)
        o_ref[...] = jnp.zeros_like(o_ref)

    a = jnp.maximum(_dot(u_ref[...], wu_ref[...]), 0.0)
    o_ref[...] += _dot((a * a).astype(BF16), wd_ref[...])

    @pl.when(f == pl.num_programs(1) - 1)
    def _():
        o_ref[...] = _rms(h_ref[...] + o_ref[...], fw_ref[...])


def _mlp(h, norm_w, wu_bf, wd_bf, final_w, *, tm=1024, tf=512):
    m, d = h.shape
    dff = wu_bf.shape[1]
    return pl.pallas_call(
        _mlp_kernel,
        out_shape=jax.ShapeDtypeStruct((m, d), F32),
        grid=(m // tm, dff // tf),
        in_specs=[
            pl.BlockSpec((tm, d), lambda i, f: (i, 0)),
            pl.BlockSpec((1, d), lambda i, f: (0, 0)),
            pl.BlockSpec((d, tf), lambda i, f: (0, f)),
            pl.BlockSpec((tf, d), lambda i, f: (f, 0)),
            pl.BlockSpec((1, d), lambda i, f: (0, 0)),
        ],
        out_specs=pl.BlockSpec((tm, d), lambda i, f: (i, 0)),
        scratch_shapes=[pltpu.VMEM((tm, d), BF16)],
        compiler_params=pltpu.CompilerParams(
            dimension_semantics=("parallel", "arbitrary"),
            vmem_limit_bytes=VMEM_LIMIT),
        name="mlp",
    )(h, norm_w.reshape(1, d), wu_bf, wd_bf, final_w.reshape(1, d))


def kernel(x, w_in, lb_logits, hg_norm_w, rel_bias, w_branch_a, w_branch_b, w_out,
           norm_mix_w, norm_mlp_w, w_up, w_down, norm_final_w):
    batch, seq, d = x.shape
    assert d == D_MODEL and seq % ATT_TQ == 0 and w_in.shape[0] == 1
    x2 = x.reshape(batch * seq, d)
    z, hf = _in_proj(x2, norm_mix_w[0], w_in[0].astype(BF16))
    ya = _hgrn(z, hf, lb_logits, hg_norm_w[0], batch, seq)
    yb = _attn(z, rel_bias[0], batch, seq)
    merged = _merge(ya, yb, w_branch_a[0].astype(BF16), w_branch_b[0].astype(BF16), z)
    h = _out_proj(merged, w_out[0].astype(BF16), x2)
    out = _mlp(h, norm_mlp_w[0], w_up[0].astype(BF16), w_down[0].astype(BF16), norm_final_w)
    return out.reshape(batch, seq, d)
```

```python
import functools

import jax
import jax.numpy as jnp
import numpy as np
from jax import lax
from jax.experimental import pallas as pl
from jax.experimental.pallas import tpu as pltpu

D_MODEL = 2048
CHUNK = 64
HG_HEADS = 8
HG_DK = 128
HG_DV = 128
HG_WIDTH = HG_HEADS * HG_DV
AT_HEADS = 16
AT_DH = 64
AT_WIDTH = AT_HEADS * AT_DH
LEFT_CHUNKS = 8
REL_CLIP = 256
EPS = 1e-6
D_IN = 4 * HG_WIDTH + 3 * AT_WIDTH + 2 * D_MODEL

LANES = 128
N_LEVELS = 6
ATT_TQ = 2 * CHUNK

BF16 = jnp.bfloat16
F32 = jnp.float32
VMEM_LIMIT = 56 * 1024 * 1024

NN = (((1,), (0,)), ((), ()))
NT = (((1,), (1,)), ((), ()))
TN = (((0,), (0,)), ((), ()))


def _dot(a, b, dims=NN):
    return lax.dot_general(a, b, dims, preferred_element_type=F32)


def _rms(xf, w):
    return xf * lax.rsqrt(jnp.mean(xf * xf, axis=-1, keepdims=True) + EPS) * w


CAST_BLOCKS = 64


def _in_proj_kernel(x_ref, nw_ref, w_ref, *rest, hf_lo, hf_hi, n_cast):
    cast_in, rest = rest[:n_cast], rest[n_cast:]
    z_ref, hf_ref = rest[:2]
    cast_out, u_ref = rest[2:2 + n_cast], rest[2 + n_cast]
    j = pl.program_id(1)

    @pl.when(j == 0)
    def _():
        u_ref[...] = _rms(x_ref[...], nw_ref[...]).astype(BF16)

    acc = _dot(u_ref[...], w_ref[...])
    z_ref[...] = acc.astype(z_ref.dtype)

    @pl.when((j >= hf_lo) & (j < hf_hi))
    def _():
        hf_ref[...] = acc

    for src, dst in zip(cast_in, cast_out):
        dst[...] = src[...].astype(BF16)


def _in_proj(x2, norm_w, w_bf, later_weights, *, tm=1024, tn=1024):
    m, k = x2.shape
    n = w_bf.shape[1]
    nj = n // tn
    assert (m // tm) * nj >= CAST_BLOCKS
    hf_lo, hf_hi = HG_WIDTH // tn, 2 * HG_WIDTH // tn

    def cast_spec(w):
        rows, cols = w.shape
        return pl.BlockSpec((rows // CAST_BLOCKS, cols),
                            lambda i, j: (jnp.minimum(i * nj + j, CAST_BLOCKS - 1), 0))

    cast_specs = [cast_spec(w) for w in later_weights]
    outs = pl.pallas_call(
        functools.partial(_in_proj_kernel, hf_lo=hf_lo, hf_hi=hf_hi, n_cast=len(later_weights)),
        out_shape=(jax.ShapeDtypeStruct((m, n), BF16),
                   jax.ShapeDtypeStruct((m, HG_WIDTH), F32),
                   *[jax.ShapeDtypeStruct(w.shape, BF16) for w in later_weights]),
        grid=(m // tm, nj),
        in_specs=[
            pl.BlockSpec((tm, k), lambda i, j: (i, 0)),
            pl.BlockSpec((1, k), lambda i, j: (0, 0)),
            pl.BlockSpec((k, tn), lambda i, j: (0, j)),
            *cast_specs,
        ],
        out_specs=(
            pl.BlockSpec((tm, tn), lambda i, j: (i, j)),
            pl.BlockSpec((tm, tn), lambda i, j: (i, jnp.clip(j - hf_lo, 0, hf_hi - hf_lo - 1))),
            *cast_specs,
        ),
        scratch_shapes=[pltpu.VMEM((tm, k), BF16)],
        compiler_params=pltpu.CompilerParams(
            dimension_semantics=("arbitrary", "arbitrary"),
            vmem_limit_bytes=VMEM_LIMIT),
        name="in_proj",
    )(x2, norm_w.reshape(1, k), w_bf, *later_weights)
    return outs[0], outs[1], outs[2:]


def _hgrn_tables():
    t = np.arange(CHUNK)[:, None]
    s = np.arange(CHUNK)[None, :]
    mats, masks = [], []
    for l in range(N_LEVELS):
        h = 1 << l
        start = (t // (2 * h)) * (2 * h)
        ref = start + h - 1
        is_q = (t - start) >= h
        w = np.where(is_q, (s > ref) & (s <= t), (s > t) & (s <= ref))
        mats.append(w)
        s_start = (s // (2 * h)) * (2 * h)
        masks.append((start == s_start) & is_q & ((s - s_start) < h))
    mats.append(s <= t)
    mats.append(s > t)
    masks.append(t == s)
    w = np.concatenate(mats, axis=0).astype(np.float32)
    w2 = np.concatenate([w, w], axis=1)
    return w2, np.stack(masks).astype(np.float32)


def _hgrn_kernel(hq_ref, hf_ref, hi_ref, hg_ref, lbl_ref, nw_ref, wexp_ref, mask_ref,
                 y_ref, st_ref, *, n_chunks):
    @pl.when(pl.program_id(1) == 0)
    def _():
        st_ref[...] = jnp.zeros_like(st_ref)

    logits = lbl_ref[...]
    l0, l1 = logits[0:1, :], logits[1:2, :]
    mx = jnp.maximum(l0, l1)
    e0, e1 = jnp.exp(l0 - mx), jnp.exp(l1 - mx)
    lb = e0 / (e0 + e1)
    nw = nw_ref[...]
    wexp = wexp_ref[...]
    n_exp = (N_LEVELS + 2) * CHUNK

    def chunk_body(c, carry):
        r = pl.ds(pl.multiple_of(c * CHUNK, CHUNK), CHUNK)
        f = jax.nn.sigmoid(hf_ref[r, :])
        g = lb + (1.0 - lb) * f
        lg = jnp.log(g)
        kk = 1.0 - g
        q = jax.nn.silu(hq_ref[r, :].astype(F32)) * (HG_DK ** -0.5)
        lg_hi = lg.astype(BF16)
        lg_lo = (lg - lg_hi.astype(F32)).astype(BF16)
        ex = jnp.exp(_dot(wexp, jnp.concatenate([lg_hi, lg_lo], axis=0)))
        for h in range(HG_HEADS):
            sl = slice(h * HG_DK, (h + 1) * HG_DK)
            qh, kh = q[:, sl], kk[:, sl]
            vh = hi_ref[r, sl]
            sc = mask_ref[N_LEVELS] * _dot(qh.astype(BF16), kh.astype(BF16), NT)
            for l in range(N_LEVELS):
                xl = ex[l * CHUNK:(l + 1) * CHUNK, sl]
                sc = sc + mask_ref[l] * _dot((qh * xl).astype(BF16), (kh * xl).astype(BF16), NT)
            eb = ex[N_LEVELS * CHUNK:(N_LEVELS + 1) * CHUNK, sl]
            er = ex[(N_LEVELS + 1) * CHUNK:n_exp, sl]
            st = st_ref[h]
            o = _dot(sc.astype(BF16), vh) + _dot((qh * eb).astype(BF16), st.astype(BF16), NT)
            d_last = eb[CHUNK - 1:CHUNK, :]
            st_ref[h] = d_last * st + _dot(vh, (kh * er).astype(BF16), TN)
            y = _rms(o, nw) * jax.nn.silu(hg_ref[r, sl].astype(F32))
            y_ref[r, sl] = y.astype(y_ref.dtype)
        return carry

    lax.fori_loop(0, n_chunks, chunk_body, 0, unroll=True)


def _hgrn(z, hf, lb_logits, hg_norm_w, batch, seq, *, tb=256):
    wexp, masks = _hgrn_tables()
    nblk = seq // tb
    wcol = HG_WIDTH

    def zspec(col):
        return pl.BlockSpec((tb, wcol), lambda b, i, col=col: (b * nblk + i, col))

    return pl.pallas_call(
        functools.partial(_hgrn_kernel, n_chunks=tb // CHUNK),
        out_shape=jax.ShapeDtypeStruct((batch * seq, HG_WIDTH), BF16),
        grid=(batch, nblk),
        in_specs=[
            zspec(0), zspec(0), zspec(2), zspec(3),
            pl.BlockSpec((2, HG_WIDTH), lambda b, i: (0, 0)),
            pl.BlockSpec((1, HG_DV), lambda b, i: (0, 0)),
            pl.BlockSpec(wexp.shape, lambda b, i: (0, 0)),
            pl.BlockSpec(masks.shape, lambda b, i: (0, 0, 0)),
        ],
        out_specs=pl.BlockSpec((tb, HG_WIDTH), lambda b, i: (b * nblk + i, 0)),
        scratch_shapes=[pltpu.VMEM((HG_HEADS, HG_DV, HG_DK), F32)],
        compiler_params=pltpu.CompilerParams(
            dimension_semantics=("parallel", "arbitrary"),
            vmem_limit_bytes=VMEM_LIMIT),
        name="hgrn2",
    )(z, hf, z, z, lb_logits, hg_norm_w.reshape(1, HG_DV),
      jnp.asarray(wexp, BF16), jnp.asarray(masks))


ATT_TK = ATT_TQ + LEFT_CHUNKS * CHUNK
ATT_ROLL = 1024
assert ATT_TQ - 1 + ATT_TK <= ATT_ROLL and ATT_TK - ATT_TQ == 2 * REL_CLIP


def _rel_table(rel_bias):
    rev = rel_bias[:, ::-1].astype(F32)
    edge = jnp.broadcast_to(rev[:, :1], (rel_bias.shape[0], REL_CLIP))
    g = jnp.concatenate([edge, rev[:, :2 * REL_CLIP], edge], axis=1)
    return g.reshape(AT_HEADS // 2, 2, ATT_ROLL)


def _attn_kernel(q_ref, k_ref, v_ref, g_ref, o_ref, bias_ref, vp_ref, *, seq):
    @pl.when(pl.program_id(1) == 0)
    def _():
        qc = lax.broadcasted_iota(jnp.int32, (ATT_TQ, ATT_TK), 0) // CHUNK
        kc = lax.broadcasted_iota(jnp.int32, (ATT_TQ, ATT_TK), 1) // CHUNK
        in_band = (kc >= qc) & (kc <= qc + LEFT_CHUNKS)
        g = g_ref[0]
        for hh in range(2):
            tbl = jnp.broadcast_to(g[hh:hh + 1, :], (ATT_TQ, ATT_ROLL))
            toep = pltpu.roll(tbl, 0, 1, stride=1, stride_axis=0)[:, :ATT_TK]
            bias_ref[hh * ATT_TQ:(hh + 1) * ATT_TQ, :] = jnp.where(in_band, toep, -jnp.inf)

    v_all = v_ref[...]
    head0_all = lax.broadcasted_iota(jnp.int32, v_all.shape, 1) < AT_DH
    ones = jnp.ones_like(v_all)
    vp_ref[0] = jnp.where(head0_all, v_all, ones)
    vp_ref[1] = jnp.where(head0_all, ones, v_all)

    head0 = lax.broadcasted_iota(jnp.int32, (ATT_TQ, LANES), 1) < AT_DH
    for i in range(seq // ATT_TQ):
        q = q_ref[i * ATT_TQ:(i + 1) * ATT_TQ, :] * (AT_DH ** -0.5)
        zero = jnp.zeros_like(q)
        qq = jnp.concatenate([jnp.where(head0, q, zero), jnp.where(head0, zero, q)], axis=0)
        k0 = max(0, (i + 1) * ATT_TQ - ATT_TK)
        k1 = (i + 1) * ATT_TQ
        c0 = ATT_TK - (k1 - k0)
        s = _dot(qq, k_ref[k0:k1, :], NT) + bias_ref[:, c0:]
        p = jnp.exp(s - jnp.max(s, axis=-1, keepdims=True)).astype(BF16)
        acc0 = _dot(p[:ATT_TQ], vp_ref[0, k0:k1, :])
        acc1 = _dot(p[ATT_TQ:], vp_ref[1, k0:k1, :])
        num = jnp.where(head0, acc0, acc1)
        den = pltpu.roll(jnp.where(head0, acc1, acc0), AT_DH, 1)
        o_ref[i * ATT_TQ:(i + 1) * ATT_TQ, :] = (num / den).astype(o_ref.dtype)


def _attn(z, rel_bias, batch, seq):
    g = _rel_table(rel_bias)
    col0 = 4 * HG_WIDTH // LANES
    ncol = AT_WIDTH // LANES

    def zspec(which):
        return pl.BlockSpec((seq, LANES), lambda hp, b, which=which: (b, col0 + which * ncol + hp))

    return pl.pallas_call(
        functools.partial(_attn_kernel, seq=seq),
        out_shape=jax.ShapeDtypeStruct((batch * seq, AT_WIDTH), BF16),
        grid=(AT_HEADS // 2, batch),
        in_specs=[
            zspec(0), zspec(1), zspec(2),
            pl.BlockSpec((1, 2, ATT_ROLL), lambda hp, b: (hp, 0, 0)),
        ],
        out_specs=pl.BlockSpec((seq, LANES), lambda hp, b: (b, hp)),
        scratch_shapes=[pltpu.VMEM((2 * ATT_TQ, ATT_TK), F32),
                        pltpu.VMEM((2, seq, LANES), BF16)],
        compiler_params=pltpu.CompilerParams(
            dimension_semantics=("parallel", "arbitrary"),
            vmem_limit_bytes=VMEM_LIMIT),
        name="band_attn",
    )(z, z, z, g)


def _merge_out_kernel(ya_ref, yb_ref, ga0_ref, ga1_ref, gb0_ref, gb1_ref, x_ref,
                      wa_ref, wb_ref, wo_ref, h_ref, *, rc):
    for c in range(h_ref.shape[0] // rc):
        r = slice(c * rc, (c + 1) * rc)
        pa = _dot(ya_ref[r, :], wa_ref[...])
        pb = _dot(yb_ref[r, :], wb_ref[...])
        ga = jnp.concatenate([ga0_ref[r, :], ga1_ref[r, :]], axis=1).astype(F32)
        gb = jnp.concatenate([gb0_ref[r, :], gb1_ref[r, :]], axis=1).astype(F32)
        merged = (jax.nn.sigmoid(ga) * pa + jax.nn.sigmoid(gb) * pb).astype(BF16)
        h_ref[r, :] = x_ref[r, :] + _dot(merged, wo_ref[...])


def _merge_out(ya, yb, z, x2, wa_bf, wb_bf, wo_bf, *, tm=512, rc=256):
    m, d = x2.shape
    gw = d // 2
    gcol = (4 * HG_WIDTH + 3 * AT_WIDTH) // gw
    assert gcol * gw == 4 * HG_WIDTH + 3 * AT_WIDTH

    def resident(shape):
        return pl.BlockSpec(shape, lambda i: (0, 0), pipeline_mode=pl.Buffered(1))

    def gate(blk):
        return pl.BlockSpec((tm, gw), lambda i, blk=blk: (i, gcol + blk))

    return pl.pallas_call(
        functools.partial(_merge_out_kernel, rc=rc),
        out_shape=jax.ShapeDtypeStruct((m, d), F32),
        grid=(m // tm,),
        in_specs=[
            pl.BlockSpec((tm, HG_WIDTH), lambda i: (i, 0)),
            pl.BlockSpec((tm, AT_WIDTH), lambda i: (i, 0)),
            gate(0), gate(1), gate(2), gate(3),
            pl.BlockSpec((tm, d), lambda i: (i, 0)),
            resident(wa_bf.shape), resident(wb_bf.shape), resident(wo_bf.shape),
        ],
        out_specs=pl.BlockSpec((tm, d), lambda i: (i, 0)),
        compiler_params=pltpu.CompilerParams(
            dimension_semantics=("parallel",),
            vmem_limit_bytes=VMEM_LIMIT),
        name="merge_out",
    )(ya, yb, z, z, z, z, x2, wa_bf, wb_bf, wo_bf)


def _mlp_kernel(h_ref, nw_ref, wu_ref, wd_ref, fw_ref, o_ref, u_ref):
    f = pl.program_id(1)

    @pl.when(f == 0)
    def _():
        u_ref[...] = _rms(h_ref[...], nw_ref[...]).astype(BF16)
        o_ref[...] = jnp.zeros_like(o_ref)

    a = jnp.maximum(_dot(u_ref[...], wu_ref[...]), 0.0)
    o_ref[...] += _dot((a * a).astype(BF16), wd_ref[...])

    @pl.when(f == pl.num_programs(1) - 1)
    def _():
        o_ref[...] = _rms(h_ref[...] + o_ref[...], fw_ref[...])


def _mlp(h, norm_w, wu_bf, wd_bf, final_w, *, tm=1024, tf=512):
    m, d = h.shape
    dff = wu_bf.shape[1]
    return pl.pallas_call(
        _mlp_kernel,
        out_shape=jax.ShapeDtypeStruct((m, d), F32),
        grid=(m // tm, dff // tf),
        in_specs=[
            pl.BlockSpec((tm, d), lambda i, f: (i, 0)),
            pl.BlockSpec((1, d), lambda i, f: (0, 0)),
            pl.BlockSpec((d, tf), lambda i, f: (0, f)),
            pl.BlockSpec((tf, d), lambda i, f: (f, 0)),
            pl.BlockSpec((1, d), lambda i, f: (0, 0)),
        ],
        out_specs=pl.BlockSpec((tm, d), lambda i, f: (i, 0)),
        scratch_shapes=[pltpu.VMEM((tm, d), BF16)],
        compiler_params=pltpu.CompilerParams(
            dimension_semantics=("parallel", "arbitrary"),
            vmem_limit_bytes=VMEM_LIMIT),
        name="mlp",
    )(h, norm_w.reshape(1, d), wu_bf, wd_bf, final_w.reshape(1, d))


def kernel(x, w_in, lb_logits, hg_norm_w, rel_bias, w_branch_a, w_branch_b, w_out,
           norm_mix_w, norm_mlp_w, w_up, w_down, norm_final_w):
    batch, seq, d = x.shape
    assert d == D_MODEL and seq % ATT_TQ == 0 and w_in.shape[0] == 1
    x2 = x.reshape(batch * seq, d)
    z, hf, (wa_bf, wb_bf, wo_bf, wu_bf, wd_bf) = _in_proj(
        x2, norm_mix_w[0], w_in[0].astype(BF16),
        (w_branch_a[0], w_branch_b[0], w_out[0], w_up[0], w_down[0]))
    ya = _hgrn(z, hf, lb_logits, hg_norm_w[0], batch, seq)
    yb = _attn(z, rel_bias[0], batch, seq)
    h = _merge_out(ya, yb, z, x2, wa_bf, wb_bf, wo_bf)
    out = _mlp(h, norm_mlp_w[0], wu_bf, wd_bf, norm_final_w)
    return out.reshape(batch, seq, d)
```

```python
import functools

import jax
import jax.numpy as jnp
import numpy as np
from jax import lax
from jax.experimental import pallas as pl
from jax.experimental.pallas import tpu as pltpu

D_MODEL = 2048
CHUNK = 64
HG_HEADS = 8
HG_DK = 128
HG_DV = 128
HG_WIDTH = HG_HEADS * HG_DV
AT_HEADS = 16
AT_DH = 64
AT_WIDTH = AT_HEADS * AT_DH
LEFT_CHUNKS = 8
REL_CLIP = 256
EPS = 1e-6
D_IN = 4 * HG_WIDTH + 3 * AT_WIDTH + 2 * D_MODEL

LANES = 128
N_LEVELS = 6
ATT_TQ = 2 * CHUNK

BF16 = jnp.bfloat16
F32 = jnp.float32
VMEM_LIMIT = 56 * 1024 * 1024

NN = (((1,), (0,)), ((), ()))
NT = (((1,), (1,)), ((), ()))
TN = (((0,), (0,)), ((), ()))


def _dot(a, b, dims=NN):
    return lax.dot_general(a, b, dims, preferred_element_type=F32)


def _rms(xf, w):
    return xf * lax.rsqrt(jnp.mean(xf * xf, axis=-1, keepdims=True) + EPS) * w


CAST_BLOCKS = 64


def _in_proj_kernel(x_ref, nw_ref, w_ref, *rest, hf_lo, hf_hi, n_cast):
    cast_in, rest = rest[:n_cast], rest[n_cast:]
    z_ref, hf_ref = rest[:2]
    cast_out, u_ref = rest[2:2 + n_cast], rest[2 + n_cast]
    j = pl.program_id(1)

    @pl.when(j == 0)
    def _():
        u_ref[...] = _rms(x_ref[...], nw_ref[...]).astype(BF16)

    acc = _dot(u_ref[...], w_ref[...])
    z_ref[...] = acc.astype(z_ref.dtype)

    @pl.when((j >= hf_lo) & (j < hf_hi))
    def _():
        hf_ref[...] = acc

    for src, dst in zip(cast_in, cast_out):
        dst[...] = src[...].astype(BF16)


def _in_proj(x2, norm_w, w_bf, later_weights, *, tm=1024, tn=1024):
    m, k = x2.shape
    n = w_bf.shape[1]
    nj = n // tn
    assert (m // tm) * nj >= CAST_BLOCKS
    hf_lo, hf_hi = HG_WIDTH // tn, 2 * HG_WIDTH // tn

    def cast_spec(w):
        rows, cols = w.shape
        return pl.BlockSpec((rows // CAST_BLOCKS, cols),
                            lambda i, j: (jnp.minimum(i * nj + j, CAST_BLOCKS - 1), 0))

    cast_specs = [cast_spec(w) for w in later_weights]
    outs = pl.pallas_call(
        functools.partial(_in_proj_kernel, hf_lo=hf_lo, hf_hi=hf_hi, n_cast=len(later_weights)),
        out_shape=(jax.ShapeDtypeStruct((m, n), BF16),
                   jax.ShapeDtypeStruct((m, HG_WIDTH), F32),
                   *[jax.ShapeDtypeStruct(w.shape, BF16) for w in later_weights]),
        grid=(m // tm, nj),
        in_specs=[
            pl.BlockSpec((tm, k), lambda i, j: (i, 0)),
            pl.BlockSpec((1, k), lambda i, j: (0, 0)),
            pl.BlockSpec((k, tn), lambda i, j: (0, j)),
            *cast_specs,
        ],
        out_specs=(
            pl.BlockSpec((tm, tn), lambda i, j: (i, j)),
            pl.BlockSpec((tm, tn), lambda i, j: (i, jnp.clip(j - hf_lo, 0, hf_hi - hf_lo - 1))),
            *cast_specs,
        ),
        scratch_shapes=[pltpu.VMEM((tm, k), BF16)],
        compiler_params=pltpu.CompilerParams(
            dimension_semantics=("arbitrary", "arbitrary"),
            vmem_limit_bytes=VMEM_LIMIT),
        name="in_proj",
    )(x2, norm_w.reshape(1, k), w_bf, *later_weights)
    return outs[0], outs[1], outs[2:]


def _hgrn_tables():
    t = np.arange(CHUNK)[:, None]
    s = np.arange(CHUNK)[None, :]
    mats, masks = [], []
    for l in range(N_LEVELS):
        h = 1 << l
        start = (t // (2 * h)) * (2 * h)
        ref = start + h - 1
        is_q = (t - start) >= h
        w = np.where(is_q, (s > ref) & (s <= t), (s > t) & (s <= ref))
        mats.append(w)
        s_start = (s // (2 * h)) * (2 * h)
        masks.append((start == s_start) & is_q & ((s - s_start) < h))
    mats.append(s <= t)
    mats.append(s > t)
    masks.append(t == s)
    w = np.concatenate(mats, axis=0).astype(np.float32)
    w2 = np.concatenate([w, w], axis=1)
    return w2, np.stack(masks).astype(np.float32)


def _hgrn_kernel(hq_ref, hf_ref, hi_ref, hg_ref, lbl_ref, nw_ref, wexp_ref, mask_ref,
                 y_ref, st_ref, *, n_chunks):
    @pl.when(pl.program_id(1) == 0)
    def _():
        st_ref[...] = jnp.zeros_like(st_ref)

    logits = lbl_ref[...]
    l0, l1 = logits[0:1, :], logits[1:2, :]
    mx = jnp.maximum(l0, l1)
    e0, e1 = jnp.exp(l0 - mx), jnp.exp(l1 - mx)
    lb = e0 / (e0 + e1)
    nw = nw_ref[...]
    wexp = wexp_ref[...]
    n_exp = (N_LEVELS + 2) * CHUNK

    def chunk_body(c, carry):
        r = pl.ds(pl.multiple_of(c * CHUNK, CHUNK), CHUNK)
        f = jax.nn.sigmoid(hf_ref[r, :])
        g = lb + (1.0 - lb) * f
        lg = jnp.log(g)
        kk = 1.0 - g
        q = jax.nn.silu(hq_ref[r, :].astype(F32)) * (HG_DK ** -0.5)
        lg_hi = lg.astype(BF16)
        lg_lo = (lg - lg_hi.astype(F32)).astype(BF16)
        ex = jnp.exp(_dot(wexp, jnp.concatenate([lg_hi, lg_lo], axis=0)))
        for h in range(HG_HEADS):
            sl = slice(h * HG_DK, (h + 1) * HG_DK)
            qh, kh = q[:, sl], kk[:, sl]
            vh = hi_ref[r, sl]
            sc = mask_ref[N_LEVELS] * _dot(qh.astype(BF16), kh.astype(BF16), NT)
            for l in range(N_LEVELS):
                xl = ex[l * CHUNK:(l + 1) * CHUNK, sl]
                sc = sc + mask_ref[l] * _dot((qh * xl).astype(BF16), (kh * xl).astype(BF16), NT)
            eb = ex[N_LEVELS * CHUNK:(N_LEVELS + 1) * CHUNK, sl]
            er = ex[(N_LEVELS + 1) * CHUNK:n_exp, sl]
            st = st_ref[h]
            o = _dot(sc.astype(BF16), vh) + _dot((qh * eb).astype(BF16), st.astype(BF16), NT)
            d_last = eb[CHUNK - 1:CHUNK, :]
            st_ref[h] = d_last * st + _dot(vh, (kh * er).astype(BF16), TN)
            y = _rms(o, nw) * jax.nn.silu(hg_ref[r, sl].astype(F32))
            y_ref[r, sl] = y.astype(y_ref.dtype)
        return carry

    lax.fori_loop(0, n_chunks, chunk_body, 0, unroll=True)


def _hgrn(z, hf, lb_logits, hg_norm_w, batch, seq, *, tb=256):
    wexp, masks = _hgrn_tables()
    nblk = seq // tb
    wcol = HG_WIDTH

    def zspec(col):
        return pl.BlockSpec((tb, wcol), lambda b, i, col=col: (b * nblk + i, col))

    return pl.pallas_call(
        functools.partial(_hgrn_kernel, n_chunks=tb // CHUNK),
        out_shape=jax.ShapeDtypeStruct((batch * seq, HG_WIDTH), BF16),
        grid=(batch, nblk),
        in_specs=[
            zspec(0), zspec(0), zspec(2), zspec(3),
            pl.BlockSpec((2, HG_WIDTH), lambda b, i: (0, 0)),
            pl.BlockSpec((1, HG_DV), lambda b, i: (0, 0)),
            pl.BlockSpec(wexp.shape, lambda b, i: (0, 0)),
            pl.BlockSpec(masks.shape, lambda b, i: (0, 0, 0)),
        ],
        out_specs=pl.BlockSpec((tb, HG_WIDTH), lambda b, i: (b * nblk + i, 0)),
        scratch_shapes=[pltpu.VMEM((HG_HEADS, HG_DV, HG_DK), F32)],
        compiler_params=pltpu.CompilerParams(
            dimension_semantics=("parallel", "arbitrary"),
            vmem_limit_bytes=VMEM_LIMIT),
        name="hgrn2",
    )(z, hf, z, z, lb_logits, hg_norm_w.reshape(1, HG_DV),
      jnp.asarray(wexp, BF16), jnp.asarray(masks))


ATT_TK = ATT_TQ + LEFT_CHUNKS * CHUNK
ATT_ROLL = 1024
assert ATT_TQ - 1 + ATT_TK <= ATT_ROLL and ATT_TK - ATT_TQ == 2 * REL_CLIP


def _rel_table(rel_bias):
    rev = rel_bias[:, ::-1].astype(F32)
    edge = jnp.broadcast_to(rev[:, :1], (rel_bias.shape[0], REL_CLIP))
    g = jnp.concatenate([edge, rev[:, :2 * REL_CLIP], edge], axis=1)
    return g.reshape(AT_HEADS // 2, 2, ATT_ROLL)


ATT_PAD = ATT_TK - ATT_TQ
ATT_EDGE = ATT_PAD // ATT_TQ


ATT_RG = 32


def _attn_kernel(q_ref, k_ref, v_ref, g_ref, o_ref, bias_ref, kp_ref, vp_ref, s_ref, p_ref, *, seq):
    @pl.when(pl.program_id(1) == 0)
    def _():
        qc = lax.broadcasted_iota(jnp.int32, (ATT_TQ, ATT_TK), 0) // CHUNK
        kc = lax.broadcasted_iota(jnp.int32, (ATT_TQ, ATT_TK), 1) // CHUNK
        in_band = (kc >= qc) & (kc <= qc + LEFT_CHUNKS)
        g = g_ref[0]
        for hh in range(2):
            tbl = jnp.broadcast_to(g[hh:hh + 1, :], (ATT_TQ, ATT_ROLL))
            toep = pltpu.roll(tbl, 0, 1, stride=1, stride_axis=0)[:, :ATT_TK]
            rows = slice(hh * ATT_TQ, (hh + 1) * ATT_TQ)
            for n in range(ATT_EDGE + 1):
                first_chunk = (ATT_PAD - n * ATT_TQ) // CHUNK if n < ATT_EDGE else 0
                bias_ref[n, rows, :] = jnp.where(in_band & (kc >= first_chunk), toep, -jnp.inf)

    v_all = v_ref[...]
    head0_all = lax.broadcasted_iota(jnp.int32, v_all.shape, 1) < AT_DH
    ones = jnp.ones_like(v_all)
    pad = jnp.zeros((ATT_PAD, LANES), BF16)
    kp_ref[:ATT_PAD, :] = pad
    kp_ref[ATT_PAD:, :] = k_ref[...]
    vp_ref[0, :ATT_PAD, :] = pad
    vp_ref[1, :ATT_PAD, :] = pad
    vp_ref[0, ATT_PAD:, :] = jnp.where(head0_all, v_all, ones)
    vp_ref[1, ATT_PAD:, :] = jnp.where(head0_all, ones, v_all)

    head0 = lax.broadcasted_iota(jnp.int32, (ATT_TQ, LANES), 1) < AT_DH
    n_blocks = seq // ATT_TQ

    def scores(i, slot):
        r0 = pl.multiple_of(i * ATT_TQ, ATT_TQ)
        q = q_ref[pl.ds(r0, ATT_TQ), :] * (AT_DH ** -0.5)
        zero = jnp.zeros_like(q)
        qq = jnp.concatenate([jnp.where(head0, q, zero), jnp.where(head0, zero, q)], axis=0)
        s = _dot(qq, kp_ref[pl.ds(r0, ATT_TK), :], NT)
        s_ref[slot] = s + bias_ref[jnp.minimum(i, ATT_EDGE)]

    def outputs(i, slot):
        r0 = pl.multiple_of(i * ATT_TQ, ATT_TQ)
        for g in range(2 * ATT_TQ // ATT_RG):
            rows = slice(g * ATT_RG, (g + 1) * ATT_RG)
            s = s_ref[slot, rows, :]
            p_ref[slot, rows, :] = jnp.exp(s - jnp.max(s, axis=-1, keepdims=True)).astype(BF16)
        acc0 = _dot(p_ref[slot, :ATT_TQ, :], vp_ref[0, pl.ds(r0, ATT_TK), :])
        acc1 = _dot(p_ref[slot, ATT_TQ:, :], vp_ref[1, pl.ds(r0, ATT_TK), :])
        num = jnp.where(head0, acc0, acc1)
        den = pltpu.roll(jnp.where(head0, acc1, acc0), AT_DH, 1)
        o_ref[pl.ds(r0, ATT_TQ), :] = (num / den).astype(o_ref.dtype)

    assert n_blocks % 2 == 0
    scores(0, 0)

    def body(t, carry):
        i = 2 * t
        scores(i + 1, 1)
        outputs(i, 0)
        scores(i + 2, 0)
        outputs(i + 1, 1)
        return carry

    lax.fori_loop(0, n_blocks // 2 - 1, body, 0, unroll=True)
    scores(n_blocks - 1, 1)
    outputs(n_blocks - 2, 0)
    outputs(n_blocks - 1, 1)


def _attn(z, rel_bias, batch, seq):
    g = _rel_table(rel_bias)
    col0 = 4 * HG_WIDTH // LANES
    ncol = AT_WIDTH // LANES

    def zspec(which):
        return pl.BlockSpec((seq, LANES), lambda hp, b, which=which: (b, col0 + which * ncol + hp))

    return pl.pallas_call(
        functools.partial(_attn_kernel, seq=seq),
        out_shape=jax.ShapeDtypeStruct((batch * seq, AT_WIDTH), BF16),
        grid=(AT_HEADS // 2, batch),
        in_specs=[
            zspec(0), zspec(1), zspec(2),
            pl.BlockSpec((1, 2, ATT_ROLL), lambda hp, b: (hp, 0, 0)),
        ],
        out_specs=pl.BlockSpec((seq, LANES), lambda hp, b: (b, hp)),
        scratch_shapes=[pltpu.VMEM((ATT_EDGE + 1, 2 * ATT_TQ, ATT_TK), F32),
                        pltpu.VMEM((ATT_PAD + seq, LANES), BF16),
                        pltpu.VMEM((2, ATT_PAD + seq, LANES), BF16),
                        pltpu.VMEM((2, 2 * ATT_TQ, ATT_TK), F32),
                        pltpu.VMEM((2, 2 * ATT_TQ, ATT_TK), BF16)],
        compiler_params=pltpu.CompilerParams(
            dimension_semantics=("parallel", "arbitrary"),
            vmem_limit_bytes=VMEM_LIMIT),
        name="band_attn",
    )(z, z, z, g)


def _merge_out_kernel(ya_ref, yb_ref, ga0_ref, ga1_ref, gb0_ref, gb1_ref, x_ref,
                      wa_ref, wb_ref, wo_ref, h_ref, *, rc):
    for c in range(h_ref.shape[0] // rc):
        r = slice(c * rc, (c + 1) * rc)
        pa = _dot(ya_ref[r, :], wa_ref[...])
        pb = _dot(yb_ref[r, :], wb_ref[...])
        ga = jnp.concatenate([ga0_ref[r, :], ga1_ref[r, :]], axis=1).astype(F32)
        gb = jnp.concatenate([gb0_ref[r, :], gb1_ref[r, :]], axis=1).astype(F32)
        merged = (jax.nn.sigmoid(ga) * pa + jax.nn.sigmoid(gb) * pb).astype(BF16)
        h_ref[r, :] = x_ref[r, :] + _dot(merged, wo_ref[...])


def _merge_out(ya, yb, z, x2, wa_bf, wb_bf, wo_bf, *, tm=512, rc=256):
    m, d = x2.shape
    gw = d // 2
    gcol = (4 * HG_WIDTH + 3 * AT_WIDTH) // gw
    assert gcol * gw == 4 * HG_WIDTH + 3 * AT_WIDTH

    def resident(shape):
        return pl.BlockSpec(shape, lambda i: (0, 0), pipeline_mode=pl.Buffered(1))

    def gate(blk):
        return pl.BlockSpec((tm, gw), lambda i, blk=blk: (i, gcol + blk))

    return pl.pallas_call(
        functools.partial(_merge_out_kernel, rc=rc),
        out_shape=jax.ShapeDtypeStruct((m, d), F32),
        grid=(m // tm,),
        in_specs=[
            pl.BlockSpec((tm, HG_WIDTH), lambda i: (i, 0)),
            pl.BlockSpec((tm, AT_WIDTH), lambda i: (i, 0)),
            gate(0), gate(1), gate(2), gate(3),
            pl.BlockSpec((tm, d), lambda i: (i, 0)),
            resident(wa_bf.shape), resident(wb_bf.shape), resident(wo_bf.shape),
        ],
        out_specs=pl.BlockSpec((tm, d), lambda i: (i, 0)),
        compiler_params=pltpu.CompilerParams(
            dimension_semantics=("parallel",),
            vmem_limit_bytes=VMEM_LIMIT),
        name="merge_out",
    )(ya, yb, z, z, z, z, x2, wa_bf, wb_bf, wo_bf)


def _mlp_kernel(h_ref, nw_ref, wu_ref, wd_ref, fw_ref, o_ref, u_ref):
    f = pl.program_id(1)

    @pl.when(f == 0)
    def _():
        u_ref[...] = _rms(h_ref[...], nw_ref[...]).astype(BF16)
        o_ref[...] = jnp.zeros_like(o_ref)

    a = jnp.maximum(_dot(u_ref[...], wu_ref[...]), 0.0)
    o_ref[...] += _dot((a * a).astype(BF16), wd_ref[...])

    @pl.when(f == pl.num_programs(1) - 1)
    def _():
        o_ref[...] = _rms(h_ref[...] + o_ref[...], fw_ref[...])


def _mlp(h, norm_w, wu_bf, wd_bf, final_w, *, tm=1024, tf=512):
    m, d = h.shape
    dff = wu_bf.shape[1]
    return pl.pallas_call(
        _mlp_kernel,
        out_shape=jax.ShapeDtypeStruct((m, d), F32),
        grid=(m // tm, dff // tf),
        in_specs=[
            pl.BlockSpec((tm, d), lambda i, f: (i, 0)),
            pl.BlockSpec((1, d), lambda i, f: (0, 0)),
            pl.BlockSpec((d, tf), lambda i, f: (0, f)),
            pl.BlockSpec((tf, d), lambda i, f: (f, 0)),
            pl.BlockSpec((1, d), lambda i, f: (0, 0)),
        ],
        out_specs=pl.BlockSpec((tm, d), lambda i, f: (i, 0)),
        scratch_shapes=[pltpu.VMEM((tm, d), BF16)],
        compiler_params=pltpu.CompilerParams(
            dimension_semantics=("parallel", "arbitrary"),
            vmem_limit_bytes=VMEM_LIMIT),
        name="mlp",
    )(h, norm_w.reshape(1, d), wu_bf, wd_bf, final_w.reshape(1, d))


def kernel(x, w_in, lb_logits, hg_norm_w, rel_bias, w_branch_a, w_branch_b, w_out,
           norm_mix_w, norm_mlp_w, w_up, w_down, norm_final_w):
    batch, seq, d = x.shape
    assert d == D_MODEL and seq % ATT_TQ == 0 and w_in.shape[0] == 1
    x2 = x.reshape(batch * seq, d)
    z, hf, (wa_bf, wb_bf, wo_bf, wu_bf, wd_bf) = _in_proj(
        x2, norm_mix_w[0], w_in[0].astype(BF16),
        (w_branch_a[0], w_branch_b[0], w_out[0], w_up[0], w_down[0]))
    ya = _hgrn(z, hf, lb_logits, hg_norm_w[0], batch, seq)
    yb = _attn(z, rel_bias[0], batch, seq)
    h = _merge_out(ya, yb, z, x2, wa_bf, wb_bf, wo_bf)
    out = _mlp(h, norm_mlp_w[0], wu_bf, wd_bf, norm_final_w)
    return out.reshape(batch, seq, d)
```

```python
import functools

import jax
import jax.numpy as jnp
import numpy as np
from jax import lax
from jax.experimental import pallas as pl
from jax.experimental.pallas import tpu as pltpu

D_MODEL = 2048
CHUNK = 64
HG_HEADS = 8
HG_DK = 128
HG_DV = 128
HG_WIDTH = HG_HEADS * HG_DV
AT_HEADS = 16
AT_DH = 64
AT_WIDTH = AT_HEADS * AT_DH
LEFT_CHUNKS = 8
REL_CLIP = 256
EPS = 1e-6
D_IN = 4 * HG_WIDTH + 3 * AT_WIDTH + 2 * D_MODEL

LANES = 128
N_LEVELS = 6
ATT_TQ = 2 * CHUNK

BF16 = jnp.bfloat16
F32 = jnp.float32
VMEM_LIMIT = 60000 * 1024

NN = (((1,), (0,)), ((), ()))
NT = (((1,), (1,)), ((), ()))
TN = (((0,), (0,)), ((), ()))


def _dot(a, b, dims=NN):
    return lax.dot_general(a, b, dims, preferred_element_type=F32)


def _rms(xf, w):
    return xf * lax.rsqrt(jnp.mean(xf * xf, axis=-1, keepdims=True) + EPS) * w


CAST_BLOCKS = 64
IN_SLAB = 256


def _in_proj_kernel(x_ref, nw_ref, w_ref, *rest, n_cast):
    cast_in, rest = rest[:n_cast], rest[n_cast:]
    z_ref, zf_ref = rest[:2]
    cast_out, u_ref = rest[2:2 + n_cast], rest[2 + n_cast]

    @pl.when(pl.program_id(1) == 0)
    def _():
        u_ref[...] = _rms(x_ref[...], nw_ref[...]).astype(BF16)

    for c in range(z_ref.shape[1] // IN_SLAB):
        cols = slice(c * IN_SLAB, (c + 1) * IN_SLAB)
        acc = _dot(u_ref[...], w_ref[:, cols])
        z_ref[:, cols] = acc.astype(z_ref.dtype)
        zf_ref[:, cols] = acc

    for src, dst in zip(cast_in, cast_out):
        dst[...] = src[...].astype(BF16)


def _in_proj(x2, norm_w, w_bf, later_weights, *, tm=1024, tn=1024):
    m, k = x2.shape
    n = w_bf.shape[1]
    nj = n // tn
    assert (m // tm) * nj >= CAST_BLOCKS and tn == HG_WIDTH
    hf_tile = 1

    def cast_spec(w):
        rows, cols = w.shape
        return pl.BlockSpec((rows // CAST_BLOCKS, cols),
                            lambda i, j: (jnp.minimum(i * nj + j, CAST_BLOCKS - 1), 0))

    cast_specs = [cast_spec(w) for w in later_weights]
    outs = pl.pallas_call(
        functools.partial(_in_proj_kernel, n_cast=len(later_weights)),
        out_shape=(jax.ShapeDtypeStruct((m, n), BF16),
                   jax.ShapeDtypeStruct((m, 2 * tn), F32),
                   *[jax.ShapeDtypeStruct(w.shape, BF16) for w in later_weights]),
        grid=(m // tm, nj),
        in_specs=[
            pl.BlockSpec((tm, k), lambda i, j: (i, 0)),
            pl.BlockSpec((1, k), lambda i, j: (0, 0)),
            pl.BlockSpec((k, tn), lambda i, j: (0, j)),
            *cast_specs,
        ],
        out_specs=(
            pl.BlockSpec((tm, tn), lambda i, j: (i, j)),
            pl.BlockSpec((tm, tn), lambda i, j: (i, jnp.where(j <= hf_tile, 0, 1))),
            *cast_specs,
        ),
        scratch_shapes=[pltpu.VMEM((tm, k), BF16)],
        compiler_params=pltpu.CompilerParams(
            dimension_semantics=("arbitrary", "arbitrary"),
            vmem_limit_bytes=VMEM_LIMIT),
        name="in_proj",
    )(x2, norm_w.reshape(1, k), w_bf, *later_weights)
    return outs[0], outs[1], outs[2:]


MXU_LEVELS = 3


def _hgrn_tables():
    t = np.arange(CHUNK)[:, None]
    s = np.arange(CHUNK)[None, :]
    mats, masks = [], []
    for l in range(N_LEVELS):
        h = 1 << l
        start = (t // (2 * h)) * (2 * h)
        ref = start + h - 1
        is_q = (t - start) >= h
        if l < MXU_LEVELS:
            mats.append(np.where(is_q, (s > ref) & (s <= t), (s > t) & (s <= ref)))
        s_start = (s // (2 * h)) * (2 * h)
        masks.append((start == s_start) & is_q & ((s - s_start) < h))
    mats.append(s <= t)
    masks.append(t == s)
    w = np.concatenate(mats, axis=0).astype(np.float32)
    w2 = np.concatenate([w, w], axis=1)
    return w2, np.stack(masks).astype(np.float32)


def _level_exponent(b, h):
    parts = []
    for p in range(b.shape[0] // (2 * h)):
        lo = p * 2 * h
        parts.append(-jnp.abs(b[lo:lo + 2 * h, :] - b[lo + h - 1:lo + h, :]))
    return parts[0] if len(parts) == 1 else jnp.concatenate(parts, axis=0)


def _hgrn_kernel(hq_ref, hf_ref, hi_ref, hg_ref, lbl_ref, nw_ref, wexp_ref, mask_ref,
                 y_ref, st_ref, *, n_chunks):
    @pl.when(pl.program_id(1) == 0)
    def _():
        st_ref[...] = jnp.zeros_like(st_ref)

    logits = lbl_ref[...]
    l0, l1 = logits[0:1, :], logits[1:2, :]
    mx = jnp.maximum(l0, l1)
    e0, e1 = jnp.exp(l0 - mx), jnp.exp(l1 - mx)
    lb = e0 / (e0 + e1)
    nw = nw_ref[...]
    wexp = wexp_ref[...]

    def chunk_body(c, carry):
        r = pl.ds(pl.multiple_of(c * CHUNK, CHUNK), CHUNK)
        f = jax.nn.sigmoid(hf_ref[r, :])
        g = lb + (1.0 - lb) * f
        lg = jnp.log2(g)
        kk = 1.0 - g
        q = jax.nn.silu(hq_ref[r, :].astype(F32)) * (HG_DK ** -0.5)
        lg_hi = lg.astype(BF16)
        lg_lo = (lg - lg_hi.astype(F32)).astype(BF16)
        e_mxu = _dot(wexp, jnp.concatenate([lg_hi, lg_lo], axis=0))
        b = e_mxu[MXU_LEVELS * CHUNK:, :]
        scale = [jnp.exp2(e_mxu[l * CHUNK:(l + 1) * CHUNK, :]) for l in range(MXU_LEVELS)]
        scale += [jnp.exp2(_level_exponent(b, 1 << l)) for l in range(MXU_LEVELS, N_LEVELS)]
        eb_all = jnp.exp2(b)
        er_all = jnp.exp2(b[CHUNK - 1:CHUNK, :] - b)
        for h in range(HG_HEADS):
            sl = slice(h * HG_DK, (h + 1) * HG_DK)
            qh, kh = q[:, sl], kk[:, sl]
            vh = hi_ref[r, sl]
            sc = mask_ref[N_LEVELS] * _dot(qh.astype(BF16), kh.astype(BF16), NT)
            for l in range(N_LEVELS):
                xl = scale[l][:, sl]
                sc = sc + mask_ref[l] * _dot((qh * xl).astype(BF16), (kh * xl).astype(BF16), NT)
            eb, er = eb_all[:, sl], er_all[:, sl]
            st = st_ref[h]
            o = _dot(sc.astype(BF16), vh) + _dot((qh * eb).astype(BF16), st.astype(BF16), NT)
            d_last = eb[CHUNK - 1:CHUNK, :]
            st_ref[h] = d_last * st + _dot(vh, (kh * er).astype(BF16), TN)
            y = _rms(o, nw) * jax.nn.silu(hg_ref[r, sl].astype(F32))
            y_ref[r, sl] = y.astype(y_ref.dtype)
        return carry

    lax.fori_loop(0, n_chunks, chunk_body, 0, unroll=True)


def _hgrn(z, hf, lb_logits, hg_norm_w, batch, seq, *, tb=256):
    wexp, masks = _hgrn_tables()
    nblk = seq // tb
    wcol = HG_WIDTH

    def zspec(col):
        return pl.BlockSpec((tb, wcol), lambda b, i, col=col: (b * nblk + i, col))

    return pl.pallas_call(
        functools.partial(_hgrn_kernel, n_chunks=tb // CHUNK),
        out_shape=jax.ShapeDtypeStruct((batch * seq, HG_WIDTH), BF16),
        grid=(batch, nblk),
        in_specs=[
            zspec(0), zspec(0), zspec(2), zspec(3),
            pl.BlockSpec((2, HG_WIDTH), lambda b, i: (0, 0)),
            pl.BlockSpec((1, HG_DV), lambda b, i: (0, 0)),
            pl.BlockSpec(wexp.shape, lambda b, i: (0, 0)),
            pl.BlockSpec(masks.shape, lambda b, i: (0, 0, 0)),
        ],
        out_specs=pl.BlockSpec((tb, HG_WIDTH), lambda b, i: (b * nblk + i, 0)),
        scratch_shapes=[pltpu.VMEM((HG_HEADS, HG_DV, HG_DK), F32)],
        compiler_params=pltpu.CompilerParams(
            dimension_semantics=("parallel", "arbitrary"),
            vmem_limit_bytes=VMEM_LIMIT),
        name="hgrn2",
    )(z, hf, z, z, lb_logits, hg_norm_w.reshape(1, HG_DV),
      jnp.asarray(wexp, BF16), jnp.asarray(masks))


ATT_TK = ATT_TQ + LEFT_CHUNKS * CHUNK
ATT_ROLL = 1024
assert ATT_TQ - 1 + ATT_TK <= ATT_ROLL and ATT_TK - ATT_TQ == 2 * REL_CLIP


def _rel_table(rel_bias):
    rev = rel_bias[:, ::-1].astype(F32)
    edge = jnp.broadcast_to(rev[:, :1], (rel_bias.shape[0], REL_CLIP))
    g = jnp.concatenate([edge, rev[:, :2 * REL_CLIP], edge], axis=1)
    return g.reshape(AT_HEADS // 2, 2, ATT_ROLL)


ATT_PAD = ATT_TK - ATT_TQ
ATT_EDGE = ATT_PAD // ATT_TQ


ATT_RG = 32


def _attn_kernel(q_ref, k_ref, v_ref, g_ref, o_ref, bias_ref, kp_ref, vp_ref, s_ref, p_ref, *, seq):
    @pl.when(pl.program_id(1) == 0)
    def _():
        qc = lax.broadcasted_iota(jnp.int32, (ATT_TQ, ATT_TK), 0) // CHUNK
        kc = lax.broadcasted_iota(jnp.int32, (ATT_TQ, ATT_TK), 1) // CHUNK
        in_band = (kc >= qc) & (kc <= qc + LEFT_CHUNKS)
        g = g_ref[0]
        for hh in range(2):
            tbl = jnp.broadcast_to(g[hh:hh + 1, :], (ATT_TQ, ATT_ROLL))
            toep = pltpu.roll(tbl, 0, 1, stride=1, stride_axis=0)[:, :ATT_TK]
            rows = slice(hh * ATT_TQ, (hh + 1) * ATT_TQ)
            for n in range(ATT_EDGE + 1):
                first_chunk = (ATT_PAD - n * ATT_TQ) // CHUNK if n < ATT_EDGE else 0
                bias_ref[n, rows, :] = jnp.where(in_band & (kc >= first_chunk), toep, -jnp.inf)

    v_all = v_ref[...]
    head0_all = lax.broadcasted_iota(jnp.int32, v_all.shape, 1) < AT_DH
    ones = jnp.ones_like(v_all)
    pad = jnp.zeros((ATT_PAD, LANES), BF16)
    kp_ref[:ATT_PAD, :] = pad
    kp_ref[ATT_PAD:, :] = k_ref[...]
    vp_ref[0, :ATT_PAD, :] = pad
    vp_ref[1, :ATT_PAD, :] = pad
    vp_ref[0, ATT_PAD:, :] = jnp.where(head0_all, v_all, ones)
    vp_ref[1, ATT_PAD:, :] = jnp.where(head0_all, ones, v_all)

    head0 = lax.broadcasted_iota(jnp.int32, (ATT_TQ, LANES), 1) < AT_DH
    n_blocks = seq // ATT_TQ

    def scores(i, slot):
        r0 = pl.multiple_of(i * ATT_TQ, ATT_TQ)
        q = q_ref[pl.ds(r0, ATT_TQ), :] * (AT_DH ** -0.5)
        zero = jnp.zeros_like(q)
        qq = jnp.concatenate([jnp.where(head0, q, zero), jnp.where(head0, zero, q)], axis=0)
        s = _dot(qq, kp_ref[pl.ds(r0, ATT_TK), :], NT)
        s_ref[slot] = s + bias_ref[jnp.minimum(i, ATT_EDGE)]

    def outputs(i, slot):
        r0 = pl.multiple_of(i * ATT_TQ, ATT_TQ)
        for g in range(2 * ATT_TQ // ATT_RG):
            rows = slice(g * ATT_RG, (g + 1) * ATT_RG)
            s = s_ref[slot, rows, :]
            p_ref[slot, rows, :] = jnp.exp(s - jnp.max(s, axis=-1, keepdims=True)).astype(BF16)
        acc0 = _dot(p_ref[slot, :ATT_TQ, :], vp_ref[0, pl.ds(r0, ATT_TK), :])
        acc1 = _dot(p_ref[slot, ATT_TQ:, :], vp_ref[1, pl.ds(r0, ATT_TK), :])
        num = jnp.where(head0, acc0, acc1)
        den = pltpu.roll(jnp.where(head0, acc1, acc0), AT_DH, 1)
        o_ref[pl.ds(r0, ATT_TQ), :] = (num / den).astype(o_ref.dtype)

    assert n_blocks % 2 == 0
    scores(0, 0)

    def body(t, carry):
        i = 2 * t
        scores(i + 1, 1)
        outputs(i, 0)
        scores(i + 2, 0)
        outputs(i + 1, 1)
        return carry

    lax.fori_loop(0, n_blocks // 2 - 1, body, 0, unroll=True)
    scores(n_blocks - 1, 1)
    outputs(n_blocks - 2, 0)
    outputs(n_blocks - 1, 1)


def _attn(z, rel_bias, batch, seq):
    g = _rel_table(rel_bias)
    col0 = 4 * HG_WIDTH // LANES
    ncol = AT_WIDTH // LANES

    def zspec(which):
        return pl.BlockSpec((seq, LANES), lambda hp, b, which=which: (b, col0 + which * ncol + hp))

    return pl.pallas_call(
        functools.partial(_attn_kernel, seq=seq),
        out_shape=jax.ShapeDtypeStruct((batch * seq, AT_WIDTH), BF16),
        grid=(AT_HEADS // 2, batch),
        in_specs=[
            zspec(0), zspec(1), zspec(2),
            pl.BlockSpec((1, 2, ATT_ROLL), lambda hp, b: (hp, 0, 0)),
        ],
        out_specs=pl.BlockSpec((seq, LANES), lambda hp, b: (b, hp)),
        scratch_shapes=[pltpu.VMEM((ATT_EDGE + 1, 2 * ATT_TQ, ATT_TK), F32),
                        pltpu.VMEM((ATT_PAD + seq, LANES), BF16),
                        pltpu.VMEM((2, ATT_PAD + seq, LANES), BF16),
                        pltpu.VMEM((2, 2 * ATT_TQ, ATT_TK), F32),
                        pltpu.VMEM((2, 2 * ATT_TQ, ATT_TK), BF16)],
        compiler_params=pltpu.CompilerParams(
            dimension_semantics=("parallel", "arbitrary"),
            vmem_limit_bytes=VMEM_LIMIT),
        name="band_attn",
    )(z, z, z, g)


def _merge_out_kernel(ya_ref, yb_ref, ga0_ref, ga1_ref, gb0_ref, gb1_ref, x_ref,
                      wa_ref, wb_ref, wo_ref, nw_ref, h_ref, u_ref, *, rc):
    for c in range(h_ref.shape[0] // rc):
        r = slice(c * rc, (c + 1) * rc)
        pa = _dot(ya_ref[r, :], wa_ref[...])
        pb = _dot(yb_ref[r, :], wb_ref[...])
        ga = jnp.concatenate([ga0_ref[r, :], ga1_ref[r, :]], axis=1).astype(F32)
        gb = jnp.concatenate([gb0_ref[r, :], gb1_ref[r, :]], axis=1).astype(F32)
        merged = (jax.nn.sigmoid(ga) * pa + jax.nn.sigmoid(gb) * pb).astype(BF16)
        h = x_ref[r, :] + _dot(merged, wo_ref[...])
        h_ref[r, :] = h
        u_ref[r, :] = _rms(h, nw_ref[...]).astype(BF16)


def _merge_out(ya, yb, z, x2, wa_bf, wb_bf, wo_bf, norm_w, *, tm=512, rc=256):
    m, d = x2.shape
    gw = d // 2
    gcol = (4 * HG_WIDTH + 3 * AT_WIDTH) // gw
    assert gcol * gw == 4 * HG_WIDTH + 3 * AT_WIDTH

    def resident(shape):
        return pl.BlockSpec(shape, lambda i: (0, 0), pipeline_mode=pl.Buffered(1))

    def gate(blk):
        return pl.BlockSpec((tm, gw), lambda i, blk=blk: (i, gcol + blk))

    return pl.pallas_call(
        functools.partial(_merge_out_kernel, rc=rc),
        out_shape=(jax.ShapeDtypeStruct((m, d), F32), jax.ShapeDtypeStruct((m, d), BF16)),
        grid=(m // tm,),
        in_specs=[
            pl.BlockSpec((tm, HG_WIDTH), lambda i: (i, 0)),
            pl.BlockSpec((tm, AT_WIDTH), lambda i: (i, 0)),
            gate(0), gate(1), gate(2), gate(3),
            pl.BlockSpec((tm, d), lambda i: (i, 0)),
            resident(wa_bf.shape), resident(wb_bf.shape), resident(wo_bf.shape),
            resident((1, d)),
        ],
        out_specs=(pl.BlockSpec((tm, d), lambda i: (i, 0)), pl.BlockSpec((tm, d), lambda i: (i, 0))),
        compiler_params=pltpu.CompilerParams(
            dimension_semantics=("parallel",),
            vmem_limit_bytes=VMEM_LIMIT),
        name="merge_out",
    )(ya, yb, z, z, z, z, x2, wa_bf, wb_bf, wo_bf, norm_w.reshape(1, d))


MLP_SLAB = 512


def _mlp_kernel(h_ref, u_ref, wu_ref, wd_ref, fw_ref, o_ref):
    f = pl.program_id(1)

    @pl.when(f == 0)
    def _():
        o_ref[...] = jnp.zeros_like(o_ref)

    a = jnp.maximum(_dot(u_ref[...], wu_ref[...]), 0.0)
    a = (a * a).astype(BF16)
    for c in range(o_ref.shape[1] // MLP_SLAB):
        cols = slice(c * MLP_SLAB, (c + 1) * MLP_SLAB)
        o_ref[:, cols] += _dot(a, wd_ref[:, cols])

    @pl.when(f == pl.num_programs(1) - 1)
    def _():
        o_ref[...] = _rms(h_ref[...] + o_ref[...], fw_ref[...])


def _mlp(h, u, wu_bf, wd_bf, final_w, *, tm=1024, tf=512):
    m, d = h.shape
    dff = wu_bf.shape[1]
    return pl.pallas_call(
        _mlp_kernel,
        out_shape=jax.ShapeDtypeStruct((m, d), F32),
        grid=(m // tm, dff // tf),
        in_specs=[
            pl.BlockSpec((tm, d), lambda i, f: (i, 0)),
            pl.BlockSpec((tm, d), lambda i, f: (i, 0)),
            pl.BlockSpec((d, tf), lambda i, f: (0, f)),
            pl.BlockSpec((tf, d), lambda i, f: (f, 0)),
            pl.BlockSpec((1, d), lambda i, f: (0, 0)),
        ],
        out_specs=pl.BlockSpec((tm, d), lambda i, f: (i, 0)),
        compiler_params=pltpu.CompilerParams(
            dimension_semantics=("parallel", "arbitrary"),
            vmem_limit_bytes=VMEM_LIMIT),
        name="mlp",
    )(h, u, wu_bf, wd_bf, final_w.reshape(1, d))


def kernel(x, w_in, lb_logits, hg_norm_w, rel_bias, w_branch_a, w_branch_b, w_out,
           norm_mix_w, norm_mlp_w, w_up, w_down, norm_final_w):
    batch, seq, d = x.shape
    assert d == D_MODEL and seq % ATT_TQ == 0 and w_in.shape[0] == 1
    x2 = x.reshape(batch * seq, d)
    z, hf, (wa_bf, wb_bf, wo_bf, wu_bf, wd_bf) = _in_proj(
        x2, norm_mix_w[0], w_in[0].astype(BF16),
        (w_branch_a[0], w_branch_b[0], w_out[0], w_up[0], w_down[0]))
    ya = _hgrn(z, hf, lb_logits, hg_norm_w[0], batch, seq)
    yb = _attn(z, rel_bias[0], batch, seq)
    h, u = _merge_out(ya, yb, z, x2, wa_bf, wb_bf, wo_bf, norm_mlp_w[0])
    out = _mlp(h, u, wu_bf, wd_bf, norm_final_w)
    return out.reshape(batch, seq, d)
```

```python
import functools

import jax
import jax.numpy as jnp
import numpy as np
from jax import lax
from jax.experimental import pallas as pl
from jax.experimental.pallas import tpu as pltpu

D_MODEL = 2048
CHUNK = 64
HG_HEADS = 8
HG_DK = 128
HG_DV = 128
HG_WIDTH = HG_HEADS * HG_DV
AT_HEADS = 16
AT_DH = 64
AT_WIDTH = AT_HEADS * AT_DH
LEFT_CHUNKS = 8
REL_CLIP = 256
EPS = 1e-6
D_IN = 4 * HG_WIDTH + 3 * AT_WIDTH + 2 * D_MODEL

LANES = 128
N_LEVELS = 6
ATT_TQ = 2 * CHUNK

BF16 = jnp.bfloat16
F32 = jnp.float32
VMEM_LIMIT = 60000 * 1024

NN = (((1,), (0,)), ((), ()))
NT = (((1,), (1,)), ((), ()))
TN = (((0,), (0,)), ((), ()))


def _dot(a, b, dims=NN):
    return lax.dot_general(a, b, dims, preferred_element_type=F32)


def _rms(xf, w):
    return xf * lax.rsqrt(jnp.mean(xf * xf, axis=-1, keepdims=True) + EPS) * w


CAST_BLOCKS = 64
IN_SLAB = 256


def _in_proj_kernel(x_ref, nw_ref, w_ref, *rest, n_cast):
    cast_in, rest = rest[:n_cast], rest[n_cast:]
    z_ref, zf_ref = rest[:2]
    cast_out, u_ref = rest[2:2 + n_cast], rest[2 + n_cast]

    @pl.when(pl.program_id(1) == 0)
    def _():
        u_ref[...] = _rms(x_ref[...], nw_ref[...]).astype(BF16)

    for c in range(z_ref.shape[1] // IN_SLAB):
        cols = slice(c * IN_SLAB, (c + 1) * IN_SLAB)
        acc = _dot(u_ref[...], w_ref[:, cols])
        z_ref[:, cols] = acc.astype(z_ref.dtype)
        zf_ref[:, cols] = acc

    for src, dst in zip(cast_in, cast_out):
        dst[...] = src[...].astype(BF16)


def _in_proj(x2, norm_w, w_bf, later_weights, *, tm=1024, tn=1024):
    m, k = x2.shape
    n = w_bf.shape[1]
    nj = n // tn
    assert (m // tm) * nj >= CAST_BLOCKS and tn == HG_WIDTH
    hf_tile = 1

    def cast_spec(w):
        rows, cols = w.shape
        return pl.BlockSpec((rows // CAST_BLOCKS, cols),
                            lambda i, j: (jnp.minimum(i * nj + j, CAST_BLOCKS - 1), 0))

    cast_specs = [cast_spec(w) for w in later_weights]
    outs = pl.pallas_call(
        functools.partial(_in_proj_kernel, n_cast=len(later_weights)),
        out_shape=(jax.ShapeDtypeStruct((m, n), BF16),
                   jax.ShapeDtypeStruct((m, 2 * tn), F32),
                   *[jax.ShapeDtypeStruct(w.shape, BF16) for w in later_weights]),
        grid=(m // tm, nj),
        in_specs=[
            pl.BlockSpec((tm, k), lambda i, j: (i, 0)),
            pl.BlockSpec((1, k), lambda i, j: (0, 0)),
            pl.BlockSpec((k, tn), lambda i, j: (0, j)),
            *cast_specs,
        ],
        out_specs=(
            pl.BlockSpec((tm, tn), lambda i, j: (i, j)),
            pl.BlockSpec((tm, tn), lambda i, j: (i, jnp.where(j <= hf_tile, 0, 1))),
            *cast_specs,
        ),
        scratch_shapes=[pltpu.VMEM((tm, k), BF16)],
        compiler_params=pltpu.CompilerParams(
            dimension_semantics=("arbitrary", "arbitrary"),
            vmem_limit_bytes=VMEM_LIMIT),
        name="in_proj",
    )(x2, norm_w.reshape(1, k), w_bf, *later_weights)
    return outs[0], outs[1], outs[2:]


MXU_LEVELS = 3


def _hgrn_tables():
    t = np.arange(CHUNK)[:, None]
    s = np.arange(CHUNK)[None, :]
    mats, masks = [], []
    for l in range(N_LEVELS):
        h = 1 << l
        start = (t // (2 * h)) * (2 * h)
        ref = start + h - 1
        is_q = (t - start) >= h
        if l < MXU_LEVELS:
            mats.append(np.where(is_q, (s > ref) & (s <= t), (s > t) & (s <= ref)))
        s_start = (s // (2 * h)) * (2 * h)
        masks.append((start == s_start) & is_q & ((s - s_start) < h))
    mats.append(s <= t)
    masks.append(t == s)
    w = np.concatenate(mats, axis=0).astype(np.float32)
    w2 = np.concatenate([w, w], axis=1)
    return w2, np.stack(masks).astype(np.float32)


def _level_exponent(b, h):
    parts = []
    for p in range(b.shape[0] // (2 * h)):
        lo = p * 2 * h
        parts.append(-jnp.abs(b[lo:lo + 2 * h, :] - b[lo + h - 1:lo + h, :]))
    return parts[0] if len(parts) == 1 else jnp.concatenate(parts, axis=0)


def _hgrn_kernel(hq_ref, hf_ref, hi_ref, hg_ref, lbl_ref, nw_ref, wexp_ref, mask_ref, cast_ref,
                 y_ref, cast_out_ref, st_ref, *, n_chunks):
    cast_out_ref[...] = cast_ref[...].astype(BF16)

    @pl.when(pl.program_id(1) == 0)
    def _():
        st_ref[...] = jnp.zeros_like(st_ref)

    logits = lbl_ref[...]
    l0, l1 = logits[0:1, :], logits[1:2, :]
    mx = jnp.maximum(l0, l1)
    e0, e1 = jnp.exp(l0 - mx), jnp.exp(l1 - mx)
    lb = e0 / (e0 + e1)
    nw = nw_ref[...]
    wexp = wexp_ref[...]

    def chunk_body(c, carry):
        r = pl.ds(pl.multiple_of(c * CHUNK, CHUNK), CHUNK)
        f = jax.nn.sigmoid(hf_ref[r, :])
        g = lb + (1.0 - lb) * f
        lg = jnp.log2(g)
        kk = 1.0 - g
        q = jax.nn.silu(hq_ref[r, :].astype(F32)) * (HG_DK ** -0.5)
        lg_hi = lg.astype(BF16)
        lg_lo = (lg - lg_hi.astype(F32)).astype(BF16)
        e_mxu = _dot(wexp, jnp.concatenate([lg_hi, lg_lo], axis=0))
        b = e_mxu[MXU_LEVELS * CHUNK:, :]
        scale = [jnp.exp2(e_mxu[l * CHUNK:(l + 1) * CHUNK, :]) for l in range(MXU_LEVELS)]
        scale += [jnp.exp2(_level_exponent(b, 1 << l)) for l in range(MXU_LEVELS, N_LEVELS)]
        eb_all = jnp.exp2(b)
        er_all = jnp.exp2(b[CHUNK - 1:CHUNK, :] - b)
        for h in range(HG_HEADS):
            sl = slice(h * HG_DK, (h + 1) * HG_DK)
            qh, kh = q[:, sl], kk[:, sl]
            vh = hi_ref[r, sl]
            sc = mask_ref[N_LEVELS] * _dot(qh.astype(BF16), kh.astype(BF16), NT)
            for l in range(N_LEVELS):
                xl = scale[l][:, sl]
                sc = sc + mask_ref[l] * _dot((qh * xl).astype(BF16), (kh * xl).astype(BF16), NT)
            eb, er = eb_all[:, sl], er_all[:, sl]
            st = st_ref[h]
            o = _dot(sc.astype(BF16), vh) + _dot((qh * eb).astype(BF16), st.astype(BF16), NT)
            d_last = eb[CHUNK - 1:CHUNK, :]
            st_ref[h] = d_last * st + _dot(vh, (kh * er).astype(BF16), TN)
            y = _rms(o, nw) * jax.nn.silu(hg_ref[r, sl].astype(F32))
            y_ref[r, sl] = y.astype(y_ref.dtype)
        return carry

    lax.fori_loop(0, n_chunks, chunk_body, 0, unroll=True)


def _step_cast_spec(w, steps, step_of):
    rows, cols = w.shape
    return pl.BlockSpec((rows // steps, cols), lambda *ids: (step_of(*ids), 0))


def _hgrn(z, hf, lb_logits, hg_norm_w, later_w, batch, seq, *, tb=256):
    wexp, masks = _hgrn_tables()
    nblk = seq // tb
    wcol = HG_WIDTH
    cast_spec = _step_cast_spec(later_w, batch * nblk, lambda b, i: b * nblk + i)

    def zspec(col):
        return pl.BlockSpec((tb, wcol), lambda b, i, col=col: (b * nblk + i, col))

    return pl.pallas_call(
        functools.partial(_hgrn_kernel, n_chunks=tb // CHUNK),
        out_shape=(jax.ShapeDtypeStruct((batch * seq, HG_WIDTH), BF16),
                   jax.ShapeDtypeStruct(later_w.shape, BF16)),
        grid=(batch, nblk),
        in_specs=[
            zspec(0), zspec(0), zspec(2), zspec(3),
            pl.BlockSpec((2, HG_WIDTH), lambda b, i: (0, 0)),
            pl.BlockSpec((1, HG_DV), lambda b, i: (0, 0)),
            pl.BlockSpec(wexp.shape, lambda b, i: (0, 0)),
            pl.BlockSpec(masks.shape, lambda b, i: (0, 0, 0)),
            cast_spec,
        ],
        out_specs=(pl.BlockSpec((tb, HG_WIDTH), lambda b, i: (b * nblk + i, 0)), cast_spec),
        scratch_shapes=[pltpu.VMEM((HG_HEADS, HG_DV, HG_DK), F32)],
        compiler_params=pltpu.CompilerParams(
            dimension_semantics=("arbitrary", "arbitrary"),
            vmem_limit_bytes=VMEM_LIMIT),
        name="hgrn2",
    )(z, hf, z, z, lb_logits, hg_norm_w.reshape(1, HG_DV),
      jnp.asarray(wexp, BF16), jnp.asarray(masks), later_w)


ATT_TK = ATT_TQ + LEFT_CHUNKS * CHUNK
ATT_ROLL = 1024
assert ATT_TQ - 1 + ATT_TK <= ATT_ROLL and ATT_TK - ATT_TQ == 2 * REL_CLIP


def _rel_table(rel_bias):
    rev = rel_bias[:, ::-1].astype(F32)
    edge = jnp.broadcast_to(rev[:, :1], (rel_bias.shape[0], REL_CLIP))
    g = jnp.concatenate([edge, rev[:, :2 * REL_CLIP], edge], axis=1)
    return g.reshape(AT_HEADS // 2, 2, ATT_ROLL)


ATT_PAD = ATT_TK - ATT_TQ
ATT_EDGE = ATT_PAD // ATT_TQ


ATT_RG = 32


def _attn_kernel(q_ref, k_ref, v_ref, g_ref, cast_ref, o_ref, cast_out_ref,
                 bias_ref, kp_ref, vp_ref, s_ref, p_ref, *, seq):
    cast_out_ref[...] = cast_ref[...].astype(BF16)

    @pl.when(pl.program_id(1) == 0)
    def _():
        qc = lax.broadcasted_iota(jnp.int32, (ATT_TQ, ATT_TK), 0) // CHUNK
        kc = lax.broadcasted_iota(jnp.int32, (ATT_TQ, ATT_TK), 1) // CHUNK
        in_band = (kc >= qc) & (kc <= qc + LEFT_CHUNKS)
        g = g_ref[0]
        for hh in range(2):
            tbl = jnp.broadcast_to(g[hh:hh + 1, :], (ATT_TQ, ATT_ROLL))
            toep = pltpu.roll(tbl, 0, 1, stride=1, stride_axis=0)[:, :ATT_TK]
            rows = slice(hh * ATT_TQ, (hh + 1) * ATT_TQ)
            for n in range(ATT_EDGE + 1):
                first_chunk = (ATT_PAD - n * ATT_TQ) // CHUNK if n < ATT_EDGE else 0
                bias_ref[n, rows, :] = jnp.where(in_band & (kc >= first_chunk), toep, -jnp.inf)

    v_all = v_ref[...]
    head0_all = lax.broadcasted_iota(jnp.int32, v_all.shape, 1) < AT_DH
    ones = jnp.ones_like(v_all)
    pad = jnp.zeros((ATT_PAD, LANES), BF16)
    kp_ref[:ATT_PAD, :] = pad
    kp_ref[ATT_PAD:, :] = k_ref[...]
    vp_ref[0, :ATT_PAD, :] = pad
    vp_ref[1, :ATT_PAD, :] = pad
    vp_ref[0, ATT_PAD:, :] = jnp.where(head0_all, v_all, ones)
    vp_ref[1, ATT_PAD:, :] = jnp.where(head0_all, ones, v_all)

    head0 = lax.broadcasted_iota(jnp.int32, (ATT_TQ, LANES), 1) < AT_DH
    n_blocks = seq // ATT_TQ

    def scores(i, slot):
        r0 = pl.multiple_of(i * ATT_TQ, ATT_TQ)
        q = q_ref[pl.ds(r0, ATT_TQ), :] * (AT_DH ** -0.5)
        zero = jnp.zeros_like(q)
        qq = jnp.concatenate([jnp.where(head0, q, zero), jnp.where(head0, zero, q)], axis=0)
        s = _dot(qq, kp_ref[pl.ds(r0, ATT_TK), :], NT)
        s_ref[slot] = s + bias_ref[jnp.minimum(i, ATT_EDGE)]

    def outputs(i, slot):
        r0 = pl.multiple_of(i * ATT_TQ, ATT_TQ)
        for g in range(2 * ATT_TQ // ATT_RG):
            rows = slice(g * ATT_RG, (g + 1) * ATT_RG)
            s = s_ref[slot, rows, :]
            p_ref[slot, rows, :] = jnp.exp(s - jnp.max(s, axis=-1, keepdims=True)).astype(BF16)
        acc0 = _dot(p_ref[slot, :ATT_TQ, :], vp_ref[0, pl.ds(r0, ATT_TK), :])
        acc1 = _dot(p_ref[slot, ATT_TQ:, :], vp_ref[1, pl.ds(r0, ATT_TK), :])
        num = jnp.where(head0, acc0, acc1)
        den = pltpu.roll(jnp.where(head0, acc1, acc0), AT_DH, 1)
        o_ref[pl.ds(r0, ATT_TQ), :] = (num / den).astype(o_ref.dtype)

    assert n_blocks % 2 == 0
    scores(0, 0)

    def body(t, carry):
        i = 2 * t
        scores(i + 1, 1)
        outputs(i, 0)
        scores(i + 2, 0)
        outputs(i + 1, 1)
        return carry

    lax.fori_loop(0, n_blocks // 2 - 1, body, 0, unroll=True)
    scores(n_blocks - 1, 1)
    outputs(n_blocks - 2, 0)
    outputs(n_blocks - 1, 1)


def _attn(z, rel_bias, later_w, batch, seq):
    g = _rel_table(rel_bias)
    col0 = 4 * HG_WIDTH // LANES
    ncol = AT_WIDTH // LANES
    cast_spec = _step_cast_spec(later_w, AT_HEADS // 2 * batch, lambda hp, b: hp * batch + b)

    def zspec(which):
        return pl.BlockSpec((seq, LANES), lambda hp, b, which=which: (b, col0 + which * ncol + hp))

    return pl.pallas_call(
        functools.partial(_attn_kernel, seq=seq),
        out_shape=(jax.ShapeDtypeStruct((batch * seq, AT_WIDTH), BF16),
                   jax.ShapeDtypeStruct(later_w.shape, BF16)),
        grid=(AT_HEADS // 2, batch),
        in_specs=[
            zspec(0), zspec(1), zspec(2),
            pl.BlockSpec((1, 2, ATT_ROLL), lambda hp, b: (hp, 0, 0)),
            cast_spec,
        ],
        out_specs=(pl.BlockSpec((seq, LANES), lambda hp, b: (b, hp)), cast_spec),
        scratch_shapes=[pltpu.VMEM((ATT_EDGE + 1, 2 * ATT_TQ, ATT_TK), F32),
                        pltpu.VMEM((ATT_PAD + seq, LANES), BF16),
                        pltpu.VMEM((2, ATT_PAD + seq, LANES), BF16),
                        pltpu.VMEM((2, 2 * ATT_TQ, ATT_TK), F32),
                        pltpu.VMEM((2, 2 * ATT_TQ, ATT_TK), BF16)],
        compiler_params=pltpu.CompilerParams(
            dimension_semantics=("arbitrary", "arbitrary"),
            vmem_limit_bytes=VMEM_LIMIT),
        name="band_attn",
    )(z, z, z, g, later_w)


def _merge_out_kernel(ya_ref, yb_ref, ga0_ref, ga1_ref, gb0_ref, gb1_ref, x_ref,
                      wa_ref, wb_ref, wo_ref, nw_ref, h_ref, u_ref, *, rc):
    for c in range(h_ref.shape[0] // rc):
        r = slice(c * rc, (c + 1) * rc)
        pa = _dot(ya_ref[r, :], wa_ref[...])
        pb = _dot(yb_ref[r, :], wb_ref[...])
        ga = jnp.concatenate([ga0_ref[r, :], ga1_ref[r, :]], axis=1).astype(F32)
        gb = jnp.concatenate([gb0_ref[r, :], gb1_ref[r, :]], axis=1).astype(F32)
        merged = (jax.nn.sigmoid(ga) * pa + jax.nn.sigmoid(gb) * pb).astype(BF16)
        h = x_ref[r, :] + _dot(merged, wo_ref[...])
        h_ref[r, :] = h
        u_ref[r, :] = _rms(h, nw_ref[...]).astype(BF16)


def _merge_out(ya, yb, z, x2, wa_bf, wb_bf, wo_bf, norm_w, *, tm=512, rc=256):
    m, d = x2.shape
    gw = d // 2
    gcol = (4 * HG_WIDTH + 3 * AT_WIDTH) // gw
    assert gcol * gw == 4 * HG_WIDTH + 3 * AT_WIDTH

    def resident(shape):
        return pl.BlockSpec(shape, lambda i: (0, 0), pipeline_mode=pl.Buffered(1))

    def gate(blk):
        return pl.BlockSpec((tm, gw), lambda i, blk=blk: (i, gcol + blk))

    return pl.pallas_call(
        functools.partial(_merge_out_kernel, rc=rc),
        out_shape=(jax.ShapeDtypeStruct((m, d), F32), jax.ShapeDtypeStruct((m, d), BF16)),
        grid=(m // tm,),
        in_specs=[
            pl.BlockSpec((tm, HG_WIDTH), lambda i: (i, 0)),
            pl.BlockSpec((tm, AT_WIDTH), lambda i: (i, 0)),
            gate(0), gate(1), gate(2), gate(3),
            pl.BlockSpec((tm, d), lambda i: (i, 0)),
            resident(wa_bf.shape), resident(wb_bf.shape), resident(wo_bf.shape),
            resident((1, d)),
        ],
        out_specs=(pl.BlockSpec((tm, d), lambda i: (i, 0)), pl.BlockSpec((tm, d), lambda i: (i, 0))),
        compiler_params=pltpu.CompilerParams(
            dimension_semantics=("parallel",),
            vmem_limit_bytes=VMEM_LIMIT),
        name="merge_out",
    )(ya, yb, z, z, z, z, x2, wa_bf, wb_bf, wo_bf, norm_w.reshape(1, d))


MLP_SLAB = 512


def _mlp_kernel(h_ref, u_ref, wu_ref, wd_ref, fw_ref, o_ref):
    f = pl.program_id(1)

    @pl.when(f == 0)
    def _():
        o_ref[...] = jnp.zeros_like(o_ref)

    a = jnp.maximum(_dot(u_ref[...], wu_ref[...]), 0.0)
    a = (a * a).astype(BF16)
    for c in range(o_ref.shape[1] // MLP_SLAB):
        cols = slice(c * MLP_SLAB, (c + 1) * MLP_SLAB)
        o_ref[:, cols] += _dot(a, wd_ref[:, cols])

    @pl.when(f == pl.num_programs(1) - 1)
    def _():
        o_ref[...] = _rms(h_ref[...] + o_ref[...], fw_ref[...])


def _mlp(h, u, wu_bf, wd_bf, final_w, *, tm=1024, tf=512):
    m, d = h.shape
    dff = wu_bf.shape[1]
    return pl.pallas_call(
        _mlp_kernel,
        out_shape=jax.ShapeDtypeStruct((m, d), F32),
        grid=(m // tm, dff // tf),
        in_specs=[
            pl.BlockSpec((tm, d), lambda i, f: (i, 0)),
            pl.BlockSpec((tm, d), lambda i, f: (i, 0)),
            pl.BlockSpec((d, tf), lambda i, f: (0, f)),
            pl.BlockSpec((tf, d), lambda i, f: (f, 0)),
            pl.BlockSpec((1, d), lambda i, f: (0, 0)),
        ],
        out_specs=pl.BlockSpec((tm, d), lambda i, f: (i, 0)),
        compiler_params=pltpu.CompilerParams(
            dimension_semantics=("parallel", "arbitrary"),
            vmem_limit_bytes=VMEM_LIMIT),
        name="mlp",
    )(h, u, wu_bf, wd_bf, final_w.reshape(1, d))


def kernel(x, w_in, lb_logits, hg_norm_w, rel_bias, w_branch_a, w_branch_b, w_out,
           norm_mix_w, norm_mlp_w, w_up, w_down, norm_final_w):
    batch, seq, d = x.shape
    assert d == D_MODEL and seq % ATT_TQ == 0 and w_in.shape[0] == 1
    x2 = x.reshape(batch * seq, d)
    z, hf, (wa_bf, wb_bf, wo_bf) = _in_proj(
        x2, norm_mix_w[0], w_in[0].astype(BF16), (w_branch_a[0], w_branch_b[0], w_out[0]))
    ya, wu_bf = _hgrn(z, hf, lb_logits, hg_norm_w[0], w_up[0], batch, seq)
    yb, wd_bf = _attn(z, rel_bias[0], w_down[0], batch, seq)
    h, u = _merge_out(ya, yb, z, x2, wa_bf, wb_bf, wo_bf, norm_mlp_w[0])
    out = _mlp(h, u, wu_bf, wd_bf, norm_final_w)
    return out.reshape(batch, seq, d)
```

```python
import functools

import jax
import jax.numpy as jnp
import numpy as np
from jax import lax
from jax.experimental import pallas as pl
from jax.experimental.pallas import tpu as pltpu

D_MODEL = 2048
CHUNK = 64
HG_HEADS = 8
HG_DK = 128
HG_DV = 128
HG_WIDTH = HG_HEADS * HG_DV
AT_HEADS = 16
AT_DH = 64
AT_WIDTH = AT_HEADS * AT_DH
LEFT_CHUNKS = 8
REL_CLIP = 256
EPS = 1e-6
D_IN = 4 * HG_WIDTH + 3 * AT_WIDTH + 2 * D_MODEL

LANES = 128
N_LEVELS = 6
ATT_TQ = 2 * CHUNK

BF16 = jnp.bfloat16
F32 = jnp.float32
VMEM_LIMIT = 60000 * 1024

NN = (((1,), (0,)), ((), ()))
NT = (((1,), (1,)), ((), ()))
TN = (((0,), (0,)), ((), ()))


def _dot(a, b, dims=NN):
    return lax.dot_general(a, b, dims, preferred_element_type=F32)


def _rms(xf, w):
    return xf * lax.rsqrt(jnp.mean(xf * xf, axis=-1, keepdims=True) + EPS) * w


CAST_BLOCKS = 64
IN_SLAB = 256


LOG2E = 1.4426950408889634
AT_QSCALE = AT_DH ** -0.5 * LOG2E


def _in_proj_kernel(x_ref, nw_ref, w_ref, *rest, n_cast, aq_tile):
    cast_in, rest = rest[:n_cast], rest[n_cast:]
    z_ref, zf_ref = rest[:2]
    cast_out, u_ref = rest[2:2 + n_cast], rest[2 + n_cast]

    @pl.when(pl.program_id(1) == 0)
    def _():
        u_ref[...] = _rms(x_ref[...], nw_ref[...]).astype(BF16)

    zscale = jnp.where(pl.program_id(1) == aq_tile, AT_QSCALE, 1.0).astype(F32)
    for c in range(z_ref.shape[1] // IN_SLAB):
        cols = slice(c * IN_SLAB, (c + 1) * IN_SLAB)
        acc = _dot(u_ref[...], w_ref[:, cols])
        z_ref[:, cols] = (acc * zscale).astype(z_ref.dtype)
        zf_ref[:, cols] = acc

    for src, dst in zip(cast_in, cast_out):
        dst[...] = src[...].astype(BF16)


def _in_proj(x2, norm_w, w_bf, later_weights, *, tm=1024, tn=1024):
    m, k = x2.shape
    n = w_bf.shape[1]
    nj = n // tn
    assert (m // tm) * nj >= CAST_BLOCKS and tn == HG_WIDTH == AT_WIDTH
    hf_tile = 1

    def cast_spec(w):
        rows, cols = w.shape
        return pl.BlockSpec((rows // CAST_BLOCKS, cols),
                            lambda i, j: (jnp.minimum(i * nj + j, CAST_BLOCKS - 1), 0))

    cast_specs = [cast_spec(w) for w in later_weights]
    outs = pl.pallas_call(
        functools.partial(_in_proj_kernel, n_cast=len(later_weights), aq_tile=4 * HG_WIDTH // tn),
        out_shape=(jax.ShapeDtypeStruct((m, n), BF16),
                   jax.ShapeDtypeStruct((m, 2 * tn), F32),
                   *[jax.ShapeDtypeStruct(w.shape, BF16) for w in later_weights]),
        grid=(m // tm, nj),
        in_specs=[
            pl.BlockSpec((tm, k), lambda i, j: (i, 0)),
            pl.BlockSpec((1, k), lambda i, j: (0, 0)),
            pl.BlockSpec((k, tn), lambda i, j: (0, j)),
            *cast_specs,
        ],
        out_specs=(
            pl.BlockSpec((tm, tn), lambda i, j: (i, j)),
            pl.BlockSpec((tm, tn), lambda i, j: (i, jnp.where(j <= hf_tile, 0, 1))),
            *cast_specs,
        ),
        scratch_shapes=[pltpu.VMEM((tm, k), BF16)],
        compiler_params=pltpu.CompilerParams(
            dimension_semantics=("arbitrary", "arbitrary"),
            vmem_limit_bytes=VMEM_LIMIT),
        name="in_proj",
    )(x2, norm_w.reshape(1, k), w_bf, *later_weights)
    return outs[0], outs[1], outs[2:]


MXU_LEVELS = 3


def _hgrn_tables():
    t = np.arange(CHUNK)[:, None]
    s = np.arange(CHUNK)[None, :]
    mats, masks = [], []
    for l in range(N_LEVELS):
        h = 1 << l
        start = (t // (2 * h)) * (2 * h)
        ref = start + h - 1
        is_q = (t - start) >= h
        if l < MXU_LEVELS:
            mats.append(np.where(is_q, (s > ref) & (s <= t), (s > t) & (s <= ref)))
        s_start = (s // (2 * h)) * (2 * h)
        masks.append((start == s_start) & is_q & ((s - s_start) < h))
    mats.append(s <= t)
    masks.append(t == s)
    w = np.concatenate(mats, axis=0).astype(np.float32)
    w2 = np.concatenate([w, w], axis=1)
    return w2, np.stack(masks).astype(np.float32)


def _level_exponent(b, h):
    parts = []
    for p in range(b.shape[0] // (2 * h)):
        lo = p * 2 * h
        parts.append(-jnp.abs(b[lo:lo + 2 * h, :] - b[lo + h - 1:lo + h, :]))
    return parts[0] if len(parts) == 1 else jnp.concatenate(parts, axis=0)


def _hgrn_units(hq_ref, hf_ref, hi_ref, hg_ref, lbl_ref, nw_ref, wexp_ref, mask_ref, y_ref, st_ref,
                *, first_block, n_chunks):
    @pl.when(first_block)
    def _():
        st_ref[...] = jnp.zeros_like(st_ref)

    logits = lbl_ref[...]
    l0, l1 = logits[0:1, :], logits[1:2, :]
    mx = jnp.maximum(l0, l1)
    e0, e1 = jnp.exp(l0 - mx), jnp.exp(l1 - mx)
    lb = e0 / (e0 + e1)
    nw = nw_ref[...]
    wexp = wexp_ref[...]
    yield

    for c in range(n_chunks):
        r = slice(c * CHUNK, (c + 1) * CHUNK)
        f = jax.nn.sigmoid(hf_ref[r, :])
        g = lb + (1.0 - lb) * f
        lg = jnp.log2(g)
        kk = 1.0 - g
        q = jax.nn.silu(hq_ref[r, :].astype(F32)) * (HG_DK ** -0.5)
        lg_hi = lg.astype(BF16)
        lg_lo = (lg - lg_hi.astype(F32)).astype(BF16)
        e_mxu = _dot(wexp, jnp.concatenate([lg_hi, lg_lo], axis=0))
        b = e_mxu[MXU_LEVELS * CHUNK:, :]
        scale = [jnp.exp2(e_mxu[l * CHUNK:(l + 1) * CHUNK, :]).astype(BF16) for l in range(MXU_LEVELS)]
        scale += [jnp.exp2(_level_exponent(b, 1 << l)).astype(BF16) for l in range(MXU_LEVELS, N_LEVELS)]
        eb_all = jnp.exp2(b)
        d_all = eb_all[CHUNK - 1:CHUNK, :]
        eb_all = eb_all.astype(BF16)
        er_all = jnp.exp2(b[CHUNK - 1:CHUNK, :] - b).astype(BF16)
        q, kk = q.astype(BF16), kk.astype(BF16)
        for h in range(HG_HEADS):
            sl = slice(h * HG_DK, (h + 1) * HG_DK)
            qh, kh = q[:, sl], kk[:, sl]
            vh = hi_ref[r, sl]
            sc = mask_ref[N_LEVELS] * _dot(qh, kh, NT)
            for l in range(N_LEVELS):
                xl = scale[l][:, sl]
                sc = sc + mask_ref[l] * _dot(qh * xl, kh * xl, NT)
            st = st_ref[h]
            o = _dot(sc.astype(BF16), vh) + _dot(qh * eb_all[:, sl], st.astype(BF16), NT)
            st_ref[h] = d_all[:, sl] * st + _dot(vh, kh * er_all[:, sl], TN)
            y = _rms(o, nw) * jax.nn.silu(hg_ref[r, sl].astype(F32))
            y_ref[r, sl] = y.astype(y_ref.dtype)
            yield


ATT_TK = ATT_TQ + LEFT_CHUNKS * CHUNK
ATT_ROLL = 1024
assert ATT_TQ - 1 + ATT_TK <= ATT_ROLL and ATT_TK - ATT_TQ == 2 * REL_CLIP


def _rel_table(rel_bias):
    rev = rel_bias[:, ::-1].astype(F32) * LOG2E
    edge = jnp.broadcast_to(rev[:, :1], (rel_bias.shape[0], REL_CLIP))
    g = jnp.concatenate([edge, rev[:, :2 * REL_CLIP], edge], axis=1)
    return g.reshape(AT_HEADS // 2, 2, ATT_ROLL)


ATT_PAD = ATT_TK - ATT_TQ
ATT_EDGE = ATT_PAD // ATT_TQ


ATT_RG = 32


def _attn_units(q_ref, k_ref, v_ref, g_ref, o_ref, bias_ref, kp_ref, vp_ref, s_ref, p_ref,
                *, new_head_pair, seq):
    @pl.when(new_head_pair)
    def _():
        qc = lax.broadcasted_iota(jnp.int32, (ATT_TQ, ATT_TK), 0) // CHUNK
        kc = lax.broadcasted_iota(jnp.int32, (ATT_TQ, ATT_TK), 1) // CHUNK
        in_band = (kc >= qc) & (kc <= qc + LEFT_CHUNKS)
        g = g_ref[0]
        for hh in range(2):
            tbl = jnp.broadcast_to(g[hh:hh + 1, :], (ATT_TQ, ATT_ROLL))
            toep = pltpu.roll(tbl, 0, 1, stride=1, stride_axis=0)[:, :ATT_TK]
            rows = slice(hh * ATT_TQ, (hh + 1) * ATT_TQ)
            for n in range(ATT_EDGE + 1):
                first_chunk = (ATT_PAD - n * ATT_TQ) // CHUNK if n < ATT_EDGE else 0
                bias_ref[n, rows, :] = jnp.where(in_band & (kc >= first_chunk), toep, -jnp.inf)

    v_all = v_ref[...]
    head0_all = lax.broadcasted_iota(jnp.int32, v_all.shape, 1) < AT_DH
    ones = jnp.ones_like(v_all)
    pad = jnp.zeros((ATT_PAD, LANES), BF16)
    kp_ref[:ATT_PAD, :] = pad
    kp_ref[ATT_PAD:, :] = k_ref[...]
    vp_ref[0, :ATT_PAD, :] = pad
    vp_ref[1, :ATT_PAD, :] = pad
    vp_ref[0, ATT_PAD:, :] = jnp.where(head0_all, v_all, ones)
    vp_ref[1, ATT_PAD:, :] = jnp.where(head0_all, ones, v_all)

    head0 = lax.broadcasted_iota(jnp.int32, (ATT_TQ, LANES), 1) < AT_DH
    n_blocks = seq // ATT_TQ
    yield

    def scores(i):
        slot = i % 2
        q = q_ref[i * ATT_TQ:(i + 1) * ATT_TQ, :]
        zero = jnp.zeros_like(q)
        qq = jnp.concatenate([jnp.where(head0, q, zero), jnp.where(head0, zero, q)], axis=0)
        s = _dot(qq, kp_ref[i * ATT_TQ:i * ATT_TQ + ATT_TK, :], NT)
        s_ref[slot] = s + bias_ref[min(i, ATT_EDGE)]

    def outputs(i):
        slot = i % 2
        for g in range(2 * ATT_TQ // ATT_RG):
            rows = slice(g * ATT_RG, (g + 1) * ATT_RG)
            s = s_ref[slot, rows, :]
            p_ref[slot, rows, :] = jnp.exp2(s - jnp.max(s, axis=-1, keepdims=True)).astype(BF16)
        keys = slice(i * ATT_TQ, i * ATT_TQ + ATT_TK)
        acc0 = _dot(p_ref[slot, :ATT_TQ, :], vp_ref[0, keys, :])
        acc1 = _dot(p_ref[slot, ATT_TQ:, :], vp_ref[1, keys, :])
        num = jnp.where(head0, acc0, acc1)
        den = pltpu.roll(jnp.where(head0, acc1, acc0), AT_DH, 1)
        o_ref[i * ATT_TQ:(i + 1) * ATT_TQ, :] = (num / den).astype(o_ref.dtype)

    scores(0)
    for i in range(n_blocks):
        if i + 1 < n_blocks:
            scores(i + 1)
        outputs(i)
        yield


def _mixers_kernel(hq_ref, hf_ref, hi_ref, hg_ref, lbl_ref, nw_ref, wexp_ref, mask_ref,
                   q_ref, k_ref, v_ref, g_ref, cast_a_ref, cast_b_ref,
                   y_ref, o_ref, cast_a_out, cast_b_out,
                   st_ref, bias_ref, kp_ref, vp_ref, s_ref, p_ref, *, n_chunks, seq, steps_per_pair):
    cast_a_out[...] = cast_a_ref[...].astype(BF16)
    cast_b_out[...] = cast_b_ref[...].astype(BF16)
    step = pl.program_id(0) * pl.num_programs(1) + pl.program_id(1)
    hgrn = _hgrn_units(hq_ref, hf_ref, hi_ref, hg_ref, lbl_ref, nw_ref, wexp_ref, mask_ref,
                       y_ref, st_ref, first_block=pl.program_id(1) == 0, n_chunks=n_chunks)
    attn = _attn_units(q_ref, k_ref, v_ref, g_ref, o_ref, bias_ref, kp_ref, vp_ref, s_ref, p_ref,
                       new_head_pair=step % steps_per_pair == 0, seq=seq)
    n_hgrn, n_attn = n_chunks * HG_HEADS, seq // ATT_TQ
    assert n_hgrn % n_attn == 0
    next(hgrn)
    next(attn)
    for _ in range(n_attn):
        next(attn)
        for _ in range(n_hgrn // n_attn):
            next(hgrn)
    assert next(attn, None) is None and next(hgrn, None) is None


def _mixers(z, hf, lb_logits, hg_norm_w, rel_bias, cast_a, cast_b, batch, seq, *, tb=256):
    wexp, masks = _hgrn_tables()
    g = _rel_table(rel_bias)
    nblk = seq // tb
    steps = batch * nblk
    assert steps == AT_HEADS // 2 * batch
    col0 = 4 * HG_WIDTH // LANES
    ncol = AT_WIDTH // LANES

    def step_of(b, i):
        return b * nblk + i

    def hspec(col):
        return pl.BlockSpec((tb, HG_WIDTH), lambda b, i, col=col: (b * nblk + i, col))

    def aspec(which):
        return pl.BlockSpec((seq, LANES), lambda b, i, which=which: (
            step_of(b, i) % batch, col0 + which * ncol + step_of(b, i) // batch))

    def const(shape):
        return pl.BlockSpec(shape, lambda b, i: (0,) * len(shape))

    def cast_spec(w):
        return pl.BlockSpec((w.shape[0] // steps, w.shape[1]), lambda b, i: (step_of(b, i), 0))

    return pl.pallas_call(
        functools.partial(_mixers_kernel, n_chunks=tb // CHUNK, seq=seq, steps_per_pair=batch),
        out_shape=(jax.ShapeDtypeStruct((batch * seq, HG_WIDTH), BF16),
                   jax.ShapeDtypeStruct((batch * seq, AT_WIDTH), BF16),
                   jax.ShapeDtypeStruct(cast_a.shape, BF16),
                   jax.ShapeDtypeStruct(cast_b.shape, BF16)),
        grid=(batch, nblk),
        in_specs=[
            hspec(0), hspec(0), hspec(2), hspec(3),
            const((2, HG_WIDTH)), const((1, HG_DV)), const(wexp.shape), const(masks.shape),
            aspec(0), aspec(1), aspec(2),
            pl.BlockSpec((1, 2, ATT_ROLL), lambda b, i: (step_of(b, i) // batch, 0, 0)),
            cast_spec(cast_a), cast_spec(cast_b),
        ],
        out_specs=(
            pl.BlockSpec((tb, HG_WIDTH), lambda b, i: (b * nblk + i, 0)),
            pl.BlockSpec((seq, LANES), lambda b, i: (step_of(b, i) % batch, step_of(b, i) // batch)),
            cast_spec(cast_a), cast_spec(cast_b),
        ),
        scratch_shapes=[pltpu.VMEM((HG_HEADS, HG_DV, HG_DK), F32),
                        pltpu.VMEM((ATT_EDGE + 1, 2 * ATT_TQ, ATT_TK), F32),
                        pltpu.VMEM((ATT_PAD + seq, LANES), BF16),
                        pltpu.VMEM((2, ATT_PAD + seq, LANES), BF16),
                        pltpu.VMEM((2, 2 * ATT_TQ, ATT_TK), F32),
                        pltpu.VMEM((2, 2 * ATT_TQ, ATT_TK), BF16)],
        compiler_params=pltpu.CompilerParams(
            dimension_semantics=("arbitrary", "arbitrary"),
            vmem_limit_bytes=VMEM_LIMIT),
        name="mixers",
    )(z, hf, z, z, lb_logits, hg_norm_w.reshape(1, HG_DV), jnp.asarray(wexp, BF16), jnp.asarray(masks),
      z, z, z, g, cast_a, cast_b)


def _merge_out_kernel(ya_ref, yb_ref, ga0_ref, ga1_ref, gb0_ref, gb1_ref, x_ref,
                      wa_ref, wb_ref, wo_ref, nw_ref, h_ref, u_ref, *, rc):
    for c in range(h_ref.shape[0] // rc):
        r = slice(c * rc, (c + 1) * rc)
        pa = _dot(ya_ref[r, :], wa_ref[...])
        pb = _dot(yb_ref[r, :], wb_ref[...])
        ga = jnp.concatenate([ga0_ref[r, :], ga1_ref[r, :]], axis=1).astype(F32)
        gb = jnp.concatenate([gb0_ref[r, :], gb1_ref[r, :]], axis=1).astype(F32)
        merged = (jax.nn.sigmoid(ga) * pa + jax.nn.sigmoid(gb) * pb).astype(BF16)
        h = x_ref[r, :] + _dot(merged, wo_ref[...])
        h_ref[r, :] = h
        u_ref[r, :] = _rms(h, nw_ref[...]).astype(BF16)


def _merge_out(ya, yb, z, x2, wa_bf, wb_bf, wo_bf, norm_w, *, tm=512, rc=256):
    m, d = x2.shape
    gw = d // 2
    gcol = (4 * HG_WIDTH + 3 * AT_WIDTH) // gw
    assert gcol * gw == 4 * HG_WIDTH + 3 * AT_WIDTH

    def resident(shape):
        return pl.BlockSpec(shape, lambda i: (0, 0), pipeline_mode=pl.Buffered(1))

    def gate(blk):
        return pl.BlockSpec((tm, gw), lambda i, blk=blk: (i, gcol + blk))

    return pl.pallas_call(
        functools.partial(_merge_out_kernel, rc=rc),
        out_shape=(jax.ShapeDtypeStruct((m, d), F32), jax.ShapeDtypeStruct((m, d), BF16)),
        grid=(m // tm,),
        in_specs=[
            pl.BlockSpec((tm, HG_WIDTH), lambda i: (i, 0)),
            pl.BlockSpec((tm, AT_WIDTH), lambda i: (i, 0)),
            gate(0), gate(1), gate(2), gate(3),
            pl.BlockSpec((tm, d), lambda i: (i, 0)),
            resident(wa_bf.shape), resident(wb_bf.shape), resident(wo_bf.shape),
            resident((1, d)),
        ],
        out_specs=(pl.BlockSpec((tm, d), lambda i: (i, 0)), pl.BlockSpec((tm, d), lambda i: (i, 0))),
        compiler_params=pltpu.CompilerParams(
            dimension_semantics=("parallel",),
            vmem_limit_bytes=VMEM_LIMIT),
        name="merge_out",
    )(ya, yb, z, z, z, z, x2, wa_bf, wb_bf, wo_bf, norm_w.reshape(1, d))


MLP_SLAB = 512


def _mlp_kernel(h_ref, u_ref, wu_ref, wd_ref, fw_ref, o_ref):
    f = pl.program_id(1)

    @pl.when(f == 0)
    def _():
        o_ref[...] = jnp.zeros_like(o_ref)

    a = jnp.maximum(_dot(u_ref[...], wu_ref[...]), 0.0)
    a = (a * a).astype(BF16)
    for c in range(o_ref.shape[1] // MLP_SLAB):
        cols = slice(c * MLP_SLAB, (c + 1) * MLP_SLAB)
        o_ref[:, cols] += _dot(a, wd_ref[:, cols])

    @pl.when(f == pl.num_programs(1) - 1)
    def _():
        o_ref[...] = _rms(h_ref[...] + o_ref[...], fw_ref[...])


def _mlp(h, u, wu_bf, wd_bf, final_w, *, tm=1024, tf=512):
    m, d = h.shape
    dff = wu_bf.shape[1]
    return pl.pallas_call(
        _mlp_kernel,
        out_shape=jax.ShapeDtypeStruct((m, d), F32),
        grid=(m // tm, dff // tf),
        in_specs=[
            pl.BlockSpec((tm, d), lambda i, f: (i, 0)),
            pl.BlockSpec((tm, d), lambda i, f: (i, 0)),
            pl.BlockSpec((d, tf), lambda i, f: (0, f)),
            pl.BlockSpec((tf, d), lambda i, f: (f, 0)),
            pl.BlockSpec((1, d), lambda i, f: (0, 0)),
        ],
        out_specs=pl.BlockSpec((tm, d), lambda i, f: (i, 0)),
        compiler_params=pltpu.CompilerParams(
            dimension_semantics=("parallel", "arbitrary"),
            vmem_limit_bytes=VMEM_LIMIT),
        name="mlp",
    )(h, u, wu_bf, wd_bf, final_w.reshape(1, d))


def kernel(x, w_in, lb_logits, hg_norm_w, rel_bias, w_branch_a, w_branch_b, w_out,
           norm_mix_w, norm_mlp_w, w_up, w_down, norm_final_w):
    batch, seq, d = x.shape
    assert d == D_MODEL and seq % ATT_TQ == 0 and w_in.shape[0] == 1
    x2 = x.reshape(batch * seq, d)
    z, hf, (wa_bf, wb_bf, wo_bf) = _in_proj(
        x2, norm_mix_w[0], w_in[0].astype(BF16), (w_branch_a[0], w_branch_b[0], w_out[0]))
    ya, yb, wu_bf, wd_bf = _mixers(z, hf, lb_logits, hg_norm_w[0], rel_bias[0], w_up[0], w_down[0],
                                   batch, seq)
    h, u = _merge_out(ya, yb, z, x2, wa_bf, wb_bf, wo_bf, norm_mlp_w[0])
    out = _mlp(h, u, wu_bf, wd_bf, norm_final_w)
    return out.reshape(batch, seq, d)
```

```python
import functools

import jax
import jax.numpy as jnp
import numpy as np
from jax import lax
from jax.experimental import pallas as pl
from jax.experimental.pallas import tpu as pltpu

D_MODEL = 2048
CHUNK = 64
HG_HEADS = 8
HG_DK = 128
HG_DV = 128
HG_WIDTH = HG_HEADS * HG_DV
AT_HEADS = 16
AT_DH = 64
AT_WIDTH = AT_HEADS * AT_DH
LEFT_CHUNKS = 8
REL_CLIP = 256
EPS = 1e-6
D_IN = 4 * HG_WIDTH + 3 * AT_WIDTH + 2 * D_MODEL

LANES = 128
N_LEVELS = 6
ATT_TQ = 2 * CHUNK

BF16 = jnp.bfloat16
F32 = jnp.float32
VMEM_LIMIT = 60000 * 1024

NN = (((1,), (0,)), ((), ()))
NT = (((1,), (1,)), ((), ()))
TN = (((0,), (0,)), ((), ()))


def _dot(a, b, dims=NN):
    return lax.dot_general(a, b, dims, preferred_element_type=F32)


def _rms(xf, w):
    return xf * lax.rsqrt(jnp.mean(xf * xf, axis=-1, keepdims=True) + EPS) * w


CAST_BLOCKS = 64
IN_SLAB = 256


LOG2E = 1.4426950408889634
AT_QSCALE = AT_DH ** -0.5 * LOG2E


AQ_COLS = (4 * HG_WIDTH, 4 * HG_WIDTH + AT_WIDTH)
ZF_WIDTH = 2 * HG_WIDTH


def _in_proj_kernel(*refs, n_cast, n_alias, emit_w):
    x_ref, nw_ref, w_ref = refs[:3]
    cast_in = refs[3:3 + n_cast]
    outs = refs[3 + n_cast + n_alias:]
    z_ref, zf_ref = outs[:2]
    outs = outs[2:]
    if emit_w:
        wbf_ref, outs = outs[0], outs[1:]
    cast_out, u_ref = outs[:n_cast], outs[n_cast]
    tn = z_ref.shape[1]

    @pl.when(pl.program_id(1) == 0)
    def _():
        u_ref[...] = _rms(x_ref[...], nw_ref[...]).astype(BF16)

    if emit_w:
        wbf_ref[...] = w_ref[...].astype(BF16)
        w_ref = wbf_ref

    col0 = pl.program_id(1) * tn
    zscale = jnp.where((col0 >= AQ_COLS[0]) & (col0 < AQ_COLS[1]), AT_QSCALE, 1.0).astype(F32)
    for c in range(tn // IN_SLAB):
        cols = slice(c * IN_SLAB, (c + 1) * IN_SLAB)
        acc = _dot(u_ref[...], w_ref[:, cols])
        z_ref[:, cols] = (acc * zscale).astype(z_ref.dtype)
        zf_ref[:, cols] = acc

    for src, dst in zip(cast_in, cast_out):
        dst[...] = src[...].astype(BF16)


def _in_proj_call(x2, norm_w, w, later_weights, aliased, *, row_tile0, row_tiles, tm, tn, name):
    m, k = x2.shape
    n = w.shape[1]
    nj = n // tn
    emit_w = w.dtype == F32
    assert HG_WIDTH % tn == 0 and AQ_COLS[0] % tn == 0 and AQ_COLS[1] % tn == 0
    hf_first, n_hf = HG_WIDTH // tn, HG_WIDTH // tn
    assert not later_weights or row_tiles * nj >= CAST_BLOCKS

    def cast_spec(wl):
        rows, cols = wl.shape
        return pl.BlockSpec((rows // CAST_BLOCKS, cols),
                            lambda i, j: (jnp.minimum(i * nj + j, CAST_BLOCKS - 1), 0))

    cast_specs = [cast_spec(wl) for wl in later_weights]
    z_spec = pl.BlockSpec((tm, tn), lambda i, j: (row_tile0 + i, j))
    zf_spec = pl.BlockSpec((tm, tn), lambda i, j: (row_tile0 + i, jnp.clip(j - hf_first, 0, n_hf)))
    w_spec = pl.BlockSpec((k, tn), lambda i, j: (0, j))
    n_in = 3 + len(later_weights)
    outs = pl.pallas_call(
        functools.partial(_in_proj_kernel, n_cast=len(later_weights), n_alias=len(aliased),
                          emit_w=emit_w),
        out_shape=(jax.ShapeDtypeStruct((m, n), BF16),
                   jax.ShapeDtypeStruct((m, ZF_WIDTH), F32),
                   *([jax.ShapeDtypeStruct(w.shape, BF16)] if emit_w else []),
                   *[jax.ShapeDtypeStruct(wl.shape, BF16) for wl in later_weights]),
        grid=(row_tiles, nj),
        in_specs=[
            pl.BlockSpec((tm, k), lambda i, j: (row_tile0 + i, 0)),
            pl.BlockSpec((1, k), lambda i, j: (0, 0)),
            w_spec,
            *cast_specs,
            *[pl.BlockSpec(memory_space=pl.ANY) for _ in aliased],
        ],
        out_specs=(z_spec, zf_spec, *([w_spec] if emit_w else []), *cast_specs),
        input_output_aliases={n_in + a: a for a in range(len(aliased))},
        scratch_shapes=[pltpu.VMEM((tm, k), BF16)],
        compiler_params=pltpu.CompilerParams(
            dimension_semantics=("arbitrary", "arbitrary"),
            vmem_limit_bytes=VMEM_LIMIT),
        name=name,
    )(x2, norm_w.reshape(1, k), w, *later_weights, *aliased)
    return outs


def _in_proj(x2, norm_w, w_f32, later_weights, *, tm=1024):
    row_tiles = x2.shape[0] // tm
    z, zf, w_bf = _in_proj_call(x2, norm_w, w_f32, (), (), row_tile0=0, row_tiles=1,
                                tm=tm, tn=512, name="in_proj_first")
    outs = _in_proj_call(x2, norm_w, w_bf, later_weights, (z, zf), row_tile0=1,
                         row_tiles=row_tiles - 1, tm=tm, tn=1024, name="in_proj")
    return outs[0], outs[1], outs[2:]


MXU_LEVELS = 3


def _hgrn_tables():
    t = np.arange(CHUNK)[:, None]
    s = np.arange(CHUNK)[None, :]
    mats, masks = [], []
    for l in range(N_LEVELS):
        h = 1 << l
        start = (t // (2 * h)) * (2 * h)
        ref = start + h - 1
        is_q = (t - start) >= h
        if l < MXU_LEVELS:
            mats.append(np.where(is_q, (s > ref) & (s <= t), (s > t) & (s <= ref)))
        s_start = (s // (2 * h)) * (2 * h)
        masks.append((start == s_start) & is_q & ((s - s_start) < h))
    mats.append(s <= t)
    masks.append(t == s)
    w = np.concatenate(mats, axis=0).astype(np.float32)
    w2 = np.concatenate([w, w], axis=1)
    return w2, np.stack(masks).astype(np.float32)


def _level_exponent(b, h):
    parts = []
    for p in range(b.shape[0] // (2 * h)):
        lo = p * 2 * h
        parts.append(-jnp.abs(b[lo:lo + 2 * h, :] - b[lo + h - 1:lo + h, :]))
    return parts[0] if len(parts) == 1 else jnp.concatenate(parts, axis=0)


def _hgrn_units(hq_ref, hf_ref, hi_ref, hg_ref, lbl_ref, nw_ref, wexp_ref, mask_ref, y_ref, st_ref,
                *, first_block, n_chunks):
    @pl.when(first_block)
    def _():
        st_ref[...] = jnp.zeros_like(st_ref)

    logits = lbl_ref[...]
    l0, l1 = logits[0:1, :], logits[1:2, :]
    mx = jnp.maximum(l0, l1)
    e0, e1 = jnp.exp(l0 - mx), jnp.exp(l1 - mx)
    lb = e0 / (e0 + e1)
    nw = nw_ref[...]
    wexp = wexp_ref[...]
    yield

    for c in range(n_chunks):
        r = slice(c * CHUNK, (c + 1) * CHUNK)
        f = jax.nn.sigmoid(hf_ref[r, :])
        g = lb + (1.0 - lb) * f
        lg = jnp.log2(g)
        kk = 1.0 - g
        q = jax.nn.silu(hq_ref[r, :].astype(F32)) * (HG_DK ** -0.5)
        lg_hi = lg.astype(BF16)
        lg_lo = (lg - lg_hi.astype(F32)).astype(BF16)
        e_mxu = _dot(wexp, jnp.concatenate([lg_hi, lg_lo], axis=0))
        b = e_mxu[MXU_LEVELS * CHUNK:, :]
        scale = [jnp.exp2(e_mxu[l * CHUNK:(l + 1) * CHUNK, :]).astype(BF16) for l in range(MXU_LEVELS)]
        scale += [jnp.exp2(_level_exponent(b, 1 << l)).astype(BF16) for l in range(MXU_LEVELS, N_LEVELS)]
        eb_all = jnp.exp2(b)
        d_all = eb_all[CHUNK - 1:CHUNK, :]
        eb_all = eb_all.astype(BF16)
        er_all = jnp.exp2(b[CHUNK - 1:CHUNK, :] - b).astype(BF16)
        q, kk = q.astype(BF16), kk.astype(BF16)
        for h in range(HG_HEADS):
            sl = slice(h * HG_DK, (h + 1) * HG_DK)
            qh, kh = q[:, sl], kk[:, sl]
            vh = hi_ref[r, sl]
            sc = mask_ref[N_LEVELS] * _dot(qh, kh, NT)
            for l in range(N_LEVELS):
                xl = scale[l][:, sl]
                sc = sc + mask_ref[l] * _dot(qh * xl, kh * xl, NT)
            st = st_ref[h]
            o = _dot(sc.astype(BF16), vh) + _dot(qh * eb_all[:, sl], st.astype(BF16), NT)
            st_ref[h] = d_all[:, sl] * st + _dot(vh, kh * er_all[:, sl], TN)
            y = _rms(o, nw) * jax.nn.silu(hg_ref[r, sl].astype(F32))
            y_ref[r, sl] = y.astype(y_ref.dtype)
            yield


ATT_TK = ATT_TQ + LEFT_CHUNKS * CHUNK
ATT_ROLL = 1024
assert ATT_TQ - 1 + ATT_TK <= ATT_ROLL and ATT_TK - ATT_TQ == 2 * REL_CLIP


def _rel_table(rel_bias):
    rev = rel_bias[:, ::-1].astype(F32) * LOG2E
    edge = jnp.broadcast_to(rev[:, :1], (rel_bias.shape[0], REL_CLIP))
    g = jnp.concatenate([edge, rev[:, :2 * REL_CLIP], edge], axis=1)
    return g.reshape(AT_HEADS // 2, 2, ATT_ROLL)


ATT_PAD = ATT_TK - ATT_TQ
ATT_EDGE = ATT_PAD // ATT_TQ


ATT_RG = 32


def _attn_units(q_ref, k_ref, v_ref, g_ref, o_ref, bias_ref, kp_ref, vp_ref, s_ref, p_ref,
                *, new_head_pair, seq):
    @pl.when(new_head_pair)
    def _():
        qc = lax.broadcasted_iota(jnp.int32, (ATT_TQ, ATT_TK), 0) // CHUNK
        kc = lax.broadcasted_iota(jnp.int32, (ATT_TQ, ATT_TK), 1) // CHUNK
        in_band = (kc >= qc) & (kc <= qc + LEFT_CHUNKS)
        g = g_ref[0]
        for hh in range(2):
            tbl = jnp.broadcast_to(g[hh:hh + 1, :], (ATT_TQ, ATT_ROLL))
            toep = pltpu.roll(tbl, 0, 1, stride=1, stride_axis=0)[:, :ATT_TK]
            rows = slice(hh * ATT_TQ, (hh + 1) * ATT_TQ)
            for n in range(ATT_EDGE + 1):
                first_chunk = (ATT_PAD - n * ATT_TQ) // CHUNK if n < ATT_EDGE else 0
                bias_ref[n, rows, :] = jnp.where(in_band & (kc >= first_chunk), toep, -jnp.inf)

    v_all = v_ref[...]
    head0_all = lax.broadcasted_iota(jnp.int32, v_all.shape, 1) < AT_DH
    ones = jnp.ones_like(v_all)
    pad = jnp.zeros((ATT_PAD, LANES), BF16)
    kp_ref[:ATT_PAD, :] = pad
    kp_ref[ATT_PAD:, :] = k_ref[...]
    vp_ref[0, :ATT_PAD, :] = pad
    vp_ref[1, :ATT_PAD, :] = pad
    vp_ref[0, ATT_PAD:, :] = jnp.where(head0_all, v_all, ones)
    vp_ref[1, ATT_PAD:, :] = jnp.where(head0_all, ones, v_all)

    head0 = lax.broadcasted_iota(jnp.int32, (ATT_TQ, LANES), 1) < AT_DH
    n_blocks = seq // ATT_TQ
    yield

    def scores(i):
        slot = i % 2
        q = q_ref[i * ATT_TQ:(i + 1) * ATT_TQ, :]
        zero = jnp.zeros_like(q)
        qq = jnp.concatenate([jnp.where(head0, q, zero), jnp.where(head0, zero, q)], axis=0)
        s = _dot(qq, kp_ref[i * ATT_TQ:i * ATT_TQ + ATT_TK, :], NT)
        s_ref[slot] = s + bias_ref[min(i, ATT_EDGE)]

    def outputs(i):
        slot = i % 2
        for g in range(2 * ATT_TQ // ATT_RG):
            rows = slice(g * ATT_RG, (g + 1) * ATT_RG)
            s = s_ref[slot, rows, :]
            p_ref[slot, rows, :] = jnp.exp2(s - jnp.max(s, axis=-1, keepdims=True)).astype(BF16)
        keys = slice(i * ATT_TQ, i * ATT_TQ + ATT_TK)
        acc0 = _dot(p_ref[slot, :ATT_TQ, :], vp_ref[0, keys, :])
        acc1 = _dot(p_ref[slot, ATT_TQ:, :], vp_ref[1, keys, :])
        num = jnp.where(head0, acc0, acc1)
        den = pltpu.roll(jnp.where(head0, acc1, acc0), AT_DH, 1)
        o_ref[i * ATT_TQ:(i + 1) * ATT_TQ, :] = (num / den).astype(o_ref.dtype)

    scores(0)
    for i in range(n_blocks):
        if i + 1 < n_blocks:
            scores(i + 1)
        outputs(i)
        yield


def _mixers_kernel(hq_ref, hf_ref, hi_ref, hg_ref, lbl_ref, nw_ref, wexp_ref, mask_ref,
                   q_ref, k_ref, v_ref, g_ref, cast_a_ref, cast_b_ref,
                   y_ref, o_ref, cast_a_out, cast_b_out,
                   st_ref, bias_ref, kp_ref, vp_ref, s_ref, p_ref, *, n_chunks, seq, steps_per_pair):
    cast_a_out[...] = cast_a_ref[...].astype(BF16)
    cast_b_out[...] = cast_b_ref[...].astype(BF16)
    step = pl.program_id(0) * pl.num_programs(1) + pl.program_id(1)
    hgrn = _hgrn_units(hq_ref, hf_ref, hi_ref, hg_ref, lbl_ref, nw_ref, wexp_ref, mask_ref,
                       y_ref, st_ref, first_block=pl.program_id(1) == 0, n_chunks=n_chunks)
    attn = _attn_units(q_ref, k_ref, v_ref, g_ref, o_ref, bias_ref, kp_ref, vp_ref, s_ref, p_ref,
                       new_head_pair=step % steps_per_pair == 0, seq=seq)
    n_hgrn, n_attn = n_chunks * HG_HEADS, seq // ATT_TQ
    assert n_hgrn % n_attn == 0
    next(hgrn)
    next(attn)
    for _ in range(n_attn):
        next(attn)
        for _ in range(n_hgrn // n_attn):
            next(hgrn)
    assert next(attn, None) is None and next(hgrn, None) is None


def _mixers(z, hf, lb_logits, hg_norm_w, rel_bias, cast_a, cast_b, batch, seq, *, tb=256):
    wexp, masks = _hgrn_tables()
    g = _rel_table(rel_bias)
    nblk = seq // tb
    steps = batch * nblk
    assert steps == AT_HEADS // 2 * batch
    col0 = 4 * HG_WIDTH // LANES
    ncol = AT_WIDTH // LANES

    def step_of(b, i):
        return b * nblk + i

    def hspec(col):
        return pl.BlockSpec((tb, HG_WIDTH), lambda b, i, col=col: (b * nblk + i, col))

    def aspec(which):
        return pl.BlockSpec((seq, LANES), lambda b, i, which=which: (
            step_of(b, i) % batch, col0 + which * ncol + step_of(b, i) // batch))

    def const(shape):
        return pl.BlockSpec(shape, lambda b, i: (0,) * len(shape))

    def cast_spec(w):
        return pl.BlockSpec((w.shape[0] // steps, w.shape[1]), lambda b, i: (step_of(b, i), 0))

    return pl.pallas_call(
        functools.partial(_mixers_kernel, n_chunks=tb // CHUNK, seq=seq, steps_per_pair=batch),
        out_shape=(jax.ShapeDtypeStruct((batch * seq, HG_WIDTH), BF16),
                   jax.ShapeDtypeStruct((batch * seq, AT_WIDTH), BF16),
                   jax.ShapeDtypeStruct(cast_a.shape, BF16),
                   jax.ShapeDtypeStruct(cast_b.shape, BF16)),
        grid=(batch, nblk),
        in_specs=[
            hspec(0), hspec(0), hspec(2), hspec(3),
            const((2, HG_WIDTH)), const((1, HG_DV)), const(wexp.shape), const(masks.shape),
            aspec(0), aspec(1), aspec(2),
            pl.BlockSpec((1, 2, ATT_ROLL), lambda b, i: (step_of(b, i) // batch, 0, 0)),
            cast_spec(cast_a), cast_spec(cast_b),
        ],
        out_specs=(
            pl.BlockSpec((tb, HG_WIDTH), lambda b, i: (b * nblk + i, 0)),
            pl.BlockSpec((seq, LANES), lambda b, i: (step_of(b, i) % batch, step_of(b, i) // batch)),
            cast_spec(cast_a), cast_spec(cast_b),
        ),
        scratch_shapes=[pltpu.VMEM((HG_HEADS, HG_DV, HG_DK), F32),
                        pltpu.VMEM((ATT_EDGE + 1, 2 * ATT_TQ, ATT_TK), F32),
                        pltpu.VMEM((ATT_PAD + seq, LANES), BF16),
                        pltpu.VMEM((2, ATT_PAD + seq, LANES), BF16),
                        pltpu.VMEM((2, 2 * ATT_TQ, ATT_TK), F32),
                        pltpu.VMEM((2, 2 * ATT_TQ, ATT_TK), BF16)],
        compiler_params=pltpu.CompilerParams(
            dimension_semantics=("arbitrary", "arbitrary"),
            vmem_limit_bytes=VMEM_LIMIT),
        name="mixers",
    )(z, hf, z, z, lb_logits, hg_norm_w.reshape(1, HG_DV), jnp.asarray(wexp, BF16), jnp.asarray(masks),
      z, z, z, g, cast_a, cast_b)


def _merge_out_kernel(ya_ref, yb_ref, ga0_ref, ga1_ref, gb0_ref, gb1_ref, x_ref,
                      wa_ref, wb_ref, wo_ref, nw_ref, h_ref, u_ref, *, rc):
    for c in range(h_ref.shape[0] // rc):
        r = slice(c * rc, (c + 1) * rc)
        pa = _dot(ya_ref[r, :], wa_ref[...])
        pb = _dot(yb_ref[r, :], wb_ref[...])
        ga = jnp.concatenate([ga0_ref[r, :], ga1_ref[r, :]], axis=1).astype(F32)
        gb = jnp.concatenate([gb0_ref[r, :], gb1_ref[r, :]], axis=1).astype(F32)
        merged = (jax.nn.sigmoid(ga) * pa + jax.nn.sigmoid(gb) * pb).astype(BF16)
        h = x_ref[r, :] + _dot(merged, wo_ref[...])
        h_ref[r, :] = h
        u_ref[r, :] = _rms(h, nw_ref[...]).astype(BF16)


def _merge_out(ya, yb, z, x2, wa_bf, wb_bf, wo_bf, norm_w, *, tm=512, rc=256):
    m, d = x2.shape
    gw = d // 2
    gcol = (4 * HG_WIDTH + 3 * AT_WIDTH) // gw
    assert gcol * gw == 4 * HG_WIDTH + 3 * AT_WIDTH

    def resident(shape):
        return pl.BlockSpec(shape, lambda i: (0, 0), pipeline_mode=pl.Buffered(1))

    def gate(blk):
        return pl.BlockSpec((tm, gw), lambda i, blk=blk: (i, gcol + blk))

    return pl.pallas_call(
        functools.partial(_merge_out_kernel, rc=rc),
        out_shape=(jax.ShapeDtypeStruct((m, d), F32), jax.ShapeDtypeStruct((m, d), BF16)),
        grid=(m // tm,),
        in_specs=[
            pl.BlockSpec((tm, HG_WIDTH), lambda i: (i, 0)),
            pl.BlockSpec((tm, AT_WIDTH), lambda i: (i, 0)),
            gate(0), gate(1), gate(2), gate(3),
            pl.BlockSpec((tm, d), lambda i: (i, 0)),
            resident(wa_bf.shape), resident(wb_bf.shape), resident(wo_bf.shape),
            resident((1, d)),
        ],
        out_specs=(pl.BlockSpec((tm, d), lambda i: (i, 0)), pl.BlockSpec((tm, d), lambda i: (i, 0))),
        compiler_params=pltpu.CompilerParams(
            dimension_semantics=("parallel",),
            vmem_limit_bytes=VMEM_LIMIT),
        name="merge_out",
    )(ya, yb, z, z, z, z, x2, wa_bf, wb_bf, wo_bf, norm_w.reshape(1, d))


MLP_SLAB = 512


def _mlp_kernel(h_ref, u_ref, wu_ref, wd_ref, fw_ref, o_ref):
    f = pl.program_id(1)

    @pl.when(f == 0)
    def _():
        o_ref[...] = jnp.zeros_like(o_ref)

    a = jnp.maximum(_dot(u_ref[...], wu_ref[...]), 0.0)
    a = (a * a).astype(BF16)
    for c in range(o_ref.shape[1] // MLP_SLAB):
        cols = slice(c * MLP_SLAB, (c + 1) * MLP_SLAB)
        o_ref[:, cols] += _dot(a, wd_ref[:, cols])

    @pl.when(f == pl.num_programs(1) - 1)
    def _():
        o_ref[...] = _rms(h_ref[...] + o_ref[...], fw_ref[...])


def _mlp(h, u, wu_bf, wd_bf, final_w, *, tm=1024, tf=512):
    m, d = h.shape
    dff = wu_bf.shape[1]
    return pl.pallas_call(
        _mlp_kernel,
        out_shape=jax.ShapeDtypeStruct((m, d), F32),
        grid=(m // tm, dff // tf),
        in_specs=[
            pl.BlockSpec((tm, d), lambda i, f: (i, 0)),
            pl.BlockSpec((tm, d), lambda i, f: (i, 0)),
            pl.BlockSpec((d, tf), lambda i, f: (0, f)),
            pl.BlockSpec((tf, d), lambda i, f: (f, 0)),
            pl.BlockSpec((1, d), lambda i, f: (0, 0)),
        ],
        out_specs=pl.BlockSpec((tm, d), lambda i, f: (i, 0)),
        compiler_params=pltpu.CompilerParams(
            dimension_semantics=("parallel", "arbitrary"),
            vmem_limit_bytes=VMEM_LIMIT),
        name="mlp",
    )(h, u, wu_bf, wd_bf, final_w.reshape(1, d))


def kernel(x, w_in, lb_logits, hg_norm_w, rel_bias, w_branch_a, w_branch_b, w_out,
           norm_mix_w, norm_mlp_w, w_up, w_down, norm_final_w):
    batch, seq, d = x.shape
    assert d == D_MODEL and seq % ATT_TQ == 0 and w_in.shape[0] == 1
    x2 = x.reshape(batch * seq, d)
    z, hf, (wa_bf, wb_bf, wo_bf) = _in_proj(
        x2, norm_mix_w[0], w_in[0], (w_branch_a[0], w_branch_b[0], w_out[0]))
    ya, yb, wu_bf, wd_bf = _mixers(z, hf, lb_logits, hg_norm_w[0], rel_bias[0], w_up[0], w_down[0],
                                   batch, seq)
    h, u = _merge_out(ya, yb, z, x2, wa_bf, wb_bf, wo_bf, norm_mlp_w[0])
    out = _mlp(h, u, wu_bf, wd_bf, norm_final_w)
    return out.reshape(batch, seq, d)
```

```python
import functools

import jax
import jax.numpy as jnp
import numpy as np
from jax import lax
from jax.experimental import pallas as pl
from jax.experimental.pallas import tpu as pltpu

D_MODEL = 2048
CHUNK = 64
HG_HEADS = 8
HG_DK = 128
HG_DV = 128
HG_WIDTH = HG_HEADS * HG_DV
AT_HEADS = 16
AT_DH = 64
AT_WIDTH = AT_HEADS * AT_DH
LEFT_CHUNKS = 8
REL_CLIP = 256
EPS = 1e-6
D_IN = 4 * HG_WIDTH + 3 * AT_WIDTH + 2 * D_MODEL

LANES = 128
N_LEVELS = 6
ATT_TQ = 2 * CHUNK

BF16 = jnp.bfloat16
F32 = jnp.float32
VMEM_LIMIT = 60000 * 1024

NN = (((1,), (0,)), ((), ()))
NT = (((1,), (1,)), ((), ()))
TN = (((0,), (0,)), ((), ()))


def _dot(a, b, dims=NN):
    return lax.dot_general(a, b, dims, preferred_element_type=F32)


def _rms(xf, w):
    return xf * lax.rsqrt(jnp.mean(xf * xf, axis=-1, keepdims=True) + EPS) * w


CAST_BLOCKS = 64
IN_SLAB = 256


LOG2E = 1.4426950408889634
AT_QSCALE = AT_DH ** -0.5 * LOG2E


AQ_COLS = (4 * HG_WIDTH, 4 * HG_WIDTH + AT_WIDTH)
ZF_WIDTH = 2 * HG_WIDTH
Z_OFF = HG_WIDTH


def _in_proj_kernel(*refs, n_cast, col_tile0, with_zf, f32_w):
    x_ref, nw_ref, w_ref = refs[:3]
    cast_in = refs[3:3 + n_cast]
    outs = refs[3 + n_cast:]
    z_ref, outs = outs[0], outs[1:]
    if with_zf:
        zf_ref, outs = outs[0], outs[1:]
    cast_out, u_ref = outs[:n_cast], outs[n_cast]
    tn = z_ref.shape[1]

    @pl.when(pl.program_id(1) == 0)
    def _():
        u_ref[...] = _rms(x_ref[...], nw_ref[...]).astype(BF16)

    if f32_w:
        wb_ref = outs[n_cast + 1]

        @pl.when(pl.program_id(0) == 0)
        def _():
            wb_ref[...] = w_ref[...].astype(BF16)
        w_ref = wb_ref

    col0 = (pl.program_id(1) + col_tile0) * tn
    zscale = jnp.where((col0 >= AQ_COLS[0]) & (col0 < AQ_COLS[1]), AT_QSCALE, 1.0).astype(F32)
    for c in range(tn // IN_SLAB):
        cols = slice(c * IN_SLAB, (c + 1) * IN_SLAB)
        acc = _dot(u_ref[...], w_ref[:, cols])
        z_ref[:, cols] = (acc * zscale).astype(z_ref.dtype)
        if with_zf:
            zf_ref[:, cols] = acc

    for src, dst in zip(cast_in, cast_out):
        dst[...] = src[...].astype(BF16)


def _in_proj_call(x2, norm_w, w, casts, *, col_tile0, col_tiles, with_zf, tm, tn, name):
    m, k = x2.shape
    f32_w = w.dtype == F32
    assert not f32_w or col_tiles == 1
    steps = (m // tm) * col_tiles
    hf_local = HG_WIDTH // tn - col_tile0
    assert tn == HG_WIDTH and AQ_COLS[0] % tn == 0 and AQ_COLS[1] % tn == 0

    def cast_spec(wl, blocks):
        assert blocks <= steps
        rows, cols = wl.shape
        return pl.BlockSpec((rows // blocks, cols),
                            lambda i, j: (jnp.minimum(i * col_tiles + j, blocks - 1), 0))

    cast_specs = [cast_spec(wl, blocks) for wl, blocks in casts]
    z_spec = pl.BlockSpec((tm, tn), lambda i, j: (i, j))
    zf_spec = pl.BlockSpec((tm, tn), lambda i, j: (i, jnp.where(j <= hf_local, 0, 1)))
    w_mode = {"pipeline_mode": pl.Buffered(1)} if f32_w else {}
    return pl.pallas_call(
        functools.partial(_in_proj_kernel, n_cast=len(casts), col_tile0=col_tile0,
                          with_zf=with_zf, f32_w=f32_w),
        out_shape=(jax.ShapeDtypeStruct((m, col_tiles * tn), BF16),
                   *([jax.ShapeDtypeStruct((m, ZF_WIDTH), F32)] if with_zf else []),
                   *[jax.ShapeDtypeStruct(wl.shape, BF16) for wl, _ in casts]),
        grid=(m // tm, col_tiles),
        in_specs=[
            pl.BlockSpec((tm, k), lambda i, j: (i, 0)),
            pl.BlockSpec((1, k), lambda i, j: (0, 0)),
            pl.BlockSpec((k, tn), lambda i, j: (0, col_tile0 + j), **w_mode),
            *cast_specs,
        ],
        out_specs=(z_spec, *([zf_spec] if with_zf else []), *cast_specs),
        scratch_shapes=[pltpu.VMEM((tm, k), BF16), *([pltpu.VMEM((k, tn), BF16)] if f32_w else [])],
        compiler_params=pltpu.CompilerParams(
            dimension_semantics=("arbitrary", "arbitrary"),
            vmem_limit_bytes=VMEM_LIMIT),
        name=name,
    )(x2, norm_w.reshape(1, k), w, *[wl for wl, _ in casts])


def _in_proj(x2, norm_w, w_f32, later_weights, *, tm=1024, tn=1024):
    tm_q = tm // 2
    zq, w_bf = _in_proj_call(x2, norm_w, w_f32, [(w_f32, x2.shape[0] // tm_q)], col_tile0=0,
                             col_tiles=1, with_zf=False, tm=tm_q, tn=tn, name="in_proj_q")
    outs = _in_proj_call(x2, norm_w, w_bf, [(wl, CAST_BLOCKS) for wl in later_weights],
                         col_tile0=1, col_tiles=w_f32.shape[1] // tn - 1, with_zf=True,
                         tm=tm, tn=tn, name="in_proj")
    return zq, outs[0], outs[1], outs[2:]


MXU_LEVELS = 3


def _hgrn_tables():
    t = np.arange(CHUNK)[:, None]
    s = np.arange(CHUNK)[None, :]
    mats, masks = [], []
    for l in range(N_LEVELS):
        h = 1 << l
        start = (t // (2 * h)) * (2 * h)
        ref = start + h - 1
        is_q = (t - start) >= h
        if l < MXU_LEVELS:
            mats.append(np.where(is_q, (s > ref) & (s <= t), (s > t) & (s <= ref)))
        s_start = (s // (2 * h)) * (2 * h)
        masks.append((start == s_start) & is_q & ((s - s_start) < h))
    mats.append(s <= t)
    masks.append(t == s)
    w = np.concatenate(mats, axis=0).astype(np.float32)
    w2 = np.concatenate([w, w], axis=1)
    return w2, np.stack(masks).astype(np.float32)


def _level_exponent(b, h):
    parts = []
    for p in range(b.shape[0] // (2 * h)):
        lo = p * 2 * h
        parts.append(-jnp.abs(b[lo:lo + 2 * h, :] - b[lo + h - 1:lo + h, :]))
    return parts[0] if len(parts) == 1 else jnp.concatenate(parts, axis=0)


def _hgrn_units(hq_ref, hf_ref, hi_ref, hg_ref, lbl_ref, nw_ref, wexp_ref, mask_ref, y_ref, st_ref,
                *, first_block, n_chunks):
    @pl.when(first_block)
    def _():
        st_ref[...] = jnp.zeros_like(st_ref)

    logits = lbl_ref[...]
    l0, l1 = logits[0:1, :], logits[1:2, :]
    mx = jnp.maximum(l0, l1)
    e0, e1 = jnp.exp(l0 - mx), jnp.exp(l1 - mx)
    lb = e0 / (e0 + e1)
    nw = nw_ref[...]
    wexp = wexp_ref[...]
    yield

    for c in range(n_chunks):
        r = slice(c * CHUNK, (c + 1) * CHUNK)
        f = jax.nn.sigmoid(hf_ref[r, :])
        g = lb + (1.0 - lb) * f
        lg = jnp.log2(g)
        kk = 1.0 - g
        q = jax.nn.silu(hq_ref[r, :].astype(F32)) * (HG_DK ** -0.5)
        lg_hi = lg.astype(BF16)
        lg_lo = (lg - lg_hi.astype(F32)).astype(BF16)
        e_mxu = _dot(wexp, jnp.concatenate([lg_hi, lg_lo], axis=0))
        b = e_mxu[MXU_LEVELS * CHUNK:, :]
        scale = [jnp.exp2(e_mxu[l * CHUNK:(l + 1) * CHUNK, :]).astype(BF16) for l in range(MXU_LEVELS)]
        scale += [jnp.exp2(_level_exponent(b, 1 << l)).astype(BF16) for l in range(MXU_LEVELS, N_LEVELS)]
        eb_all = jnp.exp2(b)
        d_all = eb_all[CHUNK - 1:CHUNK, :]
        eb_all = eb_all.astype(BF16)
        er_all = jnp.exp2(b[CHUNK - 1:CHUNK, :] - b).astype(BF16)
        q, kk = q.astype(BF16), kk.astype(BF16)
        for h in range(HG_HEADS):
            sl = slice(h * HG_DK, (h + 1) * HG_DK)
            qh, kh = q[:, sl], kk[:, sl]
            vh = hi_ref[r, sl]
            sc = mask_ref[N_LEVELS] * _dot(qh, kh, NT)
            for l in range(N_LEVELS):
                xl = scale[l][:, sl]
                sc = sc + mask_ref[l] * _dot(qh * xl, kh * xl, NT)
            st = st_ref[h]
            o = _dot(sc.astype(BF16), vh) + _dot(qh * eb_all[:, sl], st.astype(BF16), NT)
            st_ref[h] = d_all[:, sl] * st + _dot(vh, kh * er_all[:, sl], TN)
            y = _rms(o, nw) * jax.nn.silu(hg_ref[r, sl].astype(F32))
            y_ref[r, sl] = y.astype(y_ref.dtype)
            yield


ATT_TK = ATT_TQ + LEFT_CHUNKS * CHUNK
ATT_ROLL = 1024
assert ATT_TQ - 1 + ATT_TK <= ATT_ROLL and ATT_TK - ATT_TQ == 2 * REL_CLIP


def _rel_table(rel_bias):
    rev = rel_bias[:, ::-1].astype(F32) * LOG2E
    edge = jnp.broadcast_to(rev[:, :1], (rel_bias.shape[0], REL_CLIP))
    g = jnp.concatenate([edge, rev[:, :2 * REL_CLIP], edge], axis=1)
    return g.reshape(AT_HEADS // 2, 2, ATT_ROLL)


ATT_PAD = ATT_TK - ATT_TQ
ATT_EDGE = ATT_PAD // ATT_TQ


ATT_RG = 32


def _attn_units(q_ref, k_ref, v_ref, g_ref, o_ref, bias_ref, kp_ref, vp_ref, s_ref, p_ref,
                *, new_head_pair, seq):
    @pl.when(new_head_pair)
    def _():
        qc = lax.broadcasted_iota(jnp.int32, (ATT_TQ, ATT_TK), 0) // CHUNK
        kc = lax.broadcasted_iota(jnp.int32, (ATT_TQ, ATT_TK), 1) // CHUNK
        in_band = (kc >= qc) & (kc <= qc + LEFT_CHUNKS)
        g = g_ref[0]
        for hh in range(2):
            tbl = jnp.broadcast_to(g[hh:hh + 1, :], (ATT_TQ, ATT_ROLL))
            toep = pltpu.roll(tbl, 0, 1, stride=1, stride_axis=0)[:, :ATT_TK]
            rows = slice(hh * ATT_TQ, (hh + 1) * ATT_TQ)
            for n in range(ATT_EDGE + 1):
                first_chunk = (ATT_PAD - n * ATT_TQ) // CHUNK if n < ATT_EDGE else 0
                bias_ref[n, rows, :] = jnp.where(in_band & (kc >= first_chunk), toep, -jnp.inf)

    v_all = v_ref[...]
    head0_all = lax.broadcasted_iota(jnp.int32, v_all.shape, 1) < AT_DH
    ones = jnp.ones_like(v_all)
    pad = jnp.zeros((ATT_PAD, LANES), BF16)
    kp_ref[:ATT_PAD, :] = pad
    kp_ref[ATT_PAD:, :] = k_ref[...]
    vp_ref[0, :ATT_PAD, :] = pad
    vp_ref[1, :ATT_PAD, :] = pad
    vp_ref[0, ATT_PAD:, :] = jnp.where(head0_all, v_all, ones)
    vp_ref[1, ATT_PAD:, :] = jnp.where(head0_all, ones, v_all)

    head0 = lax.broadcasted_iota(jnp.int32, (ATT_TQ, LANES), 1) < AT_DH
    n_blocks = seq // ATT_TQ
    yield

    def scores(i):
        slot = i % 2
        q = q_ref[i * ATT_TQ:(i + 1) * ATT_TQ, :]
        zero = jnp.zeros_like(q)
        qq = jnp.concatenate([jnp.where(head0, q, zero), jnp.where(head0, zero, q)], axis=0)
        s = _dot(qq, kp_ref[i * ATT_TQ:i * ATT_TQ + ATT_TK, :], NT)
        s_ref[slot] = s + bias_ref[min(i, ATT_EDGE)]

    def outputs(i):
        slot = i % 2
        for g in range(2 * ATT_TQ // ATT_RG):
            rows = slice(g * ATT_RG, (g + 1) * ATT_RG)
            s = s_ref[slot, rows, :]
            p_ref[slot, rows, :] = jnp.exp2(s - jnp.max(s, axis=-1, keepdims=True)).astype(BF16)
        keys = slice(i * ATT_TQ, i * ATT_TQ + ATT_TK)
        acc0 = _dot(p_ref[slot, :ATT_TQ, :], vp_ref[0, keys, :])
        acc1 = _dot(p_ref[slot, ATT_TQ:, :], vp_ref[1, keys, :])
        num = jnp.where(head0, acc0, acc1)
        den = pltpu.roll(jnp.where(head0, acc1, acc0), AT_DH, 1)
        o_ref[i * ATT_TQ:(i + 1) * ATT_TQ, :] = (num / den).astype(o_ref.dtype)

    scores(0)
    for i in range(n_blocks):
        if i + 1 < n_blocks:
            scores(i + 1)
        outputs(i)
        yield


def _mixers_kernel(hq_ref, hf_ref, hi_ref, hg_ref, lbl_ref, nw_ref, wexp_ref, mask_ref,
                   q_ref, k_ref, v_ref, g_ref, cast_a_ref, cast_b_ref,
                   y_ref, o_ref, cast_a_out, cast_b_out,
                   st_ref, bias_ref, kp_ref, vp_ref, s_ref, p_ref, *, n_chunks, seq, steps_per_pair):
    cast_a_out[...] = cast_a_ref[...].astype(BF16)
    cast_b_out[...] = cast_b_ref[...].astype(BF16)
    step = pl.program_id(0) * pl.num_programs(1) + pl.program_id(1)
    hgrn = _hgrn_units(hq_ref, hf_ref, hi_ref, hg_ref, lbl_ref, nw_ref, wexp_ref, mask_ref,
                       y_ref, st_ref, first_block=pl.program_id(1) == 0, n_chunks=n_chunks)
    attn = _attn_units(q_ref, k_ref, v_ref, g_ref, o_ref, bias_ref, kp_ref, vp_ref, s_ref, p_ref,
                       new_head_pair=step % steps_per_pair == 0, seq=seq)
    n_hgrn, n_attn = n_chunks * HG_HEADS, seq // ATT_TQ
    assert n_hgrn % n_attn == 0
    next(hgrn)
    next(attn)
    for _ in range(n_attn):
        next(attn)
        for _ in range(n_hgrn // n_attn):
            next(hgrn)
    assert next(attn, None) is None and next(hgrn, None) is None


def _mixers(zq, z, hf, lb_logits, hg_norm_w, rel_bias, cast_a, cast_b, batch, seq, *, tb=256):
    wexp, masks = _hgrn_tables()
    g = _rel_table(rel_bias)
    nblk = seq // tb
    steps = batch * nblk
    assert steps == AT_HEADS // 2 * batch
    col0 = (4 * HG_WIDTH - Z_OFF) // LANES
    ncol = AT_WIDTH // LANES

    def step_of(b, i):
        return b * nblk + i

    def hspec(col):
        return pl.BlockSpec((tb, HG_WIDTH), lambda b, i, col=col: (b * nblk + i, col))

    def aspec(which):
        return pl.BlockSpec((seq, LANES), lambda b, i, which=which: (
            step_of(b, i) % batch, col0 + which * ncol + step_of(b, i) // batch))

    def const(shape):
        return pl.BlockSpec(shape, lambda b, i: (0,) * len(shape))

    def cast_spec(w):
        return pl.BlockSpec((w.shape[0] // steps, w.shape[1]), lambda b, i: (step_of(b, i), 0))

    return pl.pallas_call(
        functools.partial(_mixers_kernel, n_chunks=tb // CHUNK, seq=seq, steps_per_pair=batch),
        out_shape=(jax.ShapeDtypeStruct((batch * seq, HG_WIDTH), BF16),
                   jax.ShapeDtypeStruct((batch * seq, AT_WIDTH), BF16),
                   jax.ShapeDtypeStruct(cast_a.shape, BF16),
                   jax.ShapeDtypeStruct(cast_b.shape, BF16)),
        grid=(batch, nblk),
        in_specs=[
            hspec(0), hspec(0), hspec(2 - Z_OFF // HG_WIDTH), hspec(3 - Z_OFF // HG_WIDTH),
            const((2, HG_WIDTH)), const((1, HG_DV)), const(wexp.shape), const(masks.shape),
            aspec(0), aspec(1), aspec(2),
            pl.BlockSpec((1, 2, ATT_ROLL), lambda b, i: (step_of(b, i) // batch, 0, 0)),
            cast_spec(cast_a), cast_spec(cast_b),
        ],
        out_specs=(
            pl.BlockSpec((tb, HG_WIDTH), lambda b, i: (b * nblk + i, 0)),
            pl.BlockSpec((seq, LANES), lambda b, i: (step_of(b, i) % batch, step_of(b, i) // batch)),
            cast_spec(cast_a), cast_spec(cast_b),
        ),
        scratch_shapes=[pltpu.VMEM((HG_HEADS, HG_DV, HG_DK), F32),
                        pltpu.VMEM((ATT_EDGE + 1, 2 * ATT_TQ, ATT_TK), F32),
                        pltpu.VMEM((ATT_PAD + seq, LANES), BF16),
                        pltpu.VMEM((2, ATT_PAD + seq, LANES), BF16),
                        pltpu.VMEM((2, 2 * ATT_TQ, ATT_TK), F32),
                        pltpu.VMEM((2, 2 * ATT_TQ, ATT_TK), BF16)],
        compiler_params=pltpu.CompilerParams(
            dimension_semantics=("arbitrary", "arbitrary"),
            vmem_limit_bytes=VMEM_LIMIT),
        name="mixers",
    )(zq, hf, z, z, lb_logits, hg_norm_w.reshape(1, HG_DV), jnp.asarray(wexp, BF16), jnp.asarray(masks),
      z, z, z, g, cast_a, cast_b)


def _merge_out_kernel(ya_ref, yb_ref, ga0_ref, ga1_ref, gb0_ref, gb1_ref, x_ref,
                      wa_ref, wb_ref, wo_ref, nw_ref, h_ref, u_ref, *, rc):
    for c in range(h_ref.shape[0] // rc):
        r = slice(c * rc, (c + 1) * rc)
        pa = _dot(ya_ref[r, :], wa_ref[...])
        pb = _dot(yb_ref[r, :], wb_ref[...])
        ga = jnp.concatenate([ga0_ref[r, :], ga1_ref[r, :]], axis=1).astype(F32)
        gb = jnp.concatenate([gb0_ref[r, :], gb1_ref[r, :]], axis=1).astype(F32)
        merged = (jax.nn.sigmoid(ga) * pa + jax.nn.sigmoid(gb) * pb).astype(BF16)
        h = x_ref[r, :] + _dot(merged, wo_ref[...])
        h_ref[r, :] = h
        u_ref[r, :] = _rms(h, nw_ref[...]).astype(BF16)


def _merge_out(ya, yb, z, x2, wa_bf, wb_bf, wo_bf, norm_w, *, tm=512, rc=256):
    m, d = x2.shape
    gw = d // 2
    gcol = (4 * HG_WIDTH + 3 * AT_WIDTH - Z_OFF) // gw
    assert gcol * gw == 4 * HG_WIDTH + 3 * AT_WIDTH - Z_OFF

    def resident(shape):
        return pl.BlockSpec(shape, lambda i: (0, 0), pipeline_mode=pl.Buffered(1))

    def gate(blk):
        return pl.BlockSpec((tm, gw), lambda i, blk=blk: (i, gcol + blk))

    return pl.pallas_call(
        functools.partial(_merge_out_kernel, rc=rc),
        out_shape=(jax.ShapeDtypeStruct((m, d), F32), jax.ShapeDtypeStruct((m, d), BF16)),
        grid=(m // tm,),
        in_specs=[
            pl.BlockSpec((tm, HG_WIDTH), lambda i: (i, 0)),
            pl.BlockSpec((tm, AT_WIDTH), lambda i: (i, 0)),
            gate(0), gate(1), gate(2), gate(3),
            pl.BlockSpec((tm, d), lambda i: (i, 0)),
            resident(wa_bf.shape), resident(wb_bf.shape), resident(wo_bf.shape),
            resident((1, d)),
        ],
        out_specs=(pl.BlockSpec((tm, d), lambda i: (i, 0)), pl.BlockSpec((tm, d), lambda i: (i, 0))),
        compiler_params=pltpu.CompilerParams(
            dimension_semantics=("parallel",),
            vmem_limit_bytes=VMEM_LIMIT),
        name="merge_out",
    )(ya, yb, z, z, z, z, x2, wa_bf, wb_bf, wo_bf, norm_w.reshape(1, d))


MLP_SLAB = 512


def _mlp_kernel(h_ref, u_ref, wu_ref, wd_ref, fw_ref, o_ref):
    f = pl.program_id(1)

    @pl.when(f == 0)
    def _():
        o_ref[...] = jnp.zeros_like(o_ref)

    a = jnp.maximum(_dot(u_ref[...], wu_ref[...]), 0.0)
    a = (a * a).astype(BF16)
    for c in range(o_ref.shape[1] // MLP_SLAB):
        cols = slice(c * MLP_SLAB, (c + 1) * MLP_SLAB)
        o_ref[:, cols] += _dot(a, wd_ref[:, cols])

    @pl.when(f == pl.num_programs(1) - 1)
    def _():
        o_ref[...] = _rms(h_ref[...] + o_ref[...], fw_ref[...])


def _mlp(h, u, wu_bf, wd_bf, final_w, *, tm=1024, tf=512):
    m, d = h.shape
    dff = wu_bf.shape[1]
    return pl.pallas_call(
        _mlp_kernel,
        out_shape=jax.ShapeDtypeStruct((m, d), F32),
        grid=(m // tm, dff // tf),
        in_specs=[
            pl.BlockSpec((tm, d), lambda i, f: (i, 0)),
            pl.BlockSpec((tm, d), lambda i, f: (i, 0)),
            pl.BlockSpec((d, tf), lambda i, f: (0, f)),
            pl.BlockSpec((tf, d), lambda i, f: (f, 0)),
            pl.BlockSpec((1, d), lambda i, f: (0, 0)),
        ],
        out_specs=pl.BlockSpec((tm, d), lambda i, f: (i, 0)),
        compiler_params=pltpu.CompilerParams(
            dimension_semantics=("parallel", "arbitrary"),
            vmem_limit_bytes=VMEM_LIMIT),
        name="mlp",
    )(h, u, wu_bf, wd_bf, final_w.reshape(1, d))


def kernel(x, w_in, lb_logits, hg_norm_w, rel_bias, w_branch_a, w_branch_b, w_out,
           norm_mix_w, norm_mlp_w, w_up, w_down, norm_final_w):
    batch, seq, d = x.shape
    assert d == D_MODEL and seq % ATT_TQ == 0 and w_in.shape[0] == 1
    x2 = x.reshape(batch * seq, d)
    zq, z, hf, (wa_bf, wb_bf, wo_bf) = _in_proj(
        x2, norm_mix_w[0], w_in[0], (w_branch_a[0], w_branch_b[0], w_out[0]))
    ya, yb, wu_bf, wd_bf = _mixers(zq, z, hf, lb_logits, hg_norm_w[0], rel_bias[0], w_up[0],
                                   w_down[0], batch, seq)
    h, u = _merge_out(ya, yb, z, x2, wa_bf, wb_bf, wo_bf, norm_mlp_w[0])
    out = _mlp(h, u, wu_bf, wd_bf, norm_final_w)
    return out.reshape(batch, seq, d)
```

```python
import functools

import jax
import jax.numpy as jnp
import numpy as np
from jax import lax
from jax.experimental import pallas as pl
from jax.experimental.pallas import tpu as pltpu

D_MODEL = 2048
CHUNK = 64
HG_HEADS = 8
HG_DK = 128
HG_DV = 128
HG_WIDTH = HG_HEADS * HG_DV
AT_HEADS = 16
AT_DH = 64
AT_WIDTH = AT_HEADS * AT_DH
LEFT_CHUNKS = 8
REL_CLIP = 256
EPS = 1e-6
D_IN = 4 * HG_WIDTH + 3 * AT_WIDTH + 2 * D_MODEL

LANES = 128
N_LEVELS = 6
ATT_TQ = 2 * CHUNK

BF16 = jnp.bfloat16
F32 = jnp.float32
VMEM_LIMIT = 60000 * 1024

NN = (((1,), (0,)), ((), ()))
NT = (((1,), (1,)), ((), ()))
TN = (((0,), (0,)), ((), ()))


def _dot(a, b, dims=NN):
    return lax.dot_general(a, b, dims, preferred_element_type=F32)


def _rms(xf, w):
    return xf * lax.rsqrt(jnp.mean(xf * xf, axis=-1, keepdims=True) + EPS) * w


CAST_BLOCKS = 64
IN_SLAB = 256


LOG2E = 1.4426950408889634
AT_QSCALE = AT_DH ** -0.5 * LOG2E


AQ_COLS = (4 * HG_WIDTH, 4 * HG_WIDTH + AT_WIDTH)
ZF_WIDTH = 2 * HG_WIDTH


def _in_proj_kernel(x_ref, nw_ref, w_ref, *rest, n_cast):
    cast_in, (z_ref, zf_ref) = rest[:n_cast], rest[n_cast:n_cast + 2]
    cast_out, u_ref = rest[n_cast + 2:2 * n_cast + 2], rest[2 * n_cast + 2]
    tn = z_ref.shape[1]

    @pl.when(pl.program_id(1) == 0)
    def _():
        u_ref[...] = _rms(x_ref[...], nw_ref[...]).astype(BF16)

    col0 = pl.program_id(1) * tn
    zscale = jnp.where((col0 >= AQ_COLS[0]) & (col0 < AQ_COLS[1]), AT_QSCALE, 1.0).astype(F32)
    for c in range(tn // IN_SLAB):
        cols = slice(c * IN_SLAB, (c + 1) * IN_SLAB)
        acc = _dot(u_ref[...], w_ref[:, cols])
        z_ref[:, cols] = (acc * zscale).astype(z_ref.dtype)
        zf_ref[:, cols] = acc

    for src, dst in zip(cast_in, cast_out):
        dst[...] = src[...].astype(BF16)


def _in_proj(x2, norm_w, w_bf, later_weights, *, tm=1024, tn=1024):
    m, k = x2.shape
    n = w_bf.shape[1]
    nj = n // tn
    assert (m // tm) * nj >= CAST_BLOCKS and tn == HG_WIDTH
    assert AQ_COLS[0] % tn == 0 and AQ_COLS[1] % tn == 0
    hf_tile = HG_WIDTH // tn

    def cast_spec(wl):
        rows, cols = wl.shape
        return pl.BlockSpec((rows // CAST_BLOCKS, cols),
                            lambda i, j: (jnp.minimum(i * nj + j, CAST_BLOCKS - 1), 0))

    cast_specs = [cast_spec(wl) for wl in later_weights]
    outs = pl.pallas_call(
        functools.partial(_in_proj_kernel, n_cast=len(later_weights)),
        out_shape=(jax.ShapeDtypeStruct((m, n), BF16),
                   jax.ShapeDtypeStruct((m, ZF_WIDTH), F32),
                   *[jax.ShapeDtypeStruct(wl.shape, BF16) for wl in later_weights]),
        grid=(m // tm, nj),
        in_specs=[
            pl.BlockSpec((tm, k), lambda i, j: (i, 0)),
            pl.BlockSpec((1, k), lambda i, j: (0, 0)),
            pl.BlockSpec((k, tn), lambda i, j: (0, j)),
            *cast_specs,
        ],
        out_specs=(
            pl.BlockSpec((tm, tn), lambda i, j: (i, j)),
            pl.BlockSpec((tm, tn), lambda i, j: (i, jnp.where(j <= hf_tile, 0, 1))),
            *cast_specs,
        ),
        scratch_shapes=[pltpu.VMEM((tm, k), BF16)],
        compiler_params=pltpu.CompilerParams(
            dimension_semantics=("arbitrary", "arbitrary"),
            vmem_limit_bytes=VMEM_LIMIT),
        name="in_proj",
    )(x2, norm_w.reshape(1, k), w_bf, *later_weights)
    return outs[0], outs[1], outs[2:]


MXU_LEVELS = 3


def _hgrn_tables():
    t = np.arange(CHUNK)[:, None]
    s = np.arange(CHUNK)[None, :]
    mats, masks = [], []
    for l in range(N_LEVELS):
        h = 1 << l
        start = (t // (2 * h)) * (2 * h)
        ref = start + h - 1
        is_q = (t - start) >= h
        if l < MXU_LEVELS:
            mats.append(np.where(is_q, (s > ref) & (s <= t), (s > t) & (s <= ref)))
        s_start = (s // (2 * h)) * (2 * h)
        masks.append((start == s_start) & is_q & ((s - s_start) < h))
    mats.append(s <= t)
    masks.append(t == s)
    w = np.concatenate(mats, axis=0).astype(np.float32)
    w2 = np.concatenate([w, w], axis=1)
    return w2, np.stack(masks).astype(np.float32)


def _level_exponent(b, h):
    parts = []
    for p in range(b.shape[0] // (2 * h)):
        lo = p * 2 * h
        parts.append(-jnp.abs(b[lo:lo + 2 * h, :] - b[lo + h - 1:lo + h, :]))
    return parts[0] if len(parts) == 1 else jnp.concatenate(parts, axis=0)


N_OPS = N_LEVELS + 2
OP_DIAG, OP_STATE = N_LEVELS, N_LEVELS + 1


def _hgrn_units(hq_ref, hf_ref, hi_ref, hg_ref, lbl_ref, nw_ref, wexp_ref, mask_ref, y_ref,
                st_ref, qx_ref, kx_ref, d_ref, *, first_block, n_chunks):
    @pl.when(first_block)
    def _():
        st_ref[...] = jnp.zeros_like(st_ref)

    logits = lbl_ref[...]
    l0, l1 = logits[0:1, :], logits[1:2, :]
    mx = jnp.maximum(l0, l1)
    e0, e1 = jnp.exp(l0 - mx), jnp.exp(l1 - mx)
    lb = e0 / (e0 + e1)
    nw = nw_ref[...]
    wexp = wexp_ref[...]

    def stage(c):
        slot = c % 2
        r = slice(c * CHUNK, (c + 1) * CHUNK)
        f = jax.nn.sigmoid(hf_ref[r, :])
        g = lb + (1.0 - lb) * f
        lg = jnp.log2(g)
        kk = (1.0 - g).astype(BF16)
        q = (jax.nn.silu(hq_ref[r, :].astype(F32)) * (HG_DK ** -0.5)).astype(BF16)
        lg_hi = lg.astype(BF16)
        lg_lo = (lg - lg_hi.astype(F32)).astype(BF16)
        e_mxu = _dot(wexp, jnp.concatenate([lg_hi, lg_lo], axis=0))
        b = e_mxu[MXU_LEVELS * CHUNK:, :]
        for l in range(N_LEVELS):
            e = e_mxu[l * CHUNK:(l + 1) * CHUNK, :] if l < MXU_LEVELS else _level_exponent(b, 1 << l)
            x = jnp.exp2(e).astype(BF16)
            qx_ref[slot, l] = q * x
            kx_ref[slot, l] = kk * x
        qx_ref[slot, OP_DIAG] = q
        kx_ref[slot, OP_DIAG] = kk
        eb = jnp.exp2(b)
        d_ref[slot] = eb[CHUNK - 1:CHUNK, :]
        qx_ref[slot, OP_STATE] = q * eb.astype(BF16)
        kx_ref[slot, OP_STATE] = kk * jnp.exp2(b[CHUNK - 1:CHUNK, :] - b).astype(BF16)

    def scores(c, h):
        slot, sl = c % 2, slice(h * HG_DK, (h + 1) * HG_DK)
        sc = mask_ref[OP_DIAG] * _dot(qx_ref[slot, OP_DIAG, :, sl], kx_ref[slot, OP_DIAG, :, sl], NT)
        for l in range(N_LEVELS):
            sc = sc + mask_ref[l] * _dot(qx_ref[slot, l, :, sl], kx_ref[slot, l, :, sl], NT)
        return sc.astype(BF16)

    def outputs(c, h, sc):
        slot, sl = c % 2, slice(h * HG_DK, (h + 1) * HG_DK)
        r = slice(c * CHUNK, (c + 1) * CHUNK)
        vh = hi_ref[r, sl]
        st = st_ref[h]
        o = _dot(sc, vh) + _dot(qx_ref[slot, OP_STATE, :, sl], st.astype(BF16), NT)
        st_ref[h] = d_ref[slot, :, sl] * st + _dot(vh, kx_ref[slot, OP_STATE, :, sl], TN)
        y = _rms(o, nw) * jax.nn.silu(hg_ref[r, sl].astype(F32))
        y_ref[r, sl] = y.astype(y_ref.dtype)

    stage(0)
    yield
    for c in range(n_chunks):
        sc_next = scores(c, 0)
        for h in range(HG_HEADS):
            sc = sc_next
            if h + 1 < HG_HEADS:
                sc_next = scores(c, h + 1)
            if h == 0 and c + 1 < n_chunks:
                stage(c + 1)
            outputs(c, h, sc)
            yield


ATT_TK = ATT_TQ + LEFT_CHUNKS * CHUNK
ATT_ROLL = 1024
assert ATT_TQ - 1 + ATT_TK <= ATT_ROLL and ATT_TK - ATT_TQ == 2 * REL_CLIP


def _rel_table(rel_bias):
    rev = rel_bias[:, ::-1].astype(F32) * LOG2E
    edge = jnp.broadcast_to(rev[:, :1], (rel_bias.shape[0], REL_CLIP))
    g = jnp.concatenate([edge, rev[:, :2 * REL_CLIP], edge], axis=1)
    return g.reshape(AT_HEADS // 2, 2, ATT_ROLL)


ATT_PAD = ATT_TK - ATT_TQ
ATT_EDGE = ATT_PAD // ATT_TQ


ATT_RG = 32


def _attn_units(q_ref, k_ref, v_ref, g_ref, o_ref, bias_ref, kp_ref, vp_ref, s_ref, p_ref,
                *, new_head_pair, seq):
    @pl.when(new_head_pair)
    def _():
        qc = lax.broadcasted_iota(jnp.int32, (ATT_TQ, ATT_TK), 0) // CHUNK
        kc = lax.broadcasted_iota(jnp.int32, (ATT_TQ, ATT_TK), 1) // CHUNK
        in_band = (kc >= qc) & (kc <= qc + LEFT_CHUNKS)
        g = g_ref[0]
        for hh in range(2):
            tbl = jnp.broadcast_to(g[hh:hh + 1, :], (ATT_TQ, ATT_ROLL))
            toep = pltpu.roll(tbl, 0, 1, stride=1, stride_axis=0)[:, :ATT_TK]
            rows = slice(hh * ATT_TQ, (hh + 1) * ATT_TQ)
            for n in range(ATT_EDGE + 1):
                first_chunk = (ATT_PAD - n * ATT_TQ) // CHUNK if n < ATT_EDGE else 0
                bias_ref[n, rows, :] = jnp.where(in_band & (kc >= first_chunk), toep, -jnp.inf)

    v_all = v_ref[...]
    head0_all = lax.broadcasted_iota(jnp.int32, v_all.shape, 1) < AT_DH
    ones = jnp.ones_like(v_all)
    pad = jnp.zeros((ATT_PAD, LANES), BF16)
    kp_ref[:ATT_PAD, :] = pad
    kp_ref[ATT_PAD:, :] = k_ref[...]
    vp_ref[0, :ATT_PAD, :] = pad
    vp_ref[1, :ATT_PAD, :] = pad
    vp_ref[0, ATT_PAD:, :] = jnp.where(head0_all, v_all, ones)
    vp_ref[1, ATT_PAD:, :] = jnp.where(head0_all, ones, v_all)

    head0 = lax.broadcasted_iota(jnp.int32, (ATT_TQ, LANES), 1) < AT_DH
    n_blocks = seq // ATT_TQ
    yield

    def scores(i):
        slot = i % 2
        q = q_ref[i * ATT_TQ:(i + 1) * ATT_TQ, :]
        zero = jnp.zeros_like(q)
        qq = jnp.concatenate([jnp.where(head0, q, zero), jnp.where(head0, zero, q)], axis=0)
        s = _dot(qq, kp_ref[i * ATT_TQ:i * ATT_TQ + ATT_TK, :], NT)
        s_ref[slot] = s + bias_ref[min(i, ATT_EDGE)]

    def outputs(i):
        slot = i % 2
        for g in range(2 * ATT_TQ // ATT_RG):
            rows = slice(g * ATT_RG, (g + 1) * ATT_RG)
            s = s_ref[slot, rows, :]
            p_ref[slot, rows, :] = jnp.exp2(s - jnp.max(s, axis=-1, keepdims=True)).astype(BF16)
        keys = slice(i * ATT_TQ, i * ATT_TQ + ATT_TK)
        acc0 = _dot(p_ref[slot, :ATT_TQ, :], vp_ref[0, keys, :])
        acc1 = _dot(p_ref[slot, ATT_TQ:, :], vp_ref[1, keys, :])
        num = jnp.where(head0, acc0, acc1)
        den = pltpu.roll(jnp.where(head0, acc1, acc0), AT_DH, 1)
        o_ref[i * ATT_TQ:(i + 1) * ATT_TQ, :] = (num / den).astype(o_ref.dtype)

    scores(0)
    for i in range(n_blocks):
        if i + 1 < n_blocks:
            scores(i + 1)
        outputs(i)
        yield


def _mixers_kernel(hq_ref, hf_ref, hi_ref, hg_ref, lbl_ref, nw_ref, wexp_ref, mask_ref,
                   q_ref, k_ref, v_ref, g_ref, cast_a_ref, cast_b_ref,
                   y_ref, o_ref, cast_a_out, cast_b_out,
                   st_ref, qx_ref, kx_ref, d_ref, bias_ref, kp_ref, vp_ref, s_ref, p_ref,
                   *, n_chunks, seq, steps_per_pair):
    cast_a_out[...] = cast_a_ref[...].astype(BF16)
    cast_b_out[...] = cast_b_ref[...].astype(BF16)
    step = pl.program_id(0) * pl.num_programs(1) + pl.program_id(1)
    hgrn = _hgrn_units(hq_ref, hf_ref, hi_ref, hg_ref, lbl_ref, nw_ref, wexp_ref, mask_ref,
                       y_ref, st_ref, qx_ref, kx_ref, d_ref,
                       first_block=pl.program_id(1) == 0, n_chunks=n_chunks)
    attn = _attn_units(q_ref, k_ref, v_ref, g_ref, o_ref, bias_ref, kp_ref, vp_ref, s_ref, p_ref,
                       new_head_pair=step % steps_per_pair == 0, seq=seq)
    n_hgrn, n_attn = n_chunks * HG_HEADS, seq // ATT_TQ
    assert n_hgrn % n_attn == 0
    next(hgrn)
    next(attn)
    for _ in range(n_attn):
        next(attn)
        for _ in range(n_hgrn // n_attn):
            next(hgrn)
    assert next(attn, None) is None and next(hgrn, None) is None


def _mixers(z, hf, lb_logits, hg_norm_w, rel_bias, cast_a, cast_b, batch, seq, *, tb=256):
    wexp, masks = _hgrn_tables()
    g = _rel_table(rel_bias)
    nblk = seq // tb
    steps = batch * nblk
    assert steps == AT_HEADS // 2 * batch
    col0 = 4 * HG_WIDTH // LANES
    ncol = AT_WIDTH // LANES

    def step_of(b, i):
        return b * nblk + i

    def hspec(col):
        return pl.BlockSpec((tb, HG_WIDTH), lambda b, i, col=col: (b * nblk + i, col))

    def aspec(which):
        return pl.BlockSpec((seq, LANES), lambda b, i, which=which: (
            step_of(b, i) % batch, col0 + which * ncol + step_of(b, i) // batch))

    def const(shape):
        return pl.BlockSpec(shape, lambda b, i: (0,) * len(shape))

    def cast_spec(w):
        return pl.BlockSpec((w.shape[0] // steps, w.shape[1]), lambda b, i: (step_of(b, i), 0))

    return pl.pallas_call(
        functools.partial(_mixers_kernel, n_chunks=tb // CHUNK, seq=seq, steps_per_pair=batch),
        out_shape=(jax.ShapeDtypeStruct((batch * seq, HG_WIDTH), BF16),
                   jax.ShapeDtypeStruct((batch * seq, AT_WIDTH), BF16),
                   jax.ShapeDtypeStruct(cast_a.shape, BF16),
                   jax.ShapeDtypeStruct(cast_b.shape, BF16)),
        grid=(batch, nblk),
        in_specs=[
            hspec(0), hspec(0), hspec(2), hspec(3),
            const((2, HG_WIDTH)), const((1, HG_DV)), const(wexp.shape), const(masks.shape),
            aspec(0), aspec(1), aspec(2),
            pl.BlockSpec((1, 2, ATT_ROLL), lambda b, i: (step_of(b, i) // batch, 0, 0)),
            cast_spec(cast_a), cast_spec(cast_b),
        ],
        out_specs=(
            pl.BlockSpec((tb, HG_WIDTH), lambda b, i: (b * nblk + i, 0)),
            pl.BlockSpec((seq, LANES), lambda b, i: (step_of(b, i) % batch, step_of(b, i) // batch)),
            cast_spec(cast_a), cast_spec(cast_b),
        ),
        scratch_shapes=[pltpu.VMEM((HG_HEADS, HG_DV, HG_DK), F32),
                        pltpu.VMEM((2, N_OPS, CHUNK, HG_WIDTH), BF16),
                        pltpu.VMEM((2, N_OPS, CHUNK, HG_WIDTH), BF16),
                        pltpu.VMEM((2, 1, HG_WIDTH), F32),
                        pltpu.VMEM((ATT_EDGE + 1, 2 * ATT_TQ, ATT_TK), F32),
                        pltpu.VMEM((ATT_PAD + seq, LANES), BF16),
                        pltpu.VMEM((2, ATT_PAD + seq, LANES), BF16),
                        pltpu.VMEM((2, 2 * ATT_TQ, ATT_TK), F32),
                        pltpu.VMEM((2, 2 * ATT_TQ, ATT_TK), BF16)],
        compiler_params=pltpu.CompilerParams(
            dimension_semantics=("arbitrary", "arbitrary"),
            vmem_limit_bytes=VMEM_LIMIT),
        name="mixers",
    )(z, hf, z, z, lb_logits, hg_norm_w.reshape(1, HG_DV), jnp.asarray(wexp, BF16), jnp.asarray(masks),
      z, z, z, g, cast_a, cast_b)


def _merge_out_kernel(ya_ref, yb_ref, ga0_ref, ga1_ref, gb0_ref, gb1_ref, x_ref,
                      wa_ref, wb_ref, wo_ref, nw_ref, h_ref, u_ref, *, rc):
    for c in range(h_ref.shape[0] // rc):
        r = slice(c * rc, (c + 1) * rc)
        pa = _dot(ya_ref[r, :], wa_ref[...])
        pb = _dot(yb_ref[r, :], wb_ref[...])
        ga = jnp.concatenate([ga0_ref[r, :], ga1_ref[r, :]], axis=1).astype(F32)
        gb = jnp.concatenate([gb0_ref[r, :], gb1_ref[r, :]], axis=1).astype(F32)
        merged = (jax.nn.sigmoid(ga) * pa + jax.nn.sigmoid(gb) * pb).astype(BF16)
        h = x_ref[r, :] + _dot(merged, wo_ref[...])
        h_ref[r, :] = h
        u_ref[r, :] = _rms(h, nw_ref[...]).astype(BF16)


def _merge_out(ya, yb, z, x2, wa_bf, wb_bf, wo_bf, norm_w, *, tm=512, rc=256):
    m, d = x2.shape
    gw = d // 2
    gcol = (4 * HG_WIDTH + 3 * AT_WIDTH) // gw
    assert gcol * gw == 4 * HG_WIDTH + 3 * AT_WIDTH

    def resident(shape):
        return pl.BlockSpec(shape, lambda i: (0, 0), pipeline_mode=pl.Buffered(1))

    def gate(blk):
        return pl.BlockSpec((tm, gw), lambda i, blk=blk: (i, gcol + blk))

    return pl.pallas_call(
        functools.partial(_merge_out_kernel, rc=rc),
        out_shape=(jax.ShapeDtypeStruct((m, d), F32), jax.ShapeDtypeStruct((m, d), BF16)),
        grid=(m // tm,),
        in_specs=[
            pl.BlockSpec((tm, HG_WIDTH), lambda i: (i, 0)),
            pl.BlockSpec((tm, AT_WIDTH), lambda i: (i, 0)),
            gate(0), gate(1), gate(2), gate(3),
            pl.BlockSpec((tm, d), lambda i: (i, 0)),
            resident(wa_bf.shape), resident(wb_bf.shape), resident(wo_bf.shape),
            resident((1, d)),
        ],
        out_specs=(pl.BlockSpec((tm, d), lambda i: (i, 0)), pl.BlockSpec((tm, d), lambda i: (i, 0))),
        compiler_params=pltpu.CompilerParams(
            dimension_semantics=("parallel",),
            vmem_limit_bytes=VMEM_LIMIT),
        name="merge_out",
    )(ya, yb, z, z, z, z, x2, wa_bf, wb_bf, wo_bf, norm_w.reshape(1, d))


MLP_SLAB = 512


def _mlp_kernel(h_ref, u_ref, wu_ref, wd_ref, fw_ref, o_ref):
    f = pl.program_id(1)

    @pl.when(f == 0)
    def _():
        o_ref[...] = jnp.zeros_like(o_ref)

    a = jnp.maximum(_dot(u_ref[...], wu_ref[...]), 0.0)
    a = (a * a).astype(BF16)
    for c in range(o_ref.shape[1] // MLP_SLAB):
        cols = slice(c * MLP_SLAB, (c + 1) * MLP_SLAB)
        o_ref[:, cols] += _dot(a, wd_ref[:, cols])

    @pl.when(f == pl.num_programs(1) - 1)
    def _():
        o_ref[...] = _rms(h_ref[...] + o_ref[...], fw_ref[...])


def _mlp(h, u, wu_bf, wd_bf, final_w, *, tm=1024, tf=512):
    m, d = h.shape
    dff = wu_bf.shape[1]
    return pl.pallas_call(
        _mlp_kernel,
        out_shape=jax.ShapeDtypeStruct((m, d), F32),
        grid=(m // tm, dff // tf),
        in_specs=[
            pl.BlockSpec((tm, d), lambda i, f: (i, 0)),
            pl.BlockSpec((tm, d), lambda i, f: (i, 0)),
            pl.BlockSpec((d, tf), lambda i, f: (0, f)),
            pl.BlockSpec((tf, d), lambda i, f: (f, 0)),
            pl.BlockSpec((1, d), lambda i, f: (0, 0)),
        ],
        out_specs=pl.BlockSpec((tm, d), lambda i, f: (i, 0)),
        compiler_params=pltpu.CompilerParams(
            dimension_semantics=("parallel", "arbitrary"),
            vmem_limit_bytes=VMEM_LIMIT),
        name="mlp",
    )(h, u, wu_bf, wd_bf, final_w.reshape(1, d))


def kernel(x, w_in, lb_logits, hg_norm_w, rel_bias, w_branch_a, w_branch_b, w_out,
           norm_mix_w, norm_mlp_w, w_up, w_down, norm_final_w):
    batch, seq, d = x.shape
    assert d == D_MODEL and seq % ATT_TQ == 0 and w_in.shape[0] == 1
    x2 = x.reshape(batch * seq, d)
    z, hf, (wa_bf, wb_bf, wo_bf) = _in_proj(
        x2, norm_mix_w[0], w_in[0].astype(BF16), (w_branch_a[0], w_branch_b[0], w_out[0]))
    ya, yb, wu_bf, wd_bf = _mixers(z, hf, lb_logits, hg_norm_w[0], rel_bias[0], w_up[0],
                                   w_down[0], batch, seq)
    h, u = _merge_out(ya, yb, z, x2, wa_bf, wb_bf, wo_bf, norm_mlp_w[0])
    out = _mlp(h, u, wu_bf, wd_bf, norm_final_w)
    return out.reshape(batch, seq, d)
```

```python
import functools

import jax
import jax.numpy as jnp
import numpy as np
from jax import lax
from jax.experimental import pallas as pl
from jax.experimental.pallas import tpu as pltpu

D_MODEL = 2048
CHUNK = 64
HG_HEADS = 8
HG_DK = 128
HG_DV = 128
HG_WIDTH = HG_HEADS * HG_DV
AT_HEADS = 16
AT_DH = 64
AT_WIDTH = AT_HEADS * AT_DH
LEFT_CHUNKS = 8
REL_CLIP = 256
EPS = 1e-6
D_IN = 4 * HG_WIDTH + 3 * AT_WIDTH + 2 * D_MODEL

LANES = 128
N_LEVELS = 6
ATT_TQ = 2 * CHUNK

BF16 = jnp.bfloat16
F32 = jnp.float32
VMEM_LIMIT = 60000 * 1024

NN = (((1,), (0,)), ((), ()))
NT = (((1,), (1,)), ((), ()))
TN = (((0,), (0,)), ((), ()))


def _dot(a, b, dims=NN):
    return lax.dot_general(a, b, dims, preferred_element_type=F32)


def _rms(xf, w):
    return xf * lax.rsqrt(jnp.mean(xf * xf, axis=-1, keepdims=True) + EPS) * w


CAST_BLOCKS = 64
IN_SLAB = 256


LOG2E = 1.4426950408889634
AT_QSCALE = AT_DH ** -0.5 * LOG2E


AQ_COLS = (4 * HG_WIDTH, 4 * HG_WIDTH + AT_WIDTH)
ZF_WIDTH = 2 * HG_WIDTH


def _in_proj_kernel(x_ref, nw_ref, w_ref, *rest, n_cast):
    cast_in, (z_ref, zf_ref) = rest[:n_cast], rest[n_cast:n_cast + 2]
    cast_out, u_ref = rest[n_cast + 2:2 * n_cast + 2], rest[2 * n_cast + 2]
    tn = z_ref.shape[1]

    @pl.when(pl.program_id(1) == 0)
    def _():
        u_ref[...] = _rms(x_ref[...], nw_ref[...]).astype(BF16)

    col0 = pl.program_id(1) * tn
    zscale = jnp.where((col0 >= AQ_COLS[0]) & (col0 < AQ_COLS[1]), AT_QSCALE, 1.0).astype(F32)
    for c in range(tn // IN_SLAB):
        cols = slice(c * IN_SLAB, (c + 1) * IN_SLAB)
        acc = _dot(u_ref[...], w_ref[:, cols])
        z_ref[:, cols] = (acc * zscale).astype(z_ref.dtype)
        zf_ref[:, cols] = acc

    for src, dst in zip(cast_in, cast_out):
        dst[...] = src[...].astype(BF16)


def _in_proj(x2, norm_w, w_bf, later_weights, *, tm=1024, tn=1024):
    m, k = x2.shape
    n = w_bf.shape[1]
    nj = n // tn
    assert (m // tm) * nj >= CAST_BLOCKS and tn == HG_WIDTH
    assert AQ_COLS[0] % tn == 0 and AQ_COLS[1] % tn == 0
    hf_tile = HG_WIDTH // tn

    def cast_spec(wl):
        rows, cols = wl.shape
        return pl.BlockSpec((rows // CAST_BLOCKS, cols),
                            lambda i, j: (jnp.minimum(i * nj + j, CAST_BLOCKS - 1), 0))

    cast_specs = [cast_spec(wl) for wl in later_weights]
    outs = pl.pallas_call(
        functools.partial(_in_proj_kernel, n_cast=len(later_weights)),
        out_shape=(jax.ShapeDtypeStruct((m, n), BF16),
                   jax.ShapeDtypeStruct((m, ZF_WIDTH), F32),
                   *[jax.ShapeDtypeStruct(wl.shape, BF16) for wl in later_weights]),
        grid=(m // tm, nj),
        in_specs=[
            pl.BlockSpec((tm, k), lambda i, j: (i, 0)),
            pl.BlockSpec((1, k), lambda i, j: (0, 0)),
            pl.BlockSpec((k, tn), lambda i, j: (0, j)),
            *cast_specs,
        ],
        out_specs=(
            pl.BlockSpec((tm, tn), lambda i, j: (i, j)),
            pl.BlockSpec((tm, tn), lambda i, j: (i, jnp.where(j <= hf_tile, 0, 1))),
            *cast_specs,
        ),
        scratch_shapes=[pltpu.VMEM((tm, k), BF16)],
        compiler_params=pltpu.CompilerParams(
            dimension_semantics=("arbitrary", "arbitrary"),
            vmem_limit_bytes=VMEM_LIMIT),
        name="in_proj",
    )(x2, norm_w.reshape(1, k), w_bf, *later_weights)
    return outs[0], outs[1], outs[2:]


MXU_LEVELS = 3


def _hgrn_tables():
    t = np.arange(CHUNK)[:, None]
    s = np.arange(CHUNK)[None, :]
    mats, masks = [], []
    for l in range(N_LEVELS):
        h = 1 << l
        start = (t // (2 * h)) * (2 * h)
        ref = start + h - 1
        is_q = (t - start) >= h
        if l < MXU_LEVELS:
            mats.append(np.where(is_q, (s > ref) & (s <= t), (s > t) & (s <= ref)))
        s_start = (s // (2 * h)) * (2 * h)
        masks.append((start == s_start) & is_q & ((s - s_start) < h))
    mats.append(s <= t)
    masks.append(t == s)
    w = np.concatenate(mats, axis=0).astype(np.float32)
    w2 = np.concatenate([w, w], axis=1)
    return w2, np.stack(masks).astype(np.float32)


def _level_exponent(b, h):
    parts = []
    for p in range(b.shape[0] // (2 * h)):
        lo = p * 2 * h
        ref = b[lo + h - 1:lo + h, :]
        parts += [ref - b[lo:lo + h, :], b[lo + h:lo + 2 * h, :] - ref]
    return jnp.concatenate(parts, axis=0)


N_OPS = N_LEVELS + 2
OP_DIAG, OP_STATE = N_LEVELS, N_LEVELS + 1


def _hgrn_units(hq_ref, hf_ref, hi_ref, hg_ref, lbl_ref, nw_ref, wexp_ref, mask_ref, y_ref,
                st_ref, qx_ref, kx_ref, d_ref, *, first_block, n_chunks):
    @pl.when(first_block)
    def _():
        st_ref[...] = jnp.zeros_like(st_ref)

    logits = lbl_ref[...]
    l0, l1 = logits[0:1, :], logits[1:2, :]
    mx = jnp.maximum(l0, l1)
    e0, e1 = jnp.exp(l0 - mx), jnp.exp(l1 - mx)
    lb = e0 / (e0 + e1)
    nw = nw_ref[...]
    wexp = wexp_ref[...]

    def stage(c):
        slot = c % 2
        r = slice(c * CHUNK, (c + 1) * CHUNK)
        f = jax.nn.sigmoid(hf_ref[r, :])
        g = lb + (1.0 - lb) * f
        lg = jnp.log2(g)
        kk = (1.0 - g).astype(BF16)
        q = (jax.nn.silu(hq_ref[r, :].astype(F32)) * (HG_DK ** -0.5)).astype(BF16)
        lg_hi = lg.astype(BF16)
        lg_lo = (lg - lg_hi.astype(F32)).astype(BF16)
        e_mxu = _dot(wexp, jnp.concatenate([lg_hi, lg_lo], axis=0))
        b = e_mxu[MXU_LEVELS * CHUNK:, :]
        for l in range(N_LEVELS):
            e = e_mxu[l * CHUNK:(l + 1) * CHUNK, :] if l < MXU_LEVELS else _level_exponent(b, 1 << l)
            x = jnp.exp2(e).astype(BF16)
            qx_ref[slot, l] = q * x
            kx_ref[slot, l] = kk * x
        qx_ref[slot, OP_DIAG] = q
        kx_ref[slot, OP_DIAG] = kk
        eb = jnp.exp2(b)
        d_ref[slot] = eb[CHUNK - 1:CHUNK, :]
        qx_ref[slot, OP_STATE] = q * eb.astype(BF16)
        kx_ref[slot, OP_STATE] = kk * jnp.exp2(b[CHUNK - 1:CHUNK, :] - b).astype(BF16)

    def scores(c, h):
        slot, sl = c % 2, slice(h * HG_DK, (h + 1) * HG_DK)
        sc = mask_ref[OP_DIAG] * _dot(qx_ref[slot, OP_DIAG, :, sl], kx_ref[slot, OP_DIAG, :, sl], NT)
        for l in range(N_LEVELS):
            sc = sc + mask_ref[l] * _dot(qx_ref[slot, l, :, sl], kx_ref[slot, l, :, sl], NT)
        return sc.astype(BF16)

    def outputs(c, h, sc):
        slot, sl = c % 2, slice(h * HG_DK, (h + 1) * HG_DK)
        r = slice(c * CHUNK, (c + 1) * CHUNK)
        vh = hi_ref[r, sl]
        st = st_ref[h]
        o = _dot(sc, vh) + _dot(qx_ref[slot, OP_STATE, :, sl], st.astype(BF16), NT)
        st_ref[h] = d_ref[slot, :, sl] * st + _dot(vh, kx_ref[slot, OP_STATE, :, sl], TN)
        y = _rms(o, nw) * jax.nn.silu(hg_ref[r, sl].astype(F32))
        y_ref[r, sl] = y.astype(y_ref.dtype)

    stage(0)
    yield
    for c in range(n_chunks):
        sc_next = scores(c, 0)
        for h in range(HG_HEADS):
            sc = sc_next
            if h + 1 < HG_HEADS:
                sc_next = scores(c, h + 1)
            if h == 0 and c + 1 < n_chunks:
                stage(c + 1)
            outputs(c, h, sc)
            yield


ATT_TK = ATT_TQ + LEFT_CHUNKS * CHUNK
ATT_ROLL = 1024
assert ATT_TQ - 1 + ATT_TK <= ATT_ROLL and ATT_TK - ATT_TQ == 2 * REL_CLIP


def _rel_table(rel_bias):
    rev = rel_bias[:, ::-1].astype(F32) * LOG2E
    edge = jnp.broadcast_to(rev[:, :1], (rel_bias.shape[0], REL_CLIP))
    g = jnp.concatenate([edge, rev[:, :2 * REL_CLIP], edge], axis=1)
    return g.reshape(AT_HEADS // 2, 2, ATT_ROLL)


ATT_PAD = ATT_TK - ATT_TQ
ATT_EDGE = ATT_PAD // ATT_TQ


ATT_RG = 32


def _attn_units(q_ref, k_ref, v_ref, g_ref, o_ref, bias_ref, kp_ref, vp_ref, s_ref, p_ref,
                *, new_head_pair, seq):
    @pl.when(new_head_pair)
    def _():
        qc = lax.broadcasted_iota(jnp.int32, (ATT_TQ, ATT_TK), 0) // CHUNK
        kc = lax.broadcasted_iota(jnp.int32, (ATT_TQ, ATT_TK), 1) // CHUNK
        in_band = (kc >= qc) & (kc <= qc + LEFT_CHUNKS)
        g = g_ref[0]
        for hh in range(2):
            tbl = jnp.broadcast_to(g[hh:hh + 1, :], (ATT_TQ, ATT_ROLL))
            toep = pltpu.roll(tbl, 0, 1, stride=1, stride_axis=0)[:, :ATT_TK]
            rows = slice(hh * ATT_TQ, (hh + 1) * ATT_TQ)
            for n in range(ATT_EDGE + 1):
                first_chunk = (ATT_PAD - n * ATT_TQ) // CHUNK if n < ATT_EDGE else 0
                bias_ref[n, rows, :] = jnp.where(in_band & (kc >= first_chunk), toep, -jnp.inf)

    v_all = v_ref[...]
    head0_all = lax.broadcasted_iota(jnp.int32, v_all.shape, 1) < AT_DH
    ones = jnp.ones_like(v_all)
    pad = jnp.zeros((ATT_PAD, LANES), BF16)
    kp_ref[:ATT_PAD, :] = pad
    kp_ref[ATT_PAD:, :] = k_ref[...]
    vp_ref[0, :ATT_PAD, :] = pad
    vp_ref[1, :ATT_PAD, :] = pad
    vp_ref[0, ATT_PAD:, :] = jnp.where(head0_all, v_all, ones)
    vp_ref[1, ATT_PAD:, :] = jnp.where(head0_all, ones, v_all)

    head0 = lax.broadcasted_iota(jnp.int32, (ATT_TQ, LANES), 1) < AT_DH
    n_blocks = seq // ATT_TQ
    yield

    def scores(i):
        slot = i % 2
        q = q_ref[i * ATT_TQ:(i + 1) * ATT_TQ, :]
        zero = jnp.zeros_like(q)
        qq = jnp.concatenate([jnp.where(head0, q, zero), jnp.where(head0, zero, q)], axis=0)
        s = _dot(qq, kp_ref[i * ATT_TQ:i * ATT_TQ + ATT_TK, :], NT)
        s_ref[slot] = s + bias_ref[min(i, ATT_EDGE)]

    def outputs(i):
        slot = i % 2
        for g in range(2 * ATT_TQ // ATT_RG):
            rows = slice(g * ATT_RG, (g + 1) * ATT_RG)
            s = s_ref[slot, rows, :]
            p_ref[slot, rows, :] = jnp.exp2(s - jnp.max(s, axis=-1, keepdims=True)).astype(BF16)
        keys = slice(i * ATT_TQ, i * ATT_TQ + ATT_TK)
        acc0 = _dot(p_ref[slot, :ATT_TQ, :], vp_ref[0, keys, :])
        acc1 = _dot(p_ref[slot, ATT_TQ:, :], vp_ref[1, keys, :])
        num = jnp.where(head0, acc0, acc1)
        den = pltpu.roll(jnp.where(head0, acc1, acc0), AT_DH, 1)
        o_ref[i * ATT_TQ:(i + 1) * ATT_TQ, :] = (num / den).astype(o_ref.dtype)

    scores(0)
    for i in range(n_blocks):
        if i + 1 < n_blocks:
            scores(i + 1)
        outputs(i)
        yield


def _mixers_kernel(hq_ref, hf_ref, hi_ref, hg_ref, lbl_ref, nw_ref, wexp_ref, mask_ref,
                   q_ref, k_ref, v_ref, g_ref, cast_a_ref, cast_b_ref,
                   y_ref, o_ref, cast_a_out, cast_b_out,
                   st_ref, qx_ref, kx_ref, d_ref, bias_ref, kp_ref, vp_ref, s_ref, p_ref,
                   *, n_chunks, seq, steps_per_pair):
    cast_a_out[...] = cast_a_ref[...].astype(BF16)
    cast_b_out[...] = cast_b_ref[...].astype(BF16)
    step = pl.program_id(0) * pl.num_programs(1) + pl.program_id(1)
    hgrn = _hgrn_units(hq_ref, hf_ref, hi_ref, hg_ref, lbl_ref, nw_ref, wexp_ref, mask_ref,
                       y_ref, st_ref, qx_ref, kx_ref, d_ref,
                       first_block=pl.program_id(1) == 0, n_chunks=n_chunks)
    attn = _attn_units(q_ref, k_ref, v_ref, g_ref, o_ref, bias_ref, kp_ref, vp_ref, s_ref, p_ref,
                       new_head_pair=step % steps_per_pair == 0, seq=seq)
    n_hgrn, n_attn = n_chunks * HG_HEADS, seq // ATT_TQ
    assert n_hgrn % n_attn == 0
    next(hgrn)
    next(attn)
    for _ in range(n_attn):
        next(attn)
        for _ in range(n_hgrn // n_attn):
            next(hgrn)
    assert next(attn, None) is None and next(hgrn, None) is None


def _mixers(z, hf, lb_logits, hg_norm_w, rel_bias, cast_a, cast_b, batch, seq, *, tb=256):
    wexp, masks = _hgrn_tables()
    g = _rel_table(rel_bias)
    nblk = seq // tb
    steps = batch * nblk
    assert steps == AT_HEADS // 2 * batch
    col0 = 4 * HG_WIDTH // LANES
    ncol = AT_WIDTH // LANES

    def step_of(b, i):
        return b * nblk + i

    def hspec(col):
        return pl.BlockSpec((tb, HG_WIDTH), lambda b, i, col=col: (b * nblk + i, col))

    def aspec(which):
        return pl.BlockSpec((seq, LANES), lambda b, i, which=which: (
            step_of(b, i) % batch, col0 + which * ncol + step_of(b, i) // batch))

    def const(shape):
        return pl.BlockSpec(shape, lambda b, i: (0,) * len(shape))

    def cast_spec(w):
        return pl.BlockSpec((w.shape[0] // steps, w.shape[1]), lambda b, i: (step_of(b, i), 0))

    return pl.pallas_call(
        functools.partial(_mixers_kernel, n_chunks=tb // CHUNK, seq=seq, steps_per_pair=batch),
        out_shape=(jax.ShapeDtypeStruct((batch * seq, HG_WIDTH), BF16),
                   jax.ShapeDtypeStruct((batch * seq, AT_WIDTH), BF16),
                   jax.ShapeDtypeStruct(cast_a.shape, BF16),
                   jax.ShapeDtypeStruct(cast_b.shape, BF16)),
        grid=(batch, nblk),
        in_specs=[
            hspec(0), hspec(0), hspec(2), hspec(3),
            const((2, HG_WIDTH)), const((1, HG_DV)), const(wexp.shape), const(masks.shape),
            aspec(0), aspec(1), aspec(2),
            pl.BlockSpec((1, 2, ATT_ROLL), lambda b, i: (step_of(b, i) // batch, 0, 0)),
            cast_spec(cast_a), cast_spec(cast_b),
        ],
        out_specs=(
            pl.BlockSpec((tb, HG_WIDTH), lambda b, i: (b * nblk + i, 0)),
            pl.BlockSpec((seq, LANES), lambda b, i: (step_of(b, i) % batch, step_of(b, i) // batch)),
            cast_spec(cast_a), cast_spec(cast_b),
        ),
        scratch_shapes=[pltpu.VMEM((HG_HEADS, HG_DV, HG_DK), F32),
                        pltpu.VMEM((2, N_OPS, CHUNK, HG_WIDTH), BF16),
                        pltpu.VMEM((2, N_OPS, CHUNK, HG_WIDTH), BF16),
                        pltpu.VMEM((2, 1, HG_WIDTH), F32),
                        pltpu.VMEM((ATT_EDGE + 1, 2 * ATT_TQ, ATT_TK), F32),
                        pltpu.VMEM((ATT_PAD + seq, LANES), BF16),
                        pltpu.VMEM((2, ATT_PAD + seq, LANES), BF16),
                        pltpu.VMEM((2, 2 * ATT_TQ, ATT_TK), F32),
                        pltpu.VMEM((2, 2 * ATT_TQ, ATT_TK), BF16)],
        compiler_params=pltpu.CompilerParams(
            dimension_semantics=("arbitrary", "arbitrary"),
            vmem_limit_bytes=VMEM_LIMIT),
        name="mixers",
    )(z, hf, z, z, lb_logits, hg_norm_w.reshape(1, HG_DV), jnp.asarray(wexp, BF16), jnp.asarray(masks),
      z, z, z, g, cast_a, cast_b)


def _merge_out_kernel(ya_ref, yb_ref, ga0_ref, ga1_ref, gb0_ref, gb1_ref, x_ref,
                      wa_ref, wb_ref, wo_ref, nw_ref, h_ref, u_ref, *, rc):
    for c in range(h_ref.shape[0] // rc):
        r = slice(c * rc, (c + 1) * rc)
        pa = _dot(ya_ref[r, :], wa_ref[...])
        pb = _dot(yb_ref[r, :], wb_ref[...])
        ga = jnp.concatenate([ga0_ref[r, :], ga1_ref[r, :]], axis=1).astype(F32)
        gb = jnp.concatenate([gb0_ref[r, :], gb1_ref[r, :]], axis=1).astype(F32)
        merged = (jax.nn.sigmoid(ga) * pa + jax.nn.sigmoid(gb) * pb).astype(BF16)
        h = x_ref[r, :] + _dot(merged, wo_ref[...])
        h_ref[r, :] = h
        u_ref[r, :] = _rms(h, nw_ref[...]).astype(BF16)


def _merge_out(ya, yb, z, x2, wa_bf, wb_bf, wo_bf, norm_w, *, tm=512, rc=256):
    m, d = x2.shape
    gw = d // 2
    gcol = (4 * HG_WIDTH + 3 * AT_WIDTH) // gw
    assert gcol * gw == 4 * HG_WIDTH + 3 * AT_WIDTH

    def resident(shape):
        return pl.BlockSpec(shape, lambda i: (0, 0), pipeline_mode=pl.Buffered(1))

    def gate(blk):
        return pl.BlockSpec((tm, gw), lambda i, blk=blk: (i, gcol + blk))

    return pl.pallas_call(
        functools.partial(_merge_out_kernel, rc=rc),
        out_shape=(jax.ShapeDtypeStruct((m, d), F32), jax.ShapeDtypeStruct((m, d), BF16)),
        grid=(m // tm,),
        in_specs=[
            pl.BlockSpec((tm, HG_WIDTH), lambda i: (i, 0)),
            pl.BlockSpec((tm, AT_WIDTH), lambda i: (i, 0)),
            gate(0), gate(1), gate(2), gate(3),
            pl.BlockSpec((tm, d), lambda i: (i, 0)),
            resident(wa_bf.shape), resident(wb_bf.shape), resident(wo_bf.shape),
            resident((1, d)),
        ],
        out_specs=(pl.BlockSpec((tm, d), lambda i: (i, 0)), pl.BlockSpec((tm, d), lambda i: (i, 0))),
        compiler_params=pltpu.CompilerParams(
            dimension_semantics=("parallel",),
            vmem_limit_bytes=VMEM_LIMIT),
        name="merge_out",
    )(ya, yb, z, z, z, z, x2, wa_bf, wb_bf, wo_bf, norm_w.reshape(1, d))


MLP_SLAB = 512


def _mlp_kernel(h_ref, u_ref, wu_ref, wd_ref, fw_ref, o_ref):
    f = pl.program_id(1)

    @pl.when(f == 0)
    def _():
        o_ref[...] = jnp.zeros_like(o_ref)

    a = jnp.maximum(_dot(u_ref[...], wu_ref[...]), 0.0)
    a = (a * a).astype(BF16)
    for c in range(o_ref.shape[1] // MLP_SLAB):
        cols = slice(c * MLP_SLAB, (c + 1) * MLP_SLAB)
        o_ref[:, cols] += _dot(a, wd_ref[:, cols])

    @pl.when(f == pl.num_programs(1) - 1)
    def _():
        o_ref[...] = _rms(h_ref[...] + o_ref[...], fw_ref[...])


def _mlp(h, u, wu_bf, wd_bf, final_w, *, tm=1024, tf=512):
    m, d = h.shape
    dff = wu_bf.shape[1]
    return pl.pallas_call(
        _mlp_kernel,
        out_shape=jax.ShapeDtypeStruct((m, d), F32),
        grid=(m // tm, dff // tf),
        in_specs=[
            pl.BlockSpec((tm, d), lambda i, f: (i, 0)),
            pl.BlockSpec((tm, d), lambda i, f: (i, 0)),
            pl.BlockSpec((d, tf), lambda i, f: (0, f)),
            pl.BlockSpec((tf, d), lambda i, f: (f, 0)),
            pl.BlockSpec((1, d), lambda i, f: (0, 0)),
        ],
        out_specs=pl.BlockSpec((tm, d), lambda i, f: (i, 0)),
        compiler_params=pltpu.CompilerParams(
            dimension_semantics=("parallel", "arbitrary"),
            vmem_limit_bytes=VMEM_LIMIT),
        name="mlp",
    )(h, u, wu_bf, wd_bf, final_w.reshape(1, d))


def kernel(x, w_in, lb_logits, hg_norm_w, rel_bias, w_branch_a, w_branch_b, w_out,
           norm_mix_w, norm_mlp_w, w_up, w_down, norm_final_w):
    batch, seq, d = x.shape
    assert d == D_MODEL and seq % ATT_TQ == 0 and w_in.shape[0] == 1
    x2 = x.reshape(batch * seq, d)
    z, hf, (wa_bf, wb_bf, wo_bf) = _in_proj(
        x2, norm_mix_w[0], w_in[0].astype(BF16), (w_branch_a[0], w_branch_b[0], w_out[0]))
    ya, yb, wu_bf, wd_bf = _mixers(z, hf, lb_logits, hg_norm_w[0], rel_bias[0], w_up[0],
                                   w_down[0], batch, seq)
    h, u = _merge_out(ya, yb, z, x2, wa_bf, wb_bf, wo_bf, norm_mlp_w[0])
    out = _mlp(h, u, wu_bf, wd_bf, norm_final_w)
    return out.reshape(batch, seq, d)
```

```python
import functools

import jax
import jax.numpy as jnp
import numpy as np
from jax import lax
from jax.experimental import pallas as pl
from jax.experimental.pallas import tpu as pltpu

D_MODEL = 2048
CHUNK = 64
HG_HEADS = 8
HG_DK = 128
HG_DV = 128
HG_WIDTH = HG_HEADS * HG_DV
AT_HEADS = 16
AT_DH = 64
AT_WIDTH = AT_HEADS * AT_DH
LEFT_CHUNKS = 8
REL_CLIP = 256
EPS = 1e-6
D_IN = 4 * HG_WIDTH + 3 * AT_WIDTH + 2 * D_MODEL

LANES = 128
N_LEVELS = 6
ATT_TQ = 2 * CHUNK

BF16 = jnp.bfloat16
F32 = jnp.float32
VMEM_LIMIT = 60000 * 1024

NN = (((1,), (0,)), ((), ()))
NT = (((1,), (1,)), ((), ()))
TN = (((0,), (0,)), ((), ()))


def _dot(a, b, dims=NN):
    return lax.dot_general(a, b, dims, preferred_element_type=F32)


def _rms(xf, w):
    return xf * lax.rsqrt(jnp.mean(xf * xf, axis=-1, keepdims=True) + EPS) * w


CAST_BLOCKS = 64
IN_SLAB = 256


LOG2E = 1.4426950408889634
AT_QSCALE = AT_DH ** -0.5 * LOG2E


AQ_COLS = (4 * HG_WIDTH, 4 * HG_WIDTH + AT_WIDTH)
ZF_WIDTH = 2 * HG_WIDTH


def _in_proj_kernel(x_ref, nw_ref, w_ref, *rest, n_cast):
    cast_in, (z_ref, zf_ref) = rest[:n_cast], rest[n_cast:n_cast + 2]
    cast_out, u_ref = rest[n_cast + 2:2 * n_cast + 2], rest[2 * n_cast + 2]
    tn = z_ref.shape[1]

    @pl.when(pl.program_id(1) == 0)
    def _():
        u_ref[...] = _rms(x_ref[...], nw_ref[...]).astype(BF16)

    col0 = pl.program_id(1) * tn
    zscale = jnp.where((col0 >= AQ_COLS[0]) & (col0 < AQ_COLS[1]), AT_QSCALE, 1.0).astype(F32)
    for c in range(tn // IN_SLAB):
        cols = slice(c * IN_SLAB, (c + 1) * IN_SLAB)
        acc = _dot(u_ref[...], w_ref[:, cols])
        z_ref[:, cols] = (acc * zscale).astype(z_ref.dtype)
        zf_ref[:, cols] = acc

    for src, dst in zip(cast_in, cast_out):
        dst[...] = src[...].astype(BF16)


def _in_proj(x2, norm_w, w_bf, later_weights, *, tm=1024, tn=1024):
    m, k = x2.shape
    n = w_bf.shape[1]
    nj = n // tn
    assert (m // tm) * nj >= CAST_BLOCKS and tn == HG_WIDTH
    assert AQ_COLS[0] % tn == 0 and AQ_COLS[1] % tn == 0
    hf_tile = HG_WIDTH // tn

    def cast_spec(wl):
        rows, cols = wl.shape
        return pl.BlockSpec((rows // CAST_BLOCKS, cols),
                            lambda i, j: (jnp.minimum(i * nj + j, CAST_BLOCKS - 1), 0))

    cast_specs = [cast_spec(wl) for wl in later_weights]
    outs = pl.pallas_call(
        functools.partial(_in_proj_kernel, n_cast=len(later_weights)),
        out_shape=(jax.ShapeDtypeStruct((m, n), BF16),
                   jax.ShapeDtypeStruct((m, ZF_WIDTH), F32),
                   *[jax.ShapeDtypeStruct(wl.shape, BF16) for wl in later_weights]),
        grid=(m // tm, nj),
        in_specs=[
            pl.BlockSpec((tm, k), lambda i, j: (i, 0)),
            pl.BlockSpec((1, k), lambda i, j: (0, 0)),
            pl.BlockSpec((k, tn), lambda i, j: (0, j)),
            *cast_specs,
        ],
        out_specs=(
            pl.BlockSpec((tm, tn), lambda i, j: (i, j)),
            pl.BlockSpec((tm, tn), lambda i, j: (i, jnp.where(j <= hf_tile, 0, 1))),
            *cast_specs,
        ),
        scratch_shapes=[pltpu.VMEM((tm, k), BF16)],
        compiler_params=pltpu.CompilerParams(
            dimension_semantics=("arbitrary", "arbitrary"),
            vmem_limit_bytes=VMEM_LIMIT),
        name="in_proj",
    )(x2, norm_w.reshape(1, k), w_bf, *later_weights)
    return outs[0], outs[1], outs[2:]


MXU_LEVELS = 3


def _hgrn_tables():
    t = np.arange(CHUNK)[:, None]
    s = np.arange(CHUNK)[None, :]
    mats, masks = [], []
    for l in range(N_LEVELS):
        h = 1 << l
        start = (t // (2 * h)) * (2 * h)
        ref = start + h - 1
        is_q = (t - start) >= h
        if l < MXU_LEVELS:
            mats.append(np.where(is_q, (s > ref) & (s <= t), (s > t) & (s <= ref)))
        s_start = (s // (2 * h)) * (2 * h)
        masks.append((start == s_start) & is_q & ((s - s_start) < h))
    mats.append(s <= t)
    masks.append(t == s)
    w = np.concatenate(mats, axis=0).astype(np.float32)
    w2 = np.concatenate([w, w], axis=1)
    return w2, np.stack(masks).astype(np.float32)


def _level_exponent(b, h):
    parts = []
    for p in range(b.shape[0] // (2 * h)):
        lo = p * 2 * h
        ref = b[lo + h - 1:lo + h, :]
        parts += [ref - b[lo:lo + h, :], b[lo + h:lo + 2 * h, :] - ref]
    return jnp.concatenate(parts, axis=0)


N_OPS = N_LEVELS + 2
OP_DIAG, OP_STATE = N_LEVELS, N_LEVELS + 1


def _hgrn_units(hq_ref, hf_ref, hi_ref, hg_ref, lbl_ref, nw_ref, wexp_ref, mask_ref, y_ref,
                st_ref, qx_ref, kx_ref, d_ref, *, first_block, n_chunks):
    @pl.when(first_block)
    def _():
        st_ref[...] = jnp.zeros_like(st_ref)

    logits = lbl_ref[...]
    l0, l1 = logits[0:1, :], logits[1:2, :]
    mx = jnp.maximum(l0, l1)
    e0, e1 = jnp.exp(l0 - mx), jnp.exp(l1 - mx)
    lb = e0 / (e0 + e1)
    nw = nw_ref[...]
    wexp = wexp_ref[...]

    def stage(c):
        slot = c % 2
        r = slice(c * CHUNK, (c + 1) * CHUNK)
        f = jax.nn.sigmoid(hf_ref[r, :])
        g = lb + (1.0 - lb) * f
        lg = jnp.log2(g)
        kk = (1.0 - g).astype(BF16)
        q = (jax.nn.silu(hq_ref[r, :].astype(F32)) * (HG_DK ** -0.5)).astype(BF16)
        lg_hi = lg.astype(BF16)
        lg_lo = (lg - lg_hi.astype(F32)).astype(BF16)
        e_mxu = _dot(wexp, jnp.concatenate([lg_hi, lg_lo], axis=0))
        b = e_mxu[MXU_LEVELS * CHUNK:, :]
        def put(ref, op, val):
            for hh in range(HG_HEADS):
                ref[slot, op, hh] = val[:, hh * HG_DK:(hh + 1) * HG_DK]

        put(qx_ref, OP_DIAG, q)
        put(kx_ref, OP_DIAG, kk)
        for l in range(N_LEVELS):
            e = e_mxu[l * CHUNK:(l + 1) * CHUNK, :] if l < MXU_LEVELS else _level_exponent(b, 1 << l)
            x = jnp.exp2(e).astype(BF16)
            put(qx_ref, l, q * x)
            put(kx_ref, l, kk * x)
        eb = jnp.exp2(b)
        d_ref[slot] = eb[CHUNK - 1:CHUNK, :]
        put(qx_ref, OP_STATE, q * eb.astype(BF16))
        put(kx_ref, OP_STATE, kk * jnp.exp2(b[CHUNK - 1:CHUNK, :] - b).astype(BF16))

    def scores(c, h):
        slot = c % 2
        sc = mask_ref[OP_DIAG] * _dot(qx_ref[slot, OP_DIAG, h], kx_ref[slot, OP_DIAG, h], NT)
        for l in range(N_LEVELS):
            sc = sc + mask_ref[l] * _dot(qx_ref[slot, l, h], kx_ref[slot, l, h], NT)
        return sc.astype(BF16)

    def outputs(c, h, sc):
        slot, sl = c % 2, slice(h * HG_DK, (h + 1) * HG_DK)
        r = slice(c * CHUNK, (c + 1) * CHUNK)
        vh = hi_ref[r, sl]
        st = st_ref[h]
        o = _dot(sc, vh) + _dot(qx_ref[slot, OP_STATE, h], st.astype(BF16), NT)
        st_ref[h] = d_ref[slot, :, sl] * st + _dot(vh, kx_ref[slot, OP_STATE, h], TN)
        y = _rms(o, nw) * jax.nn.silu(hg_ref[r, sl].astype(F32))
        y_ref[r, sl] = y.astype(y_ref.dtype)

    stage(0)
    yield
    for c in range(n_chunks):
        sc_next = scores(c, 0)
        for h in range(HG_HEADS):
            sc = sc_next
            if h + 1 < HG_HEADS:
                sc_next = scores(c, h + 1)
            if h == 0 and c + 1 < n_chunks:
                stage(c + 1)
            outputs(c, h, sc)
            yield


ATT_TK = ATT_TQ + LEFT_CHUNKS * CHUNK
ATT_ROLL = 1024
assert ATT_TQ - 1 + ATT_TK <= ATT_ROLL and ATT_TK - ATT_TQ == 2 * REL_CLIP


def _rel_table(rel_bias):
    rev = rel_bias[:, ::-1].astype(F32) * LOG2E
    edge = jnp.broadcast_to(rev[:, :1], (rel_bias.shape[0], REL_CLIP))
    g = jnp.concatenate([edge, rev[:, :2 * REL_CLIP], edge], axis=1)
    return g.reshape(AT_HEADS // 2, 2, ATT_ROLL)


ATT_PAD = ATT_TK - ATT_TQ
ATT_EDGE = ATT_PAD // ATT_TQ


ATT_RG = 32


def _attn_units(q_ref, k_ref, v_ref, g_ref, o_ref, bias_ref, kp_ref, vp_ref, s_ref, p_ref,
                *, new_head_pair, seq):
    @pl.when(new_head_pair)
    def _():
        qc = lax.broadcasted_iota(jnp.int32, (ATT_TQ, ATT_TK), 0) // CHUNK
        kc = lax.broadcasted_iota(jnp.int32, (ATT_TQ, ATT_TK), 1) // CHUNK
        in_band = (kc >= qc) & (kc <= qc + LEFT_CHUNKS)
        g = g_ref[0]
        for hh in range(2):
            tbl = jnp.broadcast_to(g[hh:hh + 1, :], (ATT_TQ, ATT_ROLL))
            toep = pltpu.roll(tbl, 0, 1, stride=1, stride_axis=0)[:, :ATT_TK]
            rows = slice(hh * ATT_TQ, (hh + 1) * ATT_TQ)
            for n in range(ATT_EDGE + 1):
                first_chunk = (ATT_PAD - n * ATT_TQ) // CHUNK if n < ATT_EDGE else 0
                bias_ref[n, rows, :] = jnp.where(in_band & (kc >= first_chunk), toep, -jnp.inf)

    v_all = v_ref[...]
    head0_all = lax.broadcasted_iota(jnp.int32, v_all.shape, 1) < AT_DH
    ones = jnp.ones_like(v_all)
    pad = jnp.zeros((ATT_PAD, LANES), BF16)
    kp_ref[:ATT_PAD, :] = pad
    kp_ref[ATT_PAD:, :] = k_ref[...]
    vp_ref[0, :ATT_PAD, :] = pad
    vp_ref[1, :ATT_PAD, :] = pad
    vp_ref[0, ATT_PAD:, :] = jnp.where(head0_all, v_all, ones)
    vp_ref[1, ATT_PAD:, :] = jnp.where(head0_all, ones, v_all)

    head0 = lax.broadcasted_iota(jnp.int32, (ATT_TQ, LANES), 1) < AT_DH
    n_blocks = seq // ATT_TQ
    yield

    def scores(i):
        slot = i % 2
        q = q_ref[i * ATT_TQ:(i + 1) * ATT_TQ, :]
        zero = jnp.zeros_like(q)
        qq = jnp.concatenate([jnp.where(head0, q, zero), jnp.where(head0, zero, q)], axis=0)
        s = _dot(qq, kp_ref[i * ATT_TQ:i * ATT_TQ + ATT_TK, :], NT)
        s_ref[slot] = s + bias_ref[min(i, ATT_EDGE)]

    def outputs(i):
        slot = i % 2
        for g in range(2 * ATT_TQ // ATT_RG):
            rows = slice(g * ATT_RG, (g + 1) * ATT_RG)
            s = s_ref[slot, rows, :]
            p_ref[slot, rows, :] = jnp.exp2(s - jnp.max(s, axis=-1, keepdims=True)).astype(BF16)
        keys = slice(i * ATT_TQ, i * ATT_TQ + ATT_TK)
        acc0 = _dot(p_ref[slot, :ATT_TQ, :], vp_ref[0, keys, :])
        acc1 = _dot(p_ref[slot, ATT_TQ:, :], vp_ref[1, keys, :])
        num = jnp.where(head0, acc0, acc1)
        den = pltpu.roll(jnp.where(head0, acc1, acc0), AT_DH, 1)
        o_ref[i * ATT_TQ:(i + 1) * ATT_TQ, :] = (num / den).astype(o_ref.dtype)

    scores(0)
    for i in range(n_blocks):
        if i + 1 < n_blocks:
            scores(i + 1)
        outputs(i)
        yield


def _mixers_kernel(hq_ref, hf_ref, hi_ref, hg_ref, lbl_ref, nw_ref, wexp_ref, mask_ref,
                   q_ref, k_ref, v_ref, g_ref, cast_a_ref, cast_b_ref,
                   y_ref, o_ref, cast_a_out, cast_b_out,
                   st_ref, qx_ref, kx_ref, d_ref, bias_ref, kp_ref, vp_ref, s_ref, p_ref,
                   *, n_chunks, seq, steps_per_pair):
    cast_a_out[...] = cast_a_ref[...].astype(BF16)
    cast_b_out[...] = cast_b_ref[...].astype(BF16)
    step = pl.program_id(0) * pl.num_programs(1) + pl.program_id(1)
    hgrn = _hgrn_units(hq_ref, hf_ref, hi_ref, hg_ref, lbl_ref, nw_ref, wexp_ref, mask_ref,
                       y_ref, st_ref, qx_ref, kx_ref, d_ref,
                       first_block=pl.program_id(1) == 0, n_chunks=n_chunks)
    attn = _attn_units(q_ref, k_ref, v_ref, g_ref, o_ref, bias_ref, kp_ref, vp_ref, s_ref, p_ref,
                       new_head_pair=step % steps_per_pair == 0, seq=seq)
    n_hgrn, n_attn = n_chunks * HG_HEADS, seq // ATT_TQ
    assert n_hgrn % n_attn == 0
    next(hgrn)
    next(attn)
    for _ in range(n_attn):
        next(attn)
        for _ in range(n_hgrn // n_attn):
            next(hgrn)
    assert next(attn, None) is None and next(hgrn, None) is None


def _mixers(z, hf, lb_logits, hg_norm_w, rel_bias, cast_a, cast_b, batch, seq, *, tb=256):
    wexp, masks = _hgrn_tables()
    g = _rel_table(rel_bias)
    nblk = seq // tb
    steps = batch * nblk
    assert steps == AT_HEADS // 2 * batch
    col0 = 4 * HG_WIDTH // LANES
    ncol = AT_WIDTH // LANES

    def step_of(b, i):
        return b * nblk + i

    def hspec(col):
        return pl.BlockSpec((tb, HG_WIDTH), lambda b, i, col=col: (b * nblk + i, col))

    def aspec(which):
        return pl.BlockSpec((seq, LANES), lambda b, i, which=which: (
            step_of(b, i) % batch, col0 + which * ncol + step_of(b, i) // batch))

    def const(shape):
        return pl.BlockSpec(shape, lambda b, i: (0,) * len(shape))

    def cast_spec(w):
        return pl.BlockSpec((w.shape[0] // steps, w.shape[1]), lambda b, i: (step_of(b, i), 0))

    return pl.pallas_call(
        functools.partial(_mixers_kernel, n_chunks=tb // CHUNK, seq=seq, steps_per_pair=batch),
        out_shape=(jax.ShapeDtypeStruct((batch * seq, HG_WIDTH), BF16),
                   jax.ShapeDtypeStruct((batch * seq, AT_WIDTH), BF16),
                   jax.ShapeDtypeStruct(cast_a.shape, BF16),
                   jax.ShapeDtypeStruct(cast_b.shape, BF16)),
        grid=(batch, nblk),
        in_specs=[
            hspec(0), hspec(0), hspec(2), hspec(3),
            const((2, HG_WIDTH)), const((1, HG_DV)), const(wexp.shape), const(masks.shape),
            aspec(0), aspec(1), aspec(2),
            pl.BlockSpec((1, 2, ATT_ROLL), lambda b, i: (step_of(b, i) // batch, 0, 0)),
            cast_spec(cast_a), cast_spec(cast_b),
        ],
        out_specs=(
            pl.BlockSpec((tb, HG_WIDTH), lambda b, i: (b * nblk + i, 0)),
            pl.BlockSpec((seq, LANES), lambda b, i: (step_of(b, i) % batch, step_of(b, i) // batch)),
            cast_spec(cast_a), cast_spec(cast_b),
        ),
        scratch_shapes=[pltpu.VMEM((HG_HEADS, HG_DV, HG_DK), F32),
                        pltpu.VMEM((2, N_OPS, HG_HEADS, CHUNK, HG_DK), BF16),
                        pltpu.VMEM((2, N_OPS, HG_HEADS, CHUNK, HG_DK), BF16),
                        pltpu.VMEM((2, 1, HG_WIDTH), F32),
                        pltpu.VMEM((ATT_EDGE + 1, 2 * ATT_TQ, ATT_TK), F32),
                        pltpu.VMEM((ATT_PAD + seq, LANES), BF16),
                        pltpu.VMEM((2, ATT_PAD + seq, LANES), BF16),
                        pltpu.VMEM((2, 2 * ATT_TQ, ATT_TK), F32),
                        pltpu.VMEM((2, 2 * ATT_TQ, ATT_TK), BF16)],
        compiler_params=pltpu.CompilerParams(
            dimension_semantics=("arbitrary", "arbitrary"),
            vmem_limit_bytes=VMEM_LIMIT),
        name="mixers",
    )(z, hf, z, z, lb_logits, hg_norm_w.reshape(1, HG_DV), jnp.asarray(wexp, BF16), jnp.asarray(masks),
      z, z, z, g, cast_a, cast_b)


def _merge_out_kernel(ya_ref, yb_ref, ga0_ref, ga1_ref, gb0_ref, gb1_ref, x_ref,
                      wa_ref, wb_ref, wo_ref, nw_ref, h_ref, u_ref, *, rc):
    for c in range(h_ref.shape[0] // rc):
        r = slice(c * rc, (c + 1) * rc)
        pa = _dot(ya_ref[r, :], wa_ref[...])
        pb = _dot(yb_ref[r, :], wb_ref[...])
        ga = jnp.concatenate([ga0_ref[r, :], ga1_ref[r, :]], axis=1).astype(F32)
        gb = jnp.concatenate([gb0_ref[r, :], gb1_ref[r, :]], axis=1).astype(F32)
        merged = (jax.nn.sigmoid(ga) * pa + jax.nn.sigmoid(gb) * pb).astype(BF16)
        h = x_ref[r, :] + _dot(merged, wo_ref[...])
        h_ref[r, :] = h
        u_ref[r, :] = _rms(h, nw_ref[...]).astype(BF16)


def _merge_out(ya, yb, z, x2, wa_bf, wb_bf, wo_bf, norm_w, *, tm=512, rc=256):
    m, d = x2.shape
    gw = d // 2
    gcol = (4 * HG_WIDTH + 3 * AT_WIDTH) // gw
    assert gcol * gw == 4 * HG_WIDTH + 3 * AT_WIDTH

    def resident(shape):
        return pl.BlockSpec(shape, lambda i: (0, 0), pipeline_mode=pl.Buffered(1))

    def gate(blk):
        return pl.BlockSpec((tm, gw), lambda i, blk=blk: (i, gcol + blk))

    return pl.pallas_call(
        functools.partial(_merge_out_kernel, rc=rc),
        out_shape=(jax.ShapeDtypeStruct((m, d), F32), jax.ShapeDtypeStruct((m, d), BF16)),
        grid=(m // tm,),
        in_specs=[
            pl.BlockSpec((tm, HG_WIDTH), lambda i: (i, 0)),
            pl.BlockSpec((tm, AT_WIDTH), lambda i: (i, 0)),
            gate(0), gate(1), gate(2), gate(3),
            pl.BlockSpec((tm, d), lambda i: (i, 0)),
            resident(wa_bf.shape), resident(wb_bf.shape), resident(wo_bf.shape),
            resident((1, d)),
        ],
        out_specs=(pl.BlockSpec((tm, d), lambda i: (i, 0)), pl.BlockSpec((tm, d), lambda i: (i, 0))),
        compiler_params=pltpu.CompilerParams(
            dimension_semantics=("parallel",),
            vmem_limit_bytes=VMEM_LIMIT),
        name="merge_out",
    )(ya, yb, z, z, z, z, x2, wa_bf, wb_bf, wo_bf, norm_w.reshape(1, d))


MLP_SLAB = 512


def _mlp_kernel(h_ref, u_ref, wu_ref, wd_ref, fw_ref, o_ref):
    f = pl.program_id(1)

    @pl.when(f == 0)
    def _():
        o_ref[...] = jnp.zeros_like(o_ref)

    a = jnp.maximum(_dot(u_ref[...], wu_ref[...]), 0.0)
    a = (a * a).astype(BF16)
    for c in range(o_ref.shape[1] // MLP_SLAB):
        cols = slice(c * MLP_SLAB, (c + 1) * MLP_SLAB)
        o_ref[:, cols] += _dot(a, wd_ref[:, cols])

    @pl.when(f == pl.num_programs(1) - 1)
    def _():
        o_ref[...] = _rms(h_ref[...] + o_ref[...], fw_ref[...])


def _mlp(h, u, wu_bf, wd_bf, final_w, *, tm=1024, tf=512):
    m, d = h.shape
    dff = wu_bf.shape[1]
    return pl.pallas_call(
        _mlp_kernel,
        out_shape=jax.ShapeDtypeStruct((m, d), F32),
        grid=(m // tm, dff // tf),
        in_specs=[
            pl.BlockSpec((tm, d), lambda i, f: (i, 0)),
            pl.BlockSpec((tm, d), lambda i, f: (i, 0)),
            pl.BlockSpec((d, tf), lambda i, f: (0, f)),
            pl.BlockSpec((tf, d), lambda i, f: (f, 0)),
            pl.BlockSpec((1, d), lambda i, f: (0, 0)),
        ],
        out_specs=pl.BlockSpec((tm, d), lambda i, f: (i, 0)),
        compiler_params=pltpu.CompilerParams(
            dimension_semantics=("parallel", "arbitrary"),
            vmem_limit_bytes=VMEM_LIMIT),
        name="mlp",
    )(h, u, wu_bf, wd_bf, final_w.reshape(1, d))


def kernel(x, w_in, lb_logits, hg_norm_w, rel_bias, w_branch_a, w_branch_b, w_out,
           norm_mix_w, norm_mlp_w, w_up, w_down, norm_final_w):
    batch, seq, d = x.shape
    assert d == D_MODEL and seq % ATT_TQ == 0 and w_in.shape[0] == 1
    x2 = x.reshape(batch * seq, d)
    z, hf, (wa_bf, wb_bf, wo_bf) = _in_proj(
        x2, norm_mix_w[0], w_in[0].astype(BF16), (w_branch_a[0], w_branch_b[0], w_out[0]))
    ya, yb, wu_bf, wd_bf = _mixers(z, hf, lb_logits, hg_norm_w[0], rel_bias[0], w_up[0],
                                   w_down[0], batch, seq)
    h, u = _merge_out(ya, yb, z, x2, wa_bf, wb_bf, wo_bf, norm_mlp_w[0])
    out = _mlp(h, u, wu_bf, wd_bf, norm_final_w)
    return out.reshape(batch, seq, d)
```

```python
import functools

import jax
import jax.numpy as jnp
import numpy as np
from jax import lax
from jax.experimental import pallas as pl
from jax.experimental.pallas import tpu as pltpu

D_MODEL = 2048
CHUNK = 64
HG_HEADS = 8
HG_DK = 128
HG_DV = 128
HG_WIDTH = HG_HEADS * HG_DV
AT_HEADS = 16
AT_DH = 64
AT_WIDTH = AT_HEADS * AT_DH
LEFT_CHUNKS = 8
REL_CLIP = 256
EPS = 1e-6
D_IN = 4 * HG_WIDTH + 3 * AT_WIDTH + 2 * D_MODEL

LANES = 128
N_LEVELS = 6
ATT_TQ = 2 * CHUNK

BF16 = jnp.bfloat16
F32 = jnp.float32
VMEM_LIMIT = 60000 * 1024

NN = (((1,), (0,)), ((), ()))
NT = (((1,), (1,)), ((), ()))
TN = (((0,), (0,)), ((), ()))


def _dot(a, b, dims=NN):
    return lax.dot_general(a, b, dims, preferred_element_type=F32)


def _rms(xf, w):
    return xf * lax.rsqrt(jnp.mean(xf * xf, axis=-1, keepdims=True) + EPS) * w


CAST_BLOCKS = 64
W_SLAB = 256


def _cast_to_slabs(src_ref, dst_ref):
    for s in range(dst_ref.shape[0]):
        dst_ref[s] = src_ref[:, s * W_SLAB:(s + 1) * W_SLAB].astype(BF16)


def _slab_shape(w):
    return (w.shape[1] // W_SLAB, w.shape[0], W_SLAB)


LOG2E = 1.4426950408889634
AT_QSCALE = AT_DH ** -0.5 * LOG2E


AQ_COLS = (4 * HG_WIDTH, 4 * HG_WIDTH + AT_WIDTH)
ZF_WIDTH = 2 * HG_WIDTH


def _in_proj_kernel(x_ref, nw_ref, w_ref, *rest, n_cast):
    cast_in, (z_ref, zf_ref) = rest[:n_cast], rest[n_cast:n_cast + 2]
    cast_out, u_ref = rest[n_cast + 2:2 * n_cast + 2], rest[2 * n_cast + 2]
    tn = z_ref.shape[1]

    @pl.when(pl.program_id(1) == 0)
    def _():
        u_ref[...] = _rms(x_ref[...], nw_ref[...]).astype(BF16)

    col0 = pl.program_id(1) * tn
    zscale = jnp.where((col0 >= AQ_COLS[0]) & (col0 < AQ_COLS[1]), AT_QSCALE, 1.0).astype(F32)
    for c in range(w_ref.shape[0]):
        cols = slice(c * W_SLAB, (c + 1) * W_SLAB)
        acc = _dot(u_ref[...], w_ref[c])
        z_ref[:, cols] = (acc * zscale).astype(z_ref.dtype)
        zf_ref[:, cols] = acc

    for src, dst in zip(cast_in, cast_out):
        _cast_to_slabs(src, dst)


def _in_proj(x2, norm_w, w_slabs, later_weights, *, tm=1024, tn=1024):
    m, k = x2.shape
    n = w_slabs.shape[0] * W_SLAB
    nj = n // tn
    assert (m // tm) * nj >= CAST_BLOCKS and tn == HG_WIDTH
    assert AQ_COLS[0] % tn == 0 and AQ_COLS[1] % tn == 0
    hf_tile = HG_WIDTH // tn

    def cast_spec(wl):
        rows, cols = wl.shape
        return pl.BlockSpec((rows // CAST_BLOCKS, cols),
                            lambda i, j: (jnp.minimum(i * nj + j, CAST_BLOCKS - 1), 0))

    def slab_spec(wl):
        rows, cols = wl.shape
        return pl.BlockSpec((cols // W_SLAB, rows // CAST_BLOCKS, W_SLAB),
                            lambda i, j: (0, jnp.minimum(i * nj + j, CAST_BLOCKS - 1), 0))

    outs = pl.pallas_call(
        functools.partial(_in_proj_kernel, n_cast=len(later_weights)),
        out_shape=(jax.ShapeDtypeStruct((m, n), BF16),
                   jax.ShapeDtypeStruct((m, ZF_WIDTH), F32),
                   *[jax.ShapeDtypeStruct(_slab_shape(wl), BF16) for wl in later_weights]),
        grid=(m // tm, nj),
        in_specs=[
            pl.BlockSpec((tm, k), lambda i, j: (i, 0)),
            pl.BlockSpec((1, k), lambda i, j: (0, 0)),
            pl.BlockSpec((tn // W_SLAB, k, W_SLAB), lambda i, j: (j, 0, 0)),
            *[cast_spec(wl) for wl in later_weights],
        ],
        out_specs=(
            pl.BlockSpec((tm, tn), lambda i, j: (i, j)),
            pl.BlockSpec((tm, tn), lambda i, j: (i, jnp.where(j <= hf_tile, 0, 1))),
            *[slab_spec(wl) for wl in later_weights],
        ),
        scratch_shapes=[pltpu.VMEM((tm, k), BF16)],
        compiler_params=pltpu.CompilerParams(
            dimension_semantics=("arbitrary", "arbitrary"),
            vmem_limit_bytes=VMEM_LIMIT),
        name="in_proj",
    )(x2, norm_w.reshape(1, k), w_slabs, *later_weights)
    return outs[0], outs[1], outs[2:]


MXU_LEVELS = 3


def _hgrn_tables():
    t = np.arange(CHUNK)[:, None]
    s = np.arange(CHUNK)[None, :]
    mats, masks = [], []
    for l in range(N_LEVELS):
        h = 1 << l
        start = (t // (2 * h)) * (2 * h)
        ref = start + h - 1
        is_q = (t - start) >= h
        if l < MXU_LEVELS:
            mats.append(np.where(is_q, (s > ref) & (s <= t), (s > t) & (s <= ref)))
        s_start = (s // (2 * h)) * (2 * h)
        masks.append((start == s_start) & is_q & ((s - s_start) < h))
    mats.append(s <= t)
    masks.append(t == s)
    w = np.concatenate(mats, axis=0).astype(np.float32)
    w2 = np.concatenate([w, w], axis=1)
    return w2, np.stack(masks).astype(np.float32)


def _level_exponent(b, h):
    parts = []
    for p in range(b.shape[0] // (2 * h)):
        lo = p * 2 * h
        ref = b[lo + h - 1:lo + h, :]
        parts += [ref - b[lo:lo + h, :], b[lo + h:lo + 2 * h, :] - ref]
    return jnp.concatenate(parts, axis=0)


N_OPS = N_LEVELS + 2
OP_DIAG, OP_STATE = N_LEVELS, N_LEVELS + 1


def _hgrn_units(hq_ref, hf_ref, hi_ref, hg_ref, lbl_ref, nw_ref, wexp_ref, mask_ref, y_ref,
                st_ref, qx_ref, kx_ref, d_ref, *, first_block, n_chunks):
    @pl.when(first_block)
    def _():
        st_ref[...] = jnp.zeros_like(st_ref)

    logits = lbl_ref[...]
    l0, l1 = logits[0:1, :], logits[1:2, :]
    mx = jnp.maximum(l0, l1)
    e0, e1 = jnp.exp(l0 - mx), jnp.exp(l1 - mx)
    lb = e0 / (e0 + e1)
    nw = nw_ref[...]
    wexp = wexp_ref[...]

    def stage(c):
        slot = c % 2
        r = slice(c * CHUNK, (c + 1) * CHUNK)
        f = jax.nn.sigmoid(hf_ref[r, :])
        g = lb + (1.0 - lb) * f
        lg = jnp.log2(g)
        kk = (1.0 - g).astype(BF16)
        q = (jax.nn.silu(hq_ref[r, :].astype(F32)) * (HG_DK ** -0.5)).astype(BF16)
        lg_hi = lg.astype(BF16)
        lg_lo = (lg - lg_hi.astype(F32)).astype(BF16)
        e_mxu = _dot(wexp, jnp.concatenate([lg_hi, lg_lo], axis=0))
        b = e_mxu[MXU_LEVELS * CHUNK:, :]
        def put(ref, op, val):
            for hh in range(HG_HEADS):
                ref[slot, op, hh] = val[:, hh * HG_DK:(hh + 1) * HG_DK]

        put(qx_ref, OP_DIAG, q)
        put(kx_ref, OP_DIAG, kk)
        for l in range(N_LEVELS):
            e = e_mxu[l * CHUNK:(l + 1) * CHUNK, :] if l < MXU_LEVELS else _level_exponent(b, 1 << l)
            x = jnp.exp2(e).astype(BF16)
            put(qx_ref, l, q * x)
            put(kx_ref, l, kk * x)
        eb = jnp.exp2(b)
        d_ref[slot] = eb[CHUNK - 1:CHUNK, :]
        put(qx_ref, OP_STATE, q * eb.astype(BF16))
        put(kx_ref, OP_STATE, kk * jnp.exp2(b[CHUNK - 1:CHUNK, :] - b).astype(BF16))

    def scores(c, h):
        slot = c % 2
        sc = mask_ref[OP_DIAG] * _dot(qx_ref[slot, OP_DIAG, h], kx_ref[slot, OP_DIAG, h], NT)
        for l in range(N_LEVELS):
            sc = sc + mask_ref[l] * _dot(qx_ref[slot, l, h], kx_ref[slot, l, h], NT)
        return sc.astype(BF16)

    def outputs(c, h, sc):
        slot, sl = c % 2, slice(h * HG_DK, (h + 1) * HG_DK)
        r = slice(c * CHUNK, (c + 1) * CHUNK)
        vh = hi_ref[r, sl]
        st = st_ref[h]
        o = _dot(sc, vh) + _dot(qx_ref[slot, OP_STATE, h], st.astype(BF16), NT)
        st_ref[h] = d_ref[slot, :, sl] * st + _dot(vh, kx_ref[slot, OP_STATE, h], TN)
        y = _rms(o, nw) * jax.nn.silu(hg_ref[r, sl].astype(F32))
        y_ref[r, sl] = y.astype(y_ref.dtype)

    stage(0)
    yield
    for c in range(n_chunks):
        sc_next = scores(c, 0)
        for h in range(HG_HEADS):
            sc = sc_next
            if h + 1 < HG_HEADS:
                sc_next = scores(c, h + 1)
            if h == 0 and c + 1 < n_chunks:
                stage(c + 1)
            outputs(c, h, sc)
            yield


ATT_TK = ATT_TQ + LEFT_CHUNKS * CHUNK
ATT_ROLL = 1024
assert ATT_TQ - 1 + ATT_TK <= ATT_ROLL and ATT_TK - ATT_TQ == 2 * REL_CLIP


def _rel_table(rel_bias):
    rev = rel_bias[:, ::-1].astype(F32) * LOG2E
    edge = jnp.broadcast_to(rev[:, :1], (rel_bias.shape[0], REL_CLIP))
    g = jnp.concatenate([edge, rev[:, :2 * REL_CLIP], edge], axis=1)
    return g.reshape(AT_HEADS // 2, 2, ATT_ROLL)


ATT_PAD = ATT_TK - ATT_TQ
ATT_EDGE = ATT_PAD // ATT_TQ


ATT_RG = 32


def _attn_units(q_ref, k_ref, v_ref, g_ref, o_ref, bias_ref, kp_ref, vp_ref, s_ref, p_ref,
                *, new_head_pair, seq):
    @pl.when(new_head_pair)
    def _():
        qc = lax.broadcasted_iota(jnp.int32, (ATT_TQ, ATT_TK), 0) // CHUNK
        kc = lax.broadcasted_iota(jnp.int32, (ATT_TQ, ATT_TK), 1) // CHUNK
        in_band = (kc >= qc) & (kc <= qc + LEFT_CHUNKS)
        g = g_ref[0]
        for hh in range(2):
            tbl = jnp.broadcast_to(g[hh:hh + 1, :], (ATT_TQ, ATT_ROLL))
            toep = pltpu.roll(tbl, 0, 1, stride=1, stride_axis=0)[:, :ATT_TK]
            rows = slice(hh * ATT_TQ, (hh + 1) * ATT_TQ)
            for n in range(ATT_EDGE + 1):
                first_chunk = (ATT_PAD - n * ATT_TQ) // CHUNK if n < ATT_EDGE else 0
                bias_ref[n, rows, :] = jnp.where(in_band & (kc >= first_chunk), toep, -jnp.inf)

    v_all = v_ref[...]
    head0_all = lax.broadcasted_iota(jnp.int32, v_all.shape, 1) < AT_DH
    ones = jnp.ones_like(v_all)
    pad = jnp.zeros((ATT_PAD, LANES), BF16)
    kp_ref[:ATT_PAD, :] = pad
    kp_ref[ATT_PAD:, :] = k_ref[...]
    vp_ref[0, :ATT_PAD, :] = pad
    vp_ref[1, :ATT_PAD, :] = pad
    vp_ref[0, ATT_PAD:, :] = jnp.where(head0_all, v_all, ones)
    vp_ref[1, ATT_PAD:, :] = jnp.where(head0_all, ones, v_all)

    head0 = lax.broadcasted_iota(jnp.int32, (ATT_TQ, LANES), 1) < AT_DH
    n_blocks = seq // ATT_TQ
    yield

    def scores(i):
        slot = i % 2
        q = q_ref[i * ATT_TQ:(i + 1) * ATT_TQ, :]
        zero = jnp.zeros_like(q)
        qq = jnp.concatenate([jnp.where(head0, q, zero), jnp.where(head0, zero, q)], axis=0)
        s = _dot(qq, kp_ref[i * ATT_TQ:i * ATT_TQ + ATT_TK, :], NT)
        s_ref[slot] = s + bias_ref[min(i, ATT_EDGE)]

    def outputs(i):
        slot = i % 2
        for g in range(2 * ATT_TQ // ATT_RG):
            rows = slice(g * ATT_RG, (g + 1) * ATT_RG)
            s = s_ref[slot, rows, :]
            p_ref[slot, rows, :] = jnp.exp2(s - jnp.max(s, axis=-1, keepdims=True)).astype(BF16)
        keys = slice(i * ATT_TQ, i * ATT_TQ + ATT_TK)
        acc0 = _dot(p_ref[slot, :ATT_TQ, :], vp_ref[0, keys, :])
        acc1 = _dot(p_ref[slot, ATT_TQ:, :], vp_ref[1, keys, :])
        num = jnp.where(head0, acc0, acc1)
        den = pltpu.roll(jnp.where(head0, acc1, acc0), AT_DH, 1)
        o_ref[i * ATT_TQ:(i + 1) * ATT_TQ, :] = (num / den).astype(o_ref.dtype)

    scores(0)
    for i in range(n_blocks):
        if i + 1 < n_blocks:
            scores(i + 1)
        outputs(i)
        yield


def _mixers_kernel(hq_ref, hf_ref, hi_ref, hg_ref, lbl_ref, nw_ref, wexp_ref, mask_ref,
                   q_ref, k_ref, v_ref, g_ref, cast_a_ref, cast_b_ref,
                   y_ref, o_ref, cast_a_out, cast_b_out,
                   st_ref, qx_ref, kx_ref, d_ref, bias_ref, kp_ref, vp_ref, s_ref, p_ref,
                   *, n_chunks, seq, steps_per_pair):
    for src, dst in ((cast_a_ref, cast_a_out), (cast_b_ref, cast_b_out)):
        _cast_to_slabs(src, dst)
    step = pl.program_id(0) * pl.num_programs(1) + pl.program_id(1)
    hgrn = _hgrn_units(hq_ref, hf_ref, hi_ref, hg_ref, lbl_ref, nw_ref, wexp_ref, mask_ref,
                       y_ref, st_ref, qx_ref, kx_ref, d_ref,
                       first_block=pl.program_id(1) == 0, n_chunks=n_chunks)
    attn = _attn_units(q_ref, k_ref, v_ref, g_ref, o_ref, bias_ref, kp_ref, vp_ref, s_ref, p_ref,
                       new_head_pair=step % steps_per_pair == 0, seq=seq)
    n_hgrn, n_attn = n_chunks * HG_HEADS, seq // ATT_TQ
    assert n_hgrn % n_attn == 0
    next(hgrn)
    next(attn)
    for _ in range(n_attn):
        next(attn)
        for _ in range(n_hgrn // n_attn):
            next(hgrn)
    assert next(attn, None) is None and next(hgrn, None) is None


def _mixers(z, hf, lb_logits, hg_norm_w, rel_bias, cast_a, cast_b, batch, seq, *, tb=256):
    wexp, masks = _hgrn_tables()
    g = _rel_table(rel_bias)
    nblk = seq // tb
    steps = batch * nblk
    assert steps == AT_HEADS // 2 * batch
    col0 = 4 * HG_WIDTH // LANES
    ncol = AT_WIDTH // LANES

    def step_of(b, i):
        return b * nblk + i

    def hspec(col):
        return pl.BlockSpec((tb, HG_WIDTH), lambda b, i, col=col: (b * nblk + i, col))

    def aspec(which):
        return pl.BlockSpec((seq, LANES), lambda b, i, which=which: (
            step_of(b, i) % batch, col0 + which * ncol + step_of(b, i) // batch))

    def const(shape):
        return pl.BlockSpec(shape, lambda b, i: (0,) * len(shape))

    def cast_spec(w):
        return pl.BlockSpec((w.shape[0] // steps, w.shape[1]), lambda b, i: (step_of(b, i), 0))

    def slab_spec(w):
        return pl.BlockSpec((w.shape[1] // W_SLAB, w.shape[0] // steps, W_SLAB),
                            lambda b, i: (0, step_of(b, i), 0))

    return pl.pallas_call(
        functools.partial(_mixers_kernel, n_chunks=tb // CHUNK, seq=seq, steps_per_pair=batch),
        out_shape=(jax.ShapeDtypeStruct((batch * seq, HG_WIDTH), BF16),
                   jax.ShapeDtypeStruct((batch * seq, AT_WIDTH), BF16),
                   jax.ShapeDtypeStruct(_slab_shape(cast_a), BF16),
                   jax.ShapeDtypeStruct(_slab_shape(cast_b), BF16)),
        grid=(batch, nblk),
        in_specs=[
            hspec(0), hspec(0), hspec(2), hspec(3),
            const((2, HG_WIDTH)), const((1, HG_DV)), const(wexp.shape), const(masks.shape),
            aspec(0), aspec(1), aspec(2),
            pl.BlockSpec((1, 2, ATT_ROLL), lambda b, i: (step_of(b, i) // batch, 0, 0)),
            cast_spec(cast_a), cast_spec(cast_b),
        ],
        out_specs=(
            pl.BlockSpec((tb, HG_WIDTH), lambda b, i: (b * nblk + i, 0)),
            pl.BlockSpec((seq, LANES), lambda b, i: (step_of(b, i) % batch, step_of(b, i) // batch)),
            slab_spec(cast_a), slab_spec(cast_b),
        ),
        scratch_shapes=[pltpu.VMEM((HG_HEADS, HG_DV, HG_DK), F32),
                        pltpu.VMEM((2, N_OPS, HG_HEADS, CHUNK, HG_DK), BF16),
                        pltpu.VMEM((2, N_OPS, HG_HEADS, CHUNK, HG_DK), BF16),
                        pltpu.VMEM((2, 1, HG_WIDTH), F32),
                        pltpu.VMEM((ATT_EDGE + 1, 2 * ATT_TQ, ATT_TK), F32),
                        pltpu.VMEM((ATT_PAD + seq, LANES), BF16),
                        pltpu.VMEM((2, ATT_PAD + seq, LANES), BF16),
                        pltpu.VMEM((2, 2 * ATT_TQ, ATT_TK), F32),
                        pltpu.VMEM((2, 2 * ATT_TQ, ATT_TK), BF16)],
        compiler_params=pltpu.CompilerParams(
            dimension_semantics=("arbitrary", "arbitrary"),
            vmem_limit_bytes=VMEM_LIMIT),
        name="mixers",
    )(z, hf, z, z, lb_logits, hg_norm_w.reshape(1, HG_DV), jnp.asarray(wexp, BF16), jnp.asarray(masks),
      z, z, z, g, cast_a, cast_b)


def _merge_out_kernel(ya_ref, yb_ref, ga0_ref, ga1_ref, gb0_ref, gb1_ref, x_ref,
                      wa_ref, wb_ref, wo_ref, nw_ref, h_ref, u_ref, *, rc):
    n_slabs = wo_ref.shape[0]
    for c in range(h_ref.shape[0] // rc):
        r = slice(c * rc, (c + 1) * rc)
        ya, yb = ya_ref[r, :], yb_ref[r, :]
        merged = []
        for s in range(n_slabs):
            ga_ref, gb_ref = (ga0_ref, gb0_ref) if s < n_slabs // 2 else (ga1_ref, gb1_ref)
            gcols = slice((s % (n_slabs // 2)) * W_SLAB, (s % (n_slabs // 2) + 1) * W_SLAB)
            ga = jax.nn.sigmoid(ga_ref[r, gcols].astype(F32))
            gb = jax.nn.sigmoid(gb_ref[r, gcols].astype(F32))
            merged.append((ga * _dot(ya, wa_ref[s]) + gb * _dot(yb, wb_ref[s])).astype(BF16))
        merged = jnp.concatenate(merged, axis=1)
        h = x_ref[r, :] + jnp.concatenate([_dot(merged, wo_ref[s]) for s in range(n_slabs)], axis=1)
        h_ref[r, :] = h
        u_ref[r, :] = _rms(h, nw_ref[...]).astype(BF16)


def _merge_out(ya, yb, z, x2, wa_bf, wb_bf, wo_bf, norm_w, *, tm=512, rc=256):
    m, d = x2.shape
    gw = d // 2
    gcol = (4 * HG_WIDTH + 3 * AT_WIDTH) // gw
    assert gcol * gw == 4 * HG_WIDTH + 3 * AT_WIDTH

    def resident(shape):
        return pl.BlockSpec(shape, lambda i: (0,) * len(shape), pipeline_mode=pl.Buffered(1))

    def gate(blk):
        return pl.BlockSpec((tm, gw), lambda i, blk=blk: (i, gcol + blk))

    return pl.pallas_call(
        functools.partial(_merge_out_kernel, rc=rc),
        out_shape=(jax.ShapeDtypeStruct((m, d), F32), jax.ShapeDtypeStruct((m, d), BF16)),
        grid=(m // tm,),
        in_specs=[
            pl.BlockSpec((tm, HG_WIDTH), lambda i: (i, 0)),
            pl.BlockSpec((tm, AT_WIDTH), lambda i: (i, 0)),
            gate(0), gate(1), gate(2), gate(3),
            pl.BlockSpec((tm, d), lambda i: (i, 0)),
            resident(wa_bf.shape), resident(wb_bf.shape), resident(wo_bf.shape),
            resident((1, d)),
        ],
        out_specs=(pl.BlockSpec((tm, d), lambda i: (i, 0)), pl.BlockSpec((tm, d), lambda i: (i, 0))),
        compiler_params=pltpu.CompilerParams(
            dimension_semantics=("parallel",),
            vmem_limit_bytes=VMEM_LIMIT),
        name="merge_out",
    )(ya, yb, z, z, z, z, x2, wa_bf, wb_bf, wo_bf, norm_w.reshape(1, d))


def _mlp_kernel(h_ref, u_ref, wu_ref, wd_ref, fw_ref, o_ref):
    f = pl.program_id(1)

    @pl.when(f == 0)
    def _():
        o_ref[...] = jnp.zeros_like(o_ref)

    a = jnp.concatenate([jnp.maximum(_dot(u_ref[...], wu_ref[s]), 0.0) for s in range(wu_ref.shape[0])],
                        axis=1)
    a = (a * a).astype(BF16)
    for c in range(wd_ref.shape[0]):
        o_ref[:, c * W_SLAB:(c + 1) * W_SLAB] += _dot(a, wd_ref[c])

    @pl.when(f == pl.num_programs(1) - 1)
    def _():
        o_ref[...] = _rms(h_ref[...] + o_ref[...], fw_ref[...])


def _mlp(h, u, wu_slabs, wd_slabs, final_w, *, tm=1024, tf=512):
    m, d = h.shape
    dff = wd_slabs.shape[1]
    return pl.pallas_call(
        _mlp_kernel,
        out_shape=jax.ShapeDtypeStruct((m, d), F32),
        grid=(m // tm, dff // tf),
        in_specs=[
            pl.BlockSpec((tm, d), lambda i, f: (i, 0)),
            pl.BlockSpec((tm, d), lambda i, f: (i, 0)),
            pl.BlockSpec((tf // W_SLAB, d, W_SLAB), lambda i, f: (f, 0, 0)),
            pl.BlockSpec((d // W_SLAB, tf, W_SLAB), lambda i, f: (0, f, 0)),
            pl.BlockSpec((1, d), lambda i, f: (0, 0)),
        ],
        out_specs=pl.BlockSpec((tm, d), lambda i, f: (i, 0)),
        compiler_params=pltpu.CompilerParams(
            dimension_semantics=("parallel", "arbitrary"),
            vmem_limit_bytes=VMEM_LIMIT),
        name="mlp",
    )(h, u, wu_slabs, wd_slabs, final_w.reshape(1, d))


def kernel(x, w_in, lb_logits, hg_norm_w, rel_bias, w_branch_a, w_branch_b, w_out,
           norm_mix_w, norm_mlp_w, w_up, w_down, norm_final_w):
    batch, seq, d = x.shape
    assert d == D_MODEL and seq % ATT_TQ == 0 and w_in.shape[0] == 1
    x2 = x.reshape(batch * seq, d)
    w_in_slabs = w_in[0].astype(BF16).reshape(d, -1, W_SLAB).transpose(1, 0, 2)
    z, hf, (wa_bf, wb_bf, wo_bf) = _in_proj(
        x2, norm_mix_w[0], w_in_slabs, (w_branch_a[0], w_branch_b[0], w_out[0]))
    ya, yb, wu_bf, wd_bf = _mixers(z, hf, lb_logits, hg_norm_w[0], rel_bias[0], w_up[0],
                                   w_down[0], batch, seq)
    h, u = _merge_out(ya, yb, z, x2, wa_bf, wb_bf, wo_bf, norm_mlp_w[0])
    out = _mlp(h, u, wu_bf, wd_bf, norm_final_w)
    return out.reshape(batch, seq, d)
```

```python
import functools

import jax
import jax.numpy as jnp
import numpy as np
from jax import lax
from jax.experimental import pallas as pl
from jax.experimental.pallas import tpu as pltpu

D_MODEL = 2048
CHUNK = 64
HG_HEADS = 8
HG_DK = 128
HG_DV = 128
HG_WIDTH = HG_HEADS * HG_DV
AT_HEADS = 16
AT_DH = 64
AT_WIDTH = AT_HEADS * AT_DH
LEFT_CHUNKS = 8
REL_CLIP = 256
EPS = 1e-6

LANES = 128
MXU_TILE = 256
N_LEVELS = 6
ATT_TQ = 2 * CHUNK

BF16 = jnp.bfloat16
F32 = jnp.float32
VMEM_LIMIT = 60000 * 1024

IN_TM, IN_TN = 1024, 1024
MIX_TB = 4 * CHUNK
MERGE_TM, MERGE_RC = 512, 256
MLP_TM, MLP_TF = 1024, 512

NN = (((1,), (0,)), ((), ()))
NT = (((1,), (1,)), ((), ()))
TN = (((0,), (0,)), ((), ()))


def _dot(a, b, dims=NN):
    return lax.dot_general(a, b, dims, preferred_element_type=F32)


def _rms(xf, w):
    return xf * lax.rsqrt(jnp.mean(xf * xf, axis=-1, keepdims=True) + EPS) * w


CAST_BLOCKS = 64
IN_SLAB = MXU_TILE
LOG2E = 1.4426950408889634
AT_QSCALE = AT_DH ** -0.5 * LOG2E
AQ_COLS = (4 * HG_WIDTH, 4 * HG_WIDTH + AT_WIDTH)
ZF_WIDTH = 2 * HG_WIDTH


def _in_proj_kernel(x_ref, nw_ref, w_ref, *rest, n_cast):
    cast_in, (z_ref, zf_ref) = rest[:n_cast], rest[n_cast:n_cast + 2]
    cast_out, u_ref = rest[n_cast + 2:2 * n_cast + 2], rest[2 * n_cast + 2]
    tn = z_ref.shape[1]

    @pl.when(pl.program_id(1) == 0)
    def _():
        u_ref[...] = _rms(x_ref[...], nw_ref[...]).astype(BF16)

    col0 = pl.program_id(1) * tn
    zscale = jnp.where((col0 >= AQ_COLS[0]) & (col0 < AQ_COLS[1]), AT_QSCALE, 1.0).astype(F32)
    for c in range(tn // IN_SLAB):
        cols = slice(c * IN_SLAB, (c + 1) * IN_SLAB)
        acc = _dot(u_ref[...], w_ref[:, cols])
        z_ref[:, cols] = (acc * zscale).astype(z_ref.dtype)
        zf_ref[:, cols] = acc

    for src, dst in zip(cast_in, cast_out):
        dst[...] = src[...].astype(BF16)


def _in_proj(x2, norm_w, w_bf, later_weights, *, tm=IN_TM, tn=IN_TN):
    m, k = x2.shape
    n = w_bf.shape[1]
    nj = n // tn
    assert (m // tm) * nj >= CAST_BLOCKS and tn == HG_WIDTH
    assert AQ_COLS[0] % tn == 0 and AQ_COLS[1] % tn == 0
    hf_tile = HG_WIDTH // tn

    def cast_spec(wl):
        rows, cols = wl.shape
        return pl.BlockSpec((rows // CAST_BLOCKS, cols),
                            lambda i, j: (jnp.minimum(i * nj + j, CAST_BLOCKS - 1), 0))

    cast_specs = [cast_spec(wl) for wl in later_weights]
    outs = pl.pallas_call(
        functools.partial(_in_proj_kernel, n_cast=len(later_weights)),
        out_shape=(jax.ShapeDtypeStruct((m, n), BF16),
                   jax.ShapeDtypeStruct((m, ZF_WIDTH), F32),
                   *[jax.ShapeDtypeStruct(wl.shape, BF16) for wl in later_weights]),
        grid=(m // tm, nj),
        in_specs=[
            pl.BlockSpec((tm, k), lambda i, j: (i, 0)),
            pl.BlockSpec((1, k), lambda i, j: (0, 0)),
            pl.BlockSpec((k, tn), lambda i, j: (0, j)),
            *cast_specs,
        ],
        out_specs=(
            pl.BlockSpec((tm, tn), lambda i, j: (i, j)),
            pl.BlockSpec((tm, tn), lambda i, j: (i, jnp.where(j <= hf_tile, 0, 1))),
            *cast_specs,
        ),
        scratch_shapes=[pltpu.VMEM((tm, k), BF16)],
        compiler_params=pltpu.CompilerParams(
            dimension_semantics=("arbitrary", "arbitrary"),
            vmem_limit_bytes=VMEM_LIMIT),
        name="in_proj",
    )(x2, norm_w.reshape(1, k), w_bf, *later_weights)
    return outs[0], outs[1], outs[2:]


MXU_LEVELS = 3


def _hgrn_tables():
    t = np.arange(CHUNK)[:, None]
    s = np.arange(CHUNK)[None, :]
    mats, masks = [], []
    for l in range(N_LEVELS):
        h = 1 << l
        start = (t // (2 * h)) * (2 * h)
        ref = start + h - 1
        is_q = (t - start) >= h
        if l < MXU_LEVELS:
            mats.append(np.where(is_q, (s > ref) & (s <= t), (s > t) & (s <= ref)))
        s_start = (s // (2 * h)) * (2 * h)
        masks.append((start == s_start) & is_q & ((s - s_start) < h))
    mats.append(s <= t)
    masks.append(t == s)
    w = np.concatenate(mats, axis=0).astype(np.float32)
    w2 = np.concatenate([w, w], axis=1)
    return w2, np.stack(masks).astype(np.float32)


def _level_exponent(b, h):
    parts = []
    for p in range(b.shape[0] // (2 * h)):
        lo = p * 2 * h
        ref = b[lo + h - 1:lo + h, :]
        parts += [ref - b[lo:lo + h, :], b[lo + h:lo + 2 * h, :] - ref]
    return jnp.concatenate(parts, axis=0)


N_OPS = N_LEVELS + 2
OP_DIAG, OP_STATE = N_LEVELS, N_LEVELS + 1


def _hgrn_units(hq_ref, hf_ref, hi_ref, hg_ref, lbl_ref, nw_ref, wexp_ref, mask_ref, y_ref,
                st_ref, qx_ref, kx_ref, d_ref, *, first_block, n_chunks):
    @pl.when(first_block)
    def _():
        st_ref[...] = jnp.zeros_like(st_ref)

    logits = lbl_ref[...]
    l0, l1 = logits[0:1, :], logits[1:2, :]
    mx = jnp.maximum(l0, l1)
    e0, e1 = jnp.exp(l0 - mx), jnp.exp(l1 - mx)
    lb = e0 / (e0 + e1)
    nw = nw_ref[...]
    wexp = wexp_ref[...]

    def stage(c):
        slot = c % 2
        r = slice(c * CHUNK, (c + 1) * CHUNK)
        f = jax.nn.sigmoid(hf_ref[r, :])
        g = lb + (1.0 - lb) * f
        lg = jnp.log2(g)
        kk = (1.0 - g).astype(BF16)
        q = (jax.nn.silu(hq_ref[r, :].astype(F32)) * (HG_DK ** -0.5)).astype(BF16)
        lg_hi = lg.astype(BF16)
        lg_lo = (lg - lg_hi.astype(F32)).astype(BF16)
        e_mxu = _dot(wexp, jnp.concatenate([lg_hi, lg_lo], axis=0))
        b = e_mxu[MXU_LEVELS * CHUNK:, :]

        def put(ref, op, val):
            for hh in range(HG_HEADS):
                ref[slot, op, hh] = val[:, hh * HG_DK:(hh + 1) * HG_DK]

        put(qx_ref, OP_DIAG, q)
        put(kx_ref, OP_DIAG, kk)
        for l in range(N_LEVELS):
            e = e_mxu[l * CHUNK:(l + 1) * CHUNK, :] if l < MXU_LEVELS else _level_exponent(b, 1 << l)
            x = jnp.exp2(e).astype(BF16)
            put(qx_ref, l, q * x)
            put(kx_ref, l, kk * x)
        eb = jnp.exp2(b)
        d_ref[slot] = eb[CHUNK - 1:CHUNK, :]
        put(qx_ref, OP_STATE, q * eb.astype(BF16))
        put(kx_ref, OP_STATE, kk * jnp.exp2(b[CHUNK - 1:CHUNK, :] - b).astype(BF16))

    def scores(c, h):
        slot = c % 2
        sc = mask_ref[OP_DIAG] * _dot(qx_ref[slot, OP_DIAG, h], kx_ref[slot, OP_DIAG, h], NT)
        for l in range(N_LEVELS):
            sc = sc + mask_ref[l] * _dot(qx_ref[slot, l, h], kx_ref[slot, l, h], NT)
        return sc.astype(BF16)

    def outputs(c, h, sc):
        slot, sl = c % 2, slice(h * HG_DK, (h + 1) * HG_DK)
        r = slice(c * CHUNK, (c + 1) * CHUNK)
        vh = hi_ref[r, sl]
        st = st_ref[h]
        o = _dot(sc, vh) + _dot(qx_ref[slot, OP_STATE, h], st.astype(BF16), NT)
        st_ref[h] = d_ref[slot, :, sl] * st + _dot(vh, kx_ref[slot, OP_STATE, h], TN)
        y = _rms(o, nw) * jax.nn.silu(hg_ref[r, sl].astype(F32))
        y_ref[r, sl] = y.astype(y_ref.dtype)

    stage(0)
    yield
    for c in range(n_chunks):
        sc_next = scores(c, 0)
        for h in range(HG_HEADS):
            sc = sc_next
            if h + 1 < HG_HEADS:
                sc_next = scores(c, h + 1)
            if h == 0 and c + 1 < n_chunks:
                stage(c + 1)
            outputs(c, h, sc)
            yield


ATT_TK = ATT_TQ + LEFT_CHUNKS * CHUNK
ATT_ROLL = 1024
assert ATT_TQ - 1 + ATT_TK <= ATT_ROLL and ATT_TK - ATT_TQ == 2 * REL_CLIP


def _rel_table(rel_bias):
    rev = rel_bias[:, ::-1].astype(F32) * LOG2E
    edge = jnp.broadcast_to(rev[:, :1], (rel_bias.shape[0], REL_CLIP))
    g = jnp.concatenate([edge, rev[:, :2 * REL_CLIP], edge], axis=1)
    return g.reshape(AT_HEADS // 2, 2, ATT_ROLL)


ATT_PAD = ATT_TK - ATT_TQ
ATT_EDGE = ATT_PAD // ATT_TQ
ATT_RG = 32


def _attn_units(q_ref, k_ref, v_ref, g_ref, o_ref, bias_ref, kp_ref, vp_ref, s_ref, p_ref,
                *, new_head_pair, seq):
    @pl.when(new_head_pair)
    def _():
        qc = lax.broadcasted_iota(jnp.int32, (ATT_TQ, ATT_TK), 0) // CHUNK
        kc = lax.broadcasted_iota(jnp.int32, (ATT_TQ, ATT_TK), 1) // CHUNK
        in_band = (kc >= qc) & (kc <= qc + LEFT_CHUNKS)
        g = g_ref[0]
        for hh in range(2):
            tbl = jnp.broadcast_to(g[hh:hh + 1, :], (ATT_TQ, ATT_ROLL))
            toep = pltpu.roll(tbl, 0, 1, stride=1, stride_axis=0)[:, :ATT_TK]
            rows = slice(hh * ATT_TQ, (hh + 1) * ATT_TQ)
            for n in range(ATT_EDGE + 1):
                first_chunk = (ATT_PAD - n * ATT_TQ) // CHUNK if n < ATT_EDGE else 0
                bias_ref[n, rows, :] = jnp.where(in_band & (kc >= first_chunk), toep, -jnp.inf)

    v_all = v_ref[...]
    head0_all = lax.broadcasted_iota(jnp.int32, v_all.shape, 1) < AT_DH
    ones = jnp.ones_like(v_all)
    pad = jnp.zeros((ATT_PAD, LANES), BF16)
    kp_ref[:ATT_PAD, :] = pad
    kp_ref[ATT_PAD:, :] = k_ref[...]
    vp_ref[0, :ATT_PAD, :] = pad
    vp_ref[1, :ATT_PAD, :] = pad
    vp_ref[0, ATT_PAD:, :] = jnp.where(head0_all, v_all, ones)
    vp_ref[1, ATT_PAD:, :] = jnp.where(head0_all, ones, v_all)

    head0 = lax.broadcasted_iota(jnp.int32, (ATT_TQ, LANES), 1) < AT_DH
    n_blocks = seq // ATT_TQ
    yield

    def scores(i):
        slot = i % 2
        q = q_ref[i * ATT_TQ:(i + 1) * ATT_TQ, :]
        zero = jnp.zeros_like(q)
        qq = jnp.concatenate([jnp.where(head0, q, zero), jnp.where(head0, zero, q)], axis=0)
        s = _dot(qq, kp_ref[i * ATT_TQ:i * ATT_TQ + ATT_TK, :], NT)
        s_ref[slot] = s + bias_ref[min(i, ATT_EDGE)]

    def outputs(i):
        slot = i % 2
        for g in range(2 * ATT_TQ // ATT_RG):
            rows = slice(g * ATT_RG, (g + 1) * ATT_RG)
            s = s_ref[slot, rows, :]
            p_ref[slot, rows, :] = jnp.exp2(s - jnp.max(s, axis=-1, keepdims=True)).astype(BF16)
        keys = slice(i * ATT_TQ, i * ATT_TQ + ATT_TK)
        acc0 = _dot(p_ref[slot, :ATT_TQ, :], vp_ref[0, keys, :])
        acc1 = _dot(p_ref[slot, ATT_TQ:, :], vp_ref[1, keys, :])
        num = jnp.where(head0, acc0, acc1)
        den = pltpu.roll(jnp.where(head0, acc1, acc0), AT_DH, 1)
        o_ref[i * ATT_TQ:(i + 1) * ATT_TQ, :] = (num / den).astype(o_ref.dtype)

    scores(0)
    for i in range(n_blocks):
        if i + 1 < n_blocks:
            scores(i + 1)
        outputs(i)
        yield


def _mixers_kernel(hq_ref, hf_ref, hi_ref, hg_ref, lbl_ref, nw_ref, wexp_ref, mask_ref,
                   q_ref, k_ref, v_ref, g_ref, cast_a_ref, cast_b_ref,
                   y_ref, o_ref, cast_a_out, cast_b_out,
                   st_ref, qx_ref, kx_ref, d_ref, bias_ref, kp_ref, vp_ref, s_ref, p_ref,
                   *, n_chunks, seq, steps_per_pair):
    cast_a_out[...] = cast_a_ref[...].astype(BF16)
    cast_b_out[...] = cast_b_ref[...].astype(BF16)
    step = pl.program_id(0) * pl.num_programs(1) + pl.program_id(1)
    hgrn = _hgrn_units(hq_ref, hf_ref, hi_ref, hg_ref, lbl_ref, nw_ref, wexp_ref, mask_ref,
                       y_ref, st_ref, qx_ref, kx_ref, d_ref,
                       first_block=pl.program_id(1) == 0, n_chunks=n_chunks)
    attn = _attn_units(q_ref, k_ref, v_ref, g_ref, o_ref, bias_ref, kp_ref, vp_ref, s_ref, p_ref,
                       new_head_pair=step % steps_per_pair == 0, seq=seq)
    n_hgrn, n_attn = n_chunks * HG_HEADS, seq // ATT_TQ
    assert n_hgrn % n_attn == 0
    next(hgrn)
    next(attn)
    for _ in range(n_attn):
        next(attn)
        for _ in range(n_hgrn // n_attn):
            next(hgrn)
    assert next(attn, None) is None and next(hgrn, None) is None


def _mixers(z, hf, lb_logits, hg_norm_w, rel_bias, cast_a, cast_b, batch, seq, *, tb=MIX_TB):
    wexp, masks = _hgrn_tables()
    g = _rel_table(rel_bias)
    nblk = seq // tb
    steps = batch * nblk
    assert steps == AT_HEADS // 2 * batch
    col0 = 4 * HG_WIDTH // LANES
    ncol = AT_WIDTH // LANES

    def step_of(b, i):
        return b * nblk + i

    def hspec(col):
        return pl.BlockSpec((tb, HG_WIDTH), lambda b, i, col=col: (b * nblk + i, col))

    def aspec(which):
        return pl.BlockSpec((seq, LANES), lambda b, i, which=which: (
            step_of(b, i) % batch, col0 + which * ncol + step_of(b, i) // batch))

    def const(shape):
        return pl.BlockSpec(shape, lambda b, i: (0,) * len(shape))

    def cast_spec(w):
        return pl.BlockSpec((w.shape[0] // steps, w.shape[1]), lambda b, i: (step_of(b, i), 0))

    return pl.pallas_call(
        functools.partial(_mixers_kernel, n_chunks=tb // CHUNK, seq=seq, steps_per_pair=batch),
        out_shape=(jax.ShapeDtypeStruct((batch * seq, HG_WIDTH), BF16),
                   jax.ShapeDtypeStruct((batch * seq, AT_WIDTH), BF16),
                   jax.ShapeDtypeStruct(cast_a.shape, BF16),
                   jax.ShapeDtypeStruct(cast_b.shape, BF16)),
        grid=(batch, nblk),
        in_specs=[
            hspec(0), hspec(0), hspec(2), hspec(3),
            const((2, HG_WIDTH)), const((1, HG_DV)), const(wexp.shape), const(masks.shape),
            aspec(0), aspec(1), aspec(2),
            pl.BlockSpec((1, 2, ATT_ROLL), lambda b, i: (step_of(b, i) // batch, 0, 0)),
            cast_spec(cast_a), cast_spec(cast_b),
        ],
        out_specs=(
            pl.BlockSpec((tb, HG_WIDTH), lambda b, i: (b * nblk + i, 0)),
            pl.BlockSpec((seq, LANES), lambda b, i: (step_of(b, i) % batch, step_of(b, i) // batch)),
            cast_spec(cast_a), cast_spec(cast_b),
        ),
        scratch_shapes=[pltpu.VMEM((HG_HEADS, HG_DV, HG_DK), F32),
                        pltpu.VMEM((2, N_OPS, HG_HEADS, CHUNK, HG_DK), BF16),
                        pltpu.VMEM((2, N_OPS, HG_HEADS, CHUNK, HG_DK), BF16),
                        pltpu.VMEM((2, 1, HG_WIDTH), F32),
                        pltpu.VMEM((ATT_EDGE + 1, 2 * ATT_TQ, ATT_TK), F32),
                        pltpu.VMEM((ATT_PAD + seq, LANES), BF16),
                        pltpu.VMEM((2, ATT_PAD + seq, LANES), BF16),
                        pltpu.VMEM((2, 2 * ATT_TQ, ATT_TK), F32),
                        pltpu.VMEM((2, 2 * ATT_TQ, ATT_TK), BF16)],
        compiler_params=pltpu.CompilerParams(
            dimension_semantics=("arbitrary", "arbitrary"),
            vmem_limit_bytes=VMEM_LIMIT),
        name="mixers",
    )(z, hf, z, z, lb_logits, hg_norm_w.reshape(1, HG_DV), jnp.asarray(wexp, BF16), jnp.asarray(masks),
      z, z, z, g, cast_a, cast_b)


def _merge_out_kernel(ya_ref, yb_ref, ga0_ref, ga1_ref, gb0_ref, gb1_ref, x_ref,
                      wa_ref, wb_ref, wo_ref, nw_ref, h_ref, u_ref, *, rc):
    for c in range(h_ref.shape[0] // rc):
        r = slice(c * rc, (c + 1) * rc)
        pa = _dot(ya_ref[r, :], wa_ref[...])
        pb = _dot(yb_ref[r, :], wb_ref[...])
        ga = jnp.concatenate([ga0_ref[r, :], ga1_ref[r, :]], axis=1).astype(F32)
        gb = jnp.concatenate([gb0_ref[r, :], gb1_ref[r, :]], axis=1).astype(F32)
        merged = (jax.nn.sigmoid(ga) * pa + jax.nn.sigmoid(gb) * pb).astype(BF16)
        h = x_ref[r, :] + _dot(merged, wo_ref[...])
        h_ref[r, :] = h
        u_ref[r, :] = _rms(h, nw_ref[...]).astype(BF16)


def _merge_out(ya, yb, z, x2, wa_bf, wb_bf, wo_bf, norm_w, *, tm=MERGE_TM, rc=MERGE_RC):
    m, d = x2.shape
    gw = d // 2
    gcol = (4 * HG_WIDTH + 3 * AT_WIDTH) // gw
    assert gcol * gw == 4 * HG_WIDTH + 3 * AT_WIDTH

    def resident(shape):
        return pl.BlockSpec(shape, lambda i: (0, 0), pipeline_mode=pl.Buffered(1))

    def gate(blk):
        return pl.BlockSpec((tm, gw), lambda i, blk=blk: (i, gcol + blk))

    return pl.pallas_call(
        functools.partial(_merge_out_kernel, rc=rc),
        out_shape=(jax.ShapeDtypeStruct((m, d), F32), jax.ShapeDtypeStruct((m, d), BF16)),
        grid=(m // tm,),
        in_specs=[
            pl.BlockSpec((tm, HG_WIDTH), lambda i: (i, 0)),
            pl.BlockSpec((tm, AT_WIDTH), lambda i: (i, 0)),
            gate(0), gate(1), gate(2), gate(3),
            pl.BlockSpec((tm, d), lambda i: (i, 0)),
            resident(wa_bf.shape), resident(wb_bf.shape), resident(wo_bf.shape),
            resident((1, d)),
        ],
        out_specs=(pl.BlockSpec((tm, d), lambda i: (i, 0)), pl.BlockSpec((tm, d), lambda i: (i, 0))),
        compiler_params=pltpu.CompilerParams(
            dimension_semantics=("parallel",),
            vmem_limit_bytes=VMEM_LIMIT),
        name="merge_out",
    )(ya, yb, z, z, z, z, x2, wa_bf, wb_bf, wo_bf, norm_w.reshape(1, d))


MLP_SLAB = 2 * MXU_TILE


def _mlp_kernel(h_ref, u_ref, wu_ref, wd_ref, fw_ref, o_ref):
    f = pl.program_id(1)

    @pl.when(f == 0)
    def _():
        o_ref[...] = jnp.zeros_like(o_ref)

    a = jnp.maximum(_dot(u_ref[...], wu_ref[...]), 0.0)
    a = (a * a).astype(BF16)
    for c in range(o_ref.shape[1] // MLP_SLAB):
        cols = slice(c * MLP_SLAB, (c + 1) * MLP_SLAB)
        o_ref[:, cols] += _dot(a, wd_ref[:, cols])

    @pl.when(f == pl.num_programs(1) - 1)
    def _():
        o_ref[...] = _rms(h_ref[...] + o_ref[...], fw_ref[...])


def _mlp(h, u, wu_bf, wd_bf, final_w, *, tm=MLP_TM, tf=MLP_TF):
    m, d = h.shape
    dff = wu_bf.shape[1]
    return pl.pallas_call(
        _mlp_kernel,
        out_shape=jax.ShapeDtypeStruct((m, d), F32),
        grid=(m // tm, dff // tf),
        in_specs=[
            pl.BlockSpec((tm, d), lambda i, f: (i, 0)),
            pl.BlockSpec((tm, d), lambda i, f: (i, 0)),
            pl.BlockSpec((d, tf), lambda i, f: (0, f)),
            pl.BlockSpec((tf, d), lambda i, f: (f, 0)),
            pl.BlockSpec((1, d), lambda i, f: (0, 0)),
        ],
        out_specs=pl.BlockSpec((tm, d), lambda i, f: (i, 0)),
        compiler_params=pltpu.CompilerParams(
            dimension_semantics=("parallel", "arbitrary"),
            vmem_limit_bytes=VMEM_LIMIT),
        name="mlp",
    )(h, u, wu_bf, wd_bf, final_w.reshape(1, d))


def kernel(x, w_in, lb_logits, hg_norm_w, rel_bias, w_branch_a, w_branch_b, w_out,
           norm_mix_w, norm_mlp_w, w_up, w_down, norm_final_w):
    batch, seq, d = x.shape
    assert d == D_MODEL and seq % ATT_TQ == 0 and w_in.shape[0] == 1
    x2 = x.reshape(batch * seq, d)
    z, hf, (wa_bf, wb_bf, wo_bf) = _in_proj(
        x2, norm_mix_w[0], w_in[0].astype(BF16), (w_branch_a[0], w_branch_b[0], w_out[0]))
    ya, yb, wu_bf, wd_bf = _mixers(z, hf, lb_logits, hg_norm_w[0], rel_bias[0], w_up[0],
                                   w_down[0], batch, seq)
    h, u = _merge_out(ya, yb, z, x2, wa_bf, wb_bf, wo_bf, norm_mlp_w[0])
    out = _mlp(h, u, wu_bf, wd_bf, norm_final_w)
    return out.reshape(batch, seq, d)
```

```python
import functools

import jax
import jax.numpy as jnp
import numpy as np
from jax import lax
from jax.experimental import pallas as pl
from jax.experimental.pallas import tpu as pltpu

D_MODEL = 2048
CHUNK = 64
HG_HEADS = 8
HG_DK = 128
HG_DV = 128
HG_WIDTH = HG_HEADS * HG_DV
AT_HEADS = 16
AT_DH = 64
AT_WIDTH = AT_HEADS * AT_DH
LEFT_CHUNKS = 8
REL_CLIP = 256
EPS = 1e-6

LANES = 128
MXU_TILE = 256
N_LEVELS = 6
ATT_TQ = 2 * CHUNK

BF16 = jnp.bfloat16
F32 = jnp.float32
VMEM_LIMIT = 60000 * 1024

IN_TM, IN_TN = 2048, 512
MIX_TB = 4 * CHUNK
MERGE_TM, MERGE_RC = 512, 256
MLP_TM, MLP_TF = 1024, 512

NN = (((1,), (0,)), ((), ()))
NT = (((1,), (1,)), ((), ()))
TN = (((0,), (0,)), ((), ()))


def _dot(a, b, dims=NN):
    return lax.dot_general(a, b, dims, preferred_element_type=F32)


def _rms(xf, w):
    return xf * lax.rsqrt(jnp.mean(xf * xf, axis=-1, keepdims=True) + EPS) * w


CAST_BLOCKS = 64
IN_SLAB = MXU_TILE
LOG2E = 1.4426950408889634
AT_QSCALE = AT_DH ** -0.5 * LOG2E
AQ_COLS = (4 * HG_WIDTH, 4 * HG_WIDTH + AT_WIDTH)
ZF_WIDTH = HG_WIDTH + IN_TN


def _in_proj_kernel(x_ref, nw_ref, w_ref, *rest, n_cast):
    cast_in, (z_ref, zf_ref) = rest[:n_cast], rest[n_cast:n_cast + 2]
    cast_out, u_ref = rest[n_cast + 2:2 * n_cast + 2], rest[2 * n_cast + 2]
    tn = z_ref.shape[1]

    @pl.when(pl.program_id(1) == 0)
    def _():
        u_ref[...] = _rms(x_ref[...], nw_ref[...]).astype(BF16)

    col0 = pl.program_id(1) * tn
    zscale = jnp.where((col0 >= AQ_COLS[0]) & (col0 < AQ_COLS[1]), AT_QSCALE, 1.0).astype(F32)
    for c in range(tn // IN_SLAB):
        cols = slice(c * IN_SLAB, (c + 1) * IN_SLAB)
        acc = _dot(u_ref[...], w_ref[:, cols].astype(BF16))
        z_ref[:, cols] = (acc * zscale).astype(z_ref.dtype)
        zf_ref[:, cols] = acc

    for src, dst in zip(cast_in, cast_out):
        dst[...] = src[...].astype(BF16)


def _in_proj(x2, norm_w, w, later_weights, *, tm=IN_TM, tn=IN_TN):
    m, k = x2.shape
    n = w.shape[1]
    nj = n // tn
    assert (m // tm) * nj >= CAST_BLOCKS and HG_WIDTH % tn == 0
    assert AQ_COLS[0] % tn == 0 and AQ_COLS[1] % tn == 0
    hf_first, n_hf = HG_WIDTH // tn, HG_WIDTH // tn

    def cast_spec(wl):
        rows, cols = wl.shape
        return pl.BlockSpec((rows // CAST_BLOCKS, cols),
                            lambda i, j: (jnp.minimum(i * nj + j, CAST_BLOCKS - 1), 0))

    cast_specs = [cast_spec(wl) for wl in later_weights]
    outs = pl.pallas_call(
        functools.partial(_in_proj_kernel, n_cast=len(later_weights)),
        out_shape=(jax.ShapeDtypeStruct((m, n), BF16),
                   jax.ShapeDtypeStruct((m, ZF_WIDTH), F32),
                   *[jax.ShapeDtypeStruct(wl.shape, BF16) for wl in later_weights]),
        grid=(m // tm, nj),
        in_specs=[
            pl.BlockSpec((tm, k), lambda i, j: (i, 0), pipeline_mode=pl.Buffered(1)),
            pl.BlockSpec((1, k), lambda i, j: (0, 0)),
            pl.BlockSpec((k, tn), lambda i, j: (0, j)),
            *cast_specs,
        ],
        out_specs=(
            pl.BlockSpec((tm, tn), lambda i, j: (i, j)),
            pl.BlockSpec((tm, tn), lambda i, j: (i, jnp.clip(j - hf_first, 0, n_hf))),
            *cast_specs,
        ),
        scratch_shapes=[pltpu.VMEM((tm, k), BF16)],
        compiler_params=pltpu.CompilerParams(
            dimension_semantics=("arbitrary", "arbitrary"),
            vmem_limit_bytes=VMEM_LIMIT),
        name="in_proj",
    )(x2, norm_w.reshape(1, k), w, *later_weights)
    return outs[0], outs[1], outs[2:]


MXU_LEVELS = 3


def _hgrn_tables():
    t = np.arange(CHUNK)[:, None]
    s = np.arange(CHUNK)[None, :]
    mats, masks = [], []
    for l in range(N_LEVELS):
        h = 1 << l
        start = (t // (2 * h)) * (2 * h)
        ref = start + h - 1
        is_q = (t - start) >= h
        if l < MXU_LEVELS:
            mats.append(np.where(is_q, (s > ref) & (s <= t), (s > t) & (s <= ref)))
        s_start = (s // (2 * h)) * (2 * h)
        masks.append((start == s_start) & is_q & ((s - s_start) < h))
    mats.append(s <= t)
    masks.append(t == s)
    w = np.concatenate(mats, axis=0).astype(np.float32)
    w2 = np.concatenate([w, w], axis=1)
    return w2, np.stack(masks).astype(np.float32)


def _level_exponent(b, h):
    parts = []
    for p in range(b.shape[0] // (2 * h)):
        lo = p * 2 * h
        ref = b[lo + h - 1:lo + h, :]
        parts += [ref - b[lo:lo + h, :], b[lo + h:lo + 2 * h, :] - ref]
    return jnp.concatenate(parts, axis=0)


N_OPS = N_LEVELS + 2
OP_DIAG, OP_STATE = N_LEVELS, N_LEVELS + 1


def _hgrn_units(hq_ref, hf_ref, hi_ref, hg_ref, lbl_ref, nw_ref, wexp_ref, mask_ref, y_ref,
                st_ref, qx_ref, kx_ref, d_ref, *, first_block, n_chunks):
    @pl.when(first_block)
    def _():
        st_ref[...] = jnp.zeros_like(st_ref)

    logits = lbl_ref[...]
    l0, l1 = logits[0:1, :], logits[1:2, :]
    mx = jnp.maximum(l0, l1)
    e0, e1 = jnp.exp(l0 - mx), jnp.exp(l1 - mx)
    lb = e0 / (e0 + e1)
    nw = nw_ref[...]
    wexp = wexp_ref[...]

    def stage(c):
        slot = c % 2
        r = slice(c * CHUNK, (c + 1) * CHUNK)
        f = jax.nn.sigmoid(hf_ref[r, :])
        g = lb + (1.0 - lb) * f
        lg = jnp.log2(g)
        kk = (1.0 - g).astype(BF16)
        q = (jax.nn.silu(hq_ref[r, :].astype(F32)) * (HG_DK ** -0.5)).astype(BF16)
        lg_hi = lg.astype(BF16)
        lg_lo = (lg - lg_hi.astype(F32)).astype(BF16)
        e_mxu = _dot(wexp, jnp.concatenate([lg_hi, lg_lo], axis=0))
        b = e_mxu[MXU_LEVELS * CHUNK:, :]

        def put(ref, op, val):
            for hh in range(HG_HEADS):
                ref[slot, op, hh] = val[:, hh * HG_DK:(hh + 1) * HG_DK]

        put(qx_ref, OP_DIAG, q)
        put(kx_ref, OP_DIAG, kk)
        for l in range(N_LEVELS):
            e = e_mxu[l * CHUNK:(l + 1) * CHUNK, :] if l < MXU_LEVELS else _level_exponent(b, 1 << l)
            x = jnp.exp2(e).astype(BF16)
            put(qx_ref, l, q * x)
            put(kx_ref, l, kk * x)
        eb = jnp.exp2(b)
        d_ref[slot] = eb[CHUNK - 1:CHUNK, :]
        put(qx_ref, OP_STATE, q * eb.astype(BF16))
        put(kx_ref, OP_STATE, kk * jnp.exp2(b[CHUNK - 1:CHUNK, :] - b).astype(BF16))

    def scores(c, h):
        slot = c % 2
        sc = mask_ref[OP_DIAG] * _dot(qx_ref[slot, OP_DIAG, h], kx_ref[slot, OP_DIAG, h], NT)
        for l in range(N_LEVELS):
            sc = sc + mask_ref[l] * _dot(qx_ref[slot, l, h], kx_ref[slot, l, h], NT)
        return sc.astype(BF16)

    def outputs(c, h, sc):
        slot, sl = c % 2, slice(h * HG_DK, (h + 1) * HG_DK)
        r = slice(c * CHUNK, (c + 1) * CHUNK)
        vh = hi_ref[r, sl]
        st = st_ref[h]
        o = _dot(sc, vh) + _dot(qx_ref[slot, OP_STATE, h], st.astype(BF16), NT)
        st_ref[h] = d_ref[slot, :, sl] * st + _dot(vh, kx_ref[slot, OP_STATE, h], TN)
        y = _rms(o, nw) * jax.nn.silu(hg_ref[r, sl].astype(F32))
        y_ref[r, sl] = y.astype(y_ref.dtype)

    stage(0)
    yield
    for c in range(n_chunks):
        sc_next = scores(c, 0)
        for h in range(HG_HEADS):
            sc = sc_next
            if h + 1 < HG_HEADS:
                sc_next = scores(c, h + 1)
            if h == 0 and c + 1 < n_chunks:
                stage(c + 1)
            outputs(c, h, sc)
            yield


ATT_TK = ATT_TQ + LEFT_CHUNKS * CHUNK
ATT_ROLL = 1024
assert ATT_TQ - 1 + ATT_TK <= ATT_ROLL and ATT_TK - ATT_TQ == 2 * REL_CLIP


def _rel_table(rel_bias):
    rev = rel_bias[:, ::-1].astype(F32) * LOG2E
    edge = jnp.broadcast_to(rev[:, :1], (rel_bias.shape[0], REL_CLIP))
    g = jnp.concatenate([edge, rev[:, :2 * REL_CLIP], edge], axis=1)
    return g.reshape(AT_HEADS // 2, 2, ATT_ROLL)


ATT_PAD = ATT_TK - ATT_TQ
ATT_EDGE = ATT_PAD // ATT_TQ
ATT_RG = 32


def _attn_units(q_ref, k_ref, v_ref, g_ref, o_ref, bias_ref, kp_ref, vp_ref, s_ref, p_ref,
                *, new_head_pair, seq):
    @pl.when(new_head_pair)
    def _():
        qc = lax.broadcasted_iota(jnp.int32, (ATT_TQ, ATT_TK), 0) // CHUNK
        kc = lax.broadcasted_iota(jnp.int32, (ATT_TQ, ATT_TK), 1) // CHUNK
        in_band = (kc >= qc) & (kc <= qc + LEFT_CHUNKS)
        g = g_ref[0]
        for hh in range(2):
            tbl = jnp.broadcast_to(g[hh:hh + 1, :], (ATT_TQ, ATT_ROLL))
            toep = pltpu.roll(tbl, 0, 1, stride=1, stride_axis=0)[:, :ATT_TK]
            rows = slice(hh * ATT_TQ, (hh + 1) * ATT_TQ)
            for n in range(ATT_EDGE + 1):
                first_chunk = (ATT_PAD - n * ATT_TQ) // CHUNK if n < ATT_EDGE else 0
                bias_ref[n, rows, :] = jnp.where(in_band & (kc >= first_chunk), toep, -jnp.inf)

    v_all = v_ref[...]
    head0_all = lax.broadcasted_iota(jnp.int32, v_all.shape, 1) < AT_DH
    ones = jnp.ones_like(v_all)
    pad = jnp.zeros((ATT_PAD, LANES), BF16)
    kp_ref[:ATT_PAD, :] = pad
    kp_ref[ATT_PAD:, :] = k_ref[...]
    vp_ref[0, :ATT_PAD, :] = pad
    vp_ref[1, :ATT_PAD, :] = pad
    vp_ref[0, ATT_PAD:, :] = jnp.where(head0_all, v_all, ones)
    vp_ref[1, ATT_PAD:, :] = jnp.where(head0_all, ones, v_all)

    head0 = lax.broadcasted_iota(jnp.int32, (ATT_TQ, LANES), 1) < AT_DH
    n_blocks = seq // ATT_TQ
    yield

    def scores(i):
        slot = i % 2
        q = q_ref[i * ATT_TQ:(i + 1) * ATT_TQ, :]
        zero = jnp.zeros_like(q)
        qq = jnp.concatenate([jnp.where(head0, q, zero), jnp.where(head0, zero, q)], axis=0)
        s = _dot(qq, kp_ref[i * ATT_TQ:i * ATT_TQ + ATT_TK, :], NT)
        s_ref[slot] = s + bias_ref[min(i, ATT_EDGE)]

    def outputs(i):
        slot = i % 2
        for g in range(2 * ATT_TQ // ATT_RG):
            rows = slice(g * ATT_RG, (g + 1) * ATT_RG)
            s = s_ref[slot, rows, :]
            p_ref[slot, rows, :] = jnp.exp2(s - jnp.max(s, axis=-1, keepdims=True)).astype(BF16)
        keys = slice(i * ATT_TQ, i * ATT_TQ + ATT_TK)
        acc0 = _dot(p_ref[slot, :ATT_TQ, :], vp_ref[0, keys, :])
        acc1 = _dot(p_ref[slot, ATT_TQ:, :], vp_ref[1, keys, :])
        num = jnp.where(head0, acc0, acc1)
        den = pltpu.roll(jnp.where(head0, acc1, acc0), AT_DH, 1)
        o_ref[i * ATT_TQ:(i + 1) * ATT_TQ, :] = (num / den).astype(o_ref.dtype)

    scores(0)
    for i in range(n_blocks):
        if i + 1 < n_blocks:
            scores(i + 1)
        outputs(i)
        yield


def _mixers_kernel(hq_ref, hf_ref, hi_ref, hg_ref, lbl_ref, nw_ref, wexp_ref, mask_ref,
                   q_ref, k_ref, v_ref, g_ref, cast_a_ref, cast_b_ref,
                   y_ref, o_ref, cast_a_out, cast_b_out,
                   st_ref, qx_ref, kx_ref, d_ref, bias_ref, kp_ref, vp_ref, s_ref, p_ref,
                   *, n_chunks, seq, steps_per_pair):
    cast_a_out[...] = cast_a_ref[...].astype(BF16)
    cast_b_out[...] = cast_b_ref[...].astype(BF16)
    step = pl.program_id(0) * pl.num_programs(1) + pl.program_id(1)
    hgrn = _hgrn_units(hq_ref, hf_ref, hi_ref, hg_ref, lbl_ref, nw_ref, wexp_ref, mask_ref,
                       y_ref, st_ref, qx_ref, kx_ref, d_ref,
                       first_block=pl.program_id(1) == 0, n_chunks=n_chunks)
    attn = _attn_units(q_ref, k_ref, v_ref, g_ref, o_ref, bias_ref, kp_ref, vp_ref, s_ref, p_ref,
                       new_head_pair=step % steps_per_pair == 0, seq=seq)
    n_hgrn, n_attn = n_chunks * HG_HEADS, seq // ATT_TQ
    assert n_hgrn % n_attn == 0
    next(hgrn)
    next(attn)
    for _ in range(n_attn):
        next(attn)
        for _ in range(n_hgrn // n_attn):
            next(hgrn)
    assert next(attn, None) is None and next(hgrn, None) is None


def _mixers(z, hf, lb_logits, hg_norm_w, rel_bias, cast_a, cast_b, batch, seq, *, tb=MIX_TB):
    wexp, masks = _hgrn_tables()
    g = _rel_table(rel_bias)
    nblk = seq // tb
    steps = batch * nblk
    assert steps == AT_HEADS // 2 * batch
    col0 = 4 * HG_WIDTH // LANES
    ncol = AT_WIDTH // LANES

    def step_of(b, i):
        return b * nblk + i

    def hspec(col):
        return pl.BlockSpec((tb, HG_WIDTH), lambda b, i, col=col: (b * nblk + i, col))

    def aspec(which):
        return pl.BlockSpec((seq, LANES), lambda b, i, which=which: (
            step_of(b, i) % batch, col0 + which * ncol + step_of(b, i) // batch))

    def const(shape):
        return pl.BlockSpec(shape, lambda b, i: (0,) * len(shape))

    def cast_spec(w):
        return pl.BlockSpec((w.shape[0] // steps, w.shape[1]), lambda b, i: (step_of(b, i), 0))

    return pl.pallas_call(
        functools.partial(_mixers_kernel, n_chunks=tb // CHUNK, seq=seq, steps_per_pair=batch),
        out_shape=(jax.ShapeDtypeStruct((batch * seq, HG_WIDTH), BF16),
                   jax.ShapeDtypeStruct((batch * seq, AT_WIDTH), BF16),
                   jax.ShapeDtypeStruct(cast_a.shape, BF16),
                   jax.ShapeDtypeStruct(cast_b.shape, BF16)),
        grid=(batch, nblk),
        in_specs=[
            hspec(0), hspec(0), hspec(2), hspec(3),
            const((2, HG_WIDTH)), const((1, HG_DV)), const(wexp.shape), const(masks.shape),
            aspec(0), aspec(1), aspec(2),
            pl.BlockSpec((1, 2, ATT_ROLL), lambda b, i: (step_of(b, i) // batch, 0, 0)),
            cast_spec(cast_a), cast_spec(cast_b),
        ],
        out_specs=(
            pl.BlockSpec((tb, HG_WIDTH), lambda b, i: (b * nblk + i, 0)),
            pl.BlockSpec((seq, LANES), lambda b, i: (step_of(b, i) % batch, step_of(b, i) // batch)),
            cast_spec(cast_a), cast_spec(cast_b),
        ),
        scratch_shapes=[pltpu.VMEM((HG_HEADS, HG_DV, HG_DK), F32),
                        pltpu.VMEM((2, N_OPS, HG_HEADS, CHUNK, HG_DK), BF16),
                        pltpu.VMEM((2, N_OPS, HG_HEADS, CHUNK, HG_DK), BF16),
                        pltpu.VMEM((2, 1, HG_WIDTH), F32),
                        pltpu.VMEM((ATT_EDGE + 1, 2 * ATT_TQ, ATT_TK), F32),
                        pltpu.VMEM((ATT_PAD + seq, LANES), BF16),
                        pltpu.VMEM((2, ATT_PAD + seq, LANES), BF16),
                        pltpu.VMEM((2, 2 * ATT_TQ, ATT_TK), F32),
                        pltpu.VMEM((2, 2 * ATT_TQ, ATT_TK), BF16)],
        compiler_params=pltpu.CompilerParams(
            dimension_semantics=("arbitrary", "arbitrary"),
            vmem_limit_bytes=VMEM_LIMIT),
        name="mixers",
    )(z, hf, z, z, lb_logits, hg_norm_w.reshape(1, HG_DV), jnp.asarray(wexp, BF16), jnp.asarray(masks),
      z, z, z, g, cast_a, cast_b)


def _merge_out_kernel(ya_ref, yb_ref, ga0_ref, ga1_ref, gb0_ref, gb1_ref, x_ref,
                      wa_ref, wb_ref, wo_ref, nw_ref, h_ref, u_ref, *, rc):
    for c in range(h_ref.shape[0] // rc):
        r = slice(c * rc, (c + 1) * rc)
        pa = _dot(ya_ref[r, :], wa_ref[...])
        pb = _dot(yb_ref[r, :], wb_ref[...])
        ga = jnp.concatenate([ga0_ref[r, :], ga1_ref[r, :]], axis=1).astype(F32)
        gb = jnp.concatenate([gb0_ref[r, :], gb1_ref[r, :]], axis=1).astype(F32)
        merged = (jax.nn.sigmoid(ga) * pa + jax.nn.sigmoid(gb) * pb).astype(BF16)
        h = x_ref[r, :] + _dot(merged, wo_ref[...])
        h_ref[r, :] = h
        u_ref[r, :] = _rms(h, nw_ref[...]).astype(BF16)


def _merge_out(ya, yb, z, x2, wa_bf, wb_bf, wo_bf, norm_w, *, tm=MERGE_TM, rc=MERGE_RC):
    m, d = x2.shape
    gw = d // 2
    gcol = (4 * HG_WIDTH + 3 * AT_WIDTH) // gw
    assert gcol * gw == 4 * HG_WIDTH + 3 * AT_WIDTH

    def resident(shape):
        return pl.BlockSpec(shape, lambda i: (0, 0), pipeline_mode=pl.Buffered(1))

    def gate(blk):
        return pl.BlockSpec((tm, gw), lambda i, blk=blk: (i, gcol + blk))

    return pl.pallas_call(
        functools.partial(_merge_out_kernel, rc=rc),
        out_shape=(jax.ShapeDtypeStruct((m, d), F32), jax.ShapeDtypeStruct((m, d), BF16)),
        grid=(m // tm,),
        in_specs=[
            pl.BlockSpec((tm, HG_WIDTH), lambda i: (i, 0)),
            pl.BlockSpec((tm, AT_WIDTH), lambda i: (i, 0)),
            gate(0), gate(1), gate(2), gate(3),
            pl.BlockSpec((tm, d), lambda i: (i, 0)),
            resident(wa_bf.shape), resident(wb_bf.shape), resident(wo_bf.shape),
            resident((1, d)),
        ],
        out_specs=(pl.BlockSpec((tm, d), lambda i: (i, 0)), pl.BlockSpec((tm, d), lambda i: (i, 0))),
        compiler_params=pltpu.CompilerParams(
            dimension_semantics=("parallel",),
            vmem_limit_bytes=VMEM_LIMIT),
        name="merge_out",
    )(ya, yb, z, z, z, z, x2, wa_bf, wb_bf, wo_bf, norm_w.reshape(1, d))


MLP_SLAB = 2 * MXU_TILE


def _mlp_kernel(h_ref, u_ref, wu_ref, wd_ref, fw_ref, o_ref):
    f = pl.program_id(1)

    @pl.when(f == 0)
    def _():
        o_ref[...] = jnp.zeros_like(o_ref)

    a = jnp.maximum(_dot(u_ref[...], wu_ref[...]), 0.0)
    a = (a * a).astype(BF16)
    for c in range(o_ref.shape[1] // MLP_SLAB):
        cols = slice(c * MLP_SLAB, (c + 1) * MLP_SLAB)
        o_ref[:, cols] += _dot(a, wd_ref[:, cols])

    @pl.when(f == pl.num_programs(1) - 1)
    def _():
        o_ref[...] = _rms(h_ref[...] + o_ref[...], fw_ref[...])


def _mlp(h, u, wu_bf, wd_bf, final_w, *, tm=MLP_TM, tf=MLP_TF):
    m, d = h.shape
    dff = wu_bf.shape[1]
    return pl.pallas_call(
        _mlp_kernel,
        out_shape=jax.ShapeDtypeStruct((m, d), F32),
        grid=(m // tm, dff // tf),
        in_specs=[
            pl.BlockSpec((tm, d), lambda i, f: (i, 0)),
            pl.BlockSpec((tm, d), lambda i, f: (i, 0)),
            pl.BlockSpec((d, tf), lambda i, f: (0, f)),
            pl.BlockSpec((tf, d), lambda i, f: (f, 0)),
            pl.BlockSpec((1, d), lambda i, f: (0, 0)),
        ],
        out_specs=pl.BlockSpec((tm, d), lambda i, f: (i, 0)),
        compiler_params=pltpu.CompilerParams(
            dimension_semantics=("parallel", "arbitrary"),
            vmem_limit_bytes=VMEM_LIMIT),
        name="mlp",
    )(h, u, wu_bf, wd_bf, final_w.reshape(1, d))


def kernel(x, w_in, lb_logits, hg_norm_w, rel_bias, w_branch_a, w_branch_b, w_out,
           norm_mix_w, norm_mlp_w, w_up, w_down, norm_final_w):
    batch, seq, d = x.shape
    assert d == D_MODEL and seq % ATT_TQ == 0 and w_in.shape[0] == 1
    x2 = x.reshape(batch * seq, d)
    z, hf, (wa_bf, wb_bf, wo_bf) = _in_proj(
        x2, norm_mix_w[0], w_in[0], (w_branch_a[0], w_branch_b[0], w_out[0]))
    ya, yb, wu_bf, wd_bf = _mixers(z, hf, lb_logits, hg_norm_w[0], rel_bias[0], w_up[0],
                                   w_down[0], batch, seq)
    h, u = _merge_out(ya, yb, z, x2, wa_bf, wb_bf, wo_bf, norm_mlp_w[0])
    out = _mlp(h, u, wu_bf, wd_bf, norm_final_w)
    return out.reshape(batch, seq, d)
```

```python
import functools

import jax
import jax.numpy as jnp
import numpy as np
from jax import lax
from jax.experimental import pallas as pl
from jax.experimental.pallas import tpu as pltpu

D_MODEL = 2048
CHUNK = 64
HG_HEADS = 8
HG_DK = 128
HG_DV = 128
HG_WIDTH = HG_HEADS * HG_DV
AT_HEADS = 16
AT_DH = 64
AT_WIDTH = AT_HEADS * AT_DH
LEFT_CHUNKS = 8
REL_CLIP = 256
EPS = 1e-6

LANES = 128
MXU_TILE = 256
N_LEVELS = 6
ATT_TQ = 2 * CHUNK

BF16 = jnp.bfloat16
F32 = jnp.float32
VMEM_LIMIT = 60000 * 1024

IN_TM, IN_TN = 2048, 512
MIX_TB = 4 * CHUNK
MERGE_TM, MERGE_RC = 512, 256
MLP_TM, MLP_TF = 1024, 512

NN = (((1,), (0,)), ((), ()))
NT = (((1,), (1,)), ((), ()))
TN = (((0,), (0,)), ((), ()))


def _dot(a, b, dims=NN):
    return lax.dot_general(a, b, dims, preferred_element_type=F32)


def _rms(xf, w):
    return xf * lax.rsqrt(jnp.mean(xf * xf, axis=-1, keepdims=True) + EPS) * w


CAST_BLOCKS = 64
IN_SLAB = MXU_TILE
LOG2E = 1.4426950408889634
AT_QSCALE = AT_DH ** -0.5 * LOG2E
AQ_COLS = (4 * HG_WIDTH, 4 * HG_WIDTH + AT_WIDTH)
ZF_WIDTH = HG_WIDTH + IN_TN


def _in_proj_kernel(x_ref, nw_ref, w_ref, *rest, n_cast):
    cast_in, (z_ref, zf_ref) = rest[:n_cast], rest[n_cast:n_cast + 2]
    cast_out, (u_ref, xbuf_ref, x_sem) = rest[n_cast + 2:2 * n_cast + 2], rest[2 * n_cast + 2:]
    tm, tn = z_ref.shape
    i, j = pl.program_id(0), pl.program_id(1)

    def x_copy(tile):
        return pltpu.make_async_copy(x_ref.at[pl.ds(tile * tm, tm), :], xbuf_ref, x_sem)

    @pl.when((i == 0) & (j == 0))
    def _():
        x_copy(0).start()

    @pl.when(j == 0)
    def _():
        x_copy(i).wait()
        u_ref[...] = _rms(xbuf_ref[...], nw_ref[...]).astype(BF16)

    @pl.when((j == 1) & (i + 1 < pl.num_programs(0)))
    def _():
        x_copy(i + 1).start()

    col0 = pl.program_id(1) * tn
    zscale = jnp.where((col0 >= AQ_COLS[0]) & (col0 < AQ_COLS[1]), AT_QSCALE, 1.0).astype(F32)
    for c in range(tn // IN_SLAB):
        cols = slice(c * IN_SLAB, (c + 1) * IN_SLAB)
        acc = _dot(u_ref[...], w_ref[:, cols].astype(BF16))
        z_ref[:, cols] = (acc * zscale).astype(z_ref.dtype)
        zf_ref[:, cols] = acc

    for src, dst in zip(cast_in, cast_out):
        dst[...] = src[...].astype(BF16)


def _in_proj(x2, norm_w, w, later_weights, *, tm=IN_TM, tn=IN_TN):
    m, k = x2.shape
    n = w.shape[1]
    nj = n // tn
    assert (m // tm) * nj >= CAST_BLOCKS and HG_WIDTH % tn == 0
    assert AQ_COLS[0] % tn == 0 and AQ_COLS[1] % tn == 0
    hf_first, n_hf = HG_WIDTH // tn, HG_WIDTH // tn

    def cast_spec(wl):
        rows, cols = wl.shape
        return pl.BlockSpec((rows // CAST_BLOCKS, cols),
                            lambda i, j: (jnp.minimum(i * nj + j, CAST_BLOCKS - 1), 0))

    cast_specs = [cast_spec(wl) for wl in later_weights]
    outs = pl.pallas_call(
        functools.partial(_in_proj_kernel, n_cast=len(later_weights)),
        out_shape=(jax.ShapeDtypeStruct((m, n), BF16),
                   jax.ShapeDtypeStruct((m, ZF_WIDTH), F32),
                   *[jax.ShapeDtypeStruct(wl.shape, BF16) for wl in later_weights]),
        grid=(m // tm, nj),
        in_specs=[
            pl.BlockSpec(memory_space=pl.ANY),
            pl.BlockSpec((1, k), lambda i, j: (0, 0)),
            pl.BlockSpec((k, tn), lambda i, j: (0, j)),
            *cast_specs,
        ],
        out_specs=(
            pl.BlockSpec((tm, tn), lambda i, j: (i, j)),
            pl.BlockSpec((tm, tn), lambda i, j: (i, jnp.clip(j - hf_first, 0, n_hf))),
            *cast_specs,
        ),
        scratch_shapes=[pltpu.VMEM((tm, k), BF16), pltpu.VMEM((tm, k), F32), pltpu.SemaphoreType.DMA],
        compiler_params=pltpu.CompilerParams(
            dimension_semantics=("arbitrary", "arbitrary"),
            vmem_limit_bytes=VMEM_LIMIT),
        name="in_proj",
    )(x2, norm_w.reshape(1, k), w, *later_weights)
    return outs[0], outs[1], outs[2:]


MXU_LEVELS = 3


def _hgrn_tables():
    t = np.arange(CHUNK)[:, None]
    s = np.arange(CHUNK)[None, :]
    mats, masks = [], []
    for l in range(N_LEVELS):
        h = 1 << l
        start = (t // (2 * h)) * (2 * h)
        ref = start + h - 1
        is_q = (t - start) >= h
        if l < MXU_LEVELS:
            mats.append(np.where(is_q, (s > ref) & (s <= t), (s > t) & (s <= ref)))
        s_start = (s // (2 * h)) * (2 * h)
        masks.append((start == s_start) & is_q & ((s - s_start) < h))
    mats.append(s <= t)
    masks.append(t == s)
    w = np.concatenate(mats, axis=0).astype(np.float32)
    w2 = np.concatenate([w, w], axis=1)
    return w2, np.stack(masks).astype(np.float32)


def _level_exponent(b, h):
    parts = []
    for p in range(b.shape[0] // (2 * h)):
        lo = p * 2 * h
        ref = b[lo + h - 1:lo + h, :]
        parts += [ref - b[lo:lo + h, :], b[lo + h:lo + 2 * h, :] - ref]
    return jnp.concatenate(parts, axis=0)


N_OPS = N_LEVELS + 2
OP_DIAG, OP_STATE = N_LEVELS, N_LEVELS + 1


def _hgrn_units(hq_ref, hf_ref, hi_ref, hg_ref, lbl_ref, nw_ref, wexp_ref, mask_ref, y_ref,
                st_ref, qx_ref, kx_ref, d_ref, *, first_block, n_chunks):
    @pl.when(first_block)
    def _():
        st_ref[...] = jnp.zeros_like(st_ref)

    logits = lbl_ref[...]
    l0, l1 = logits[0:1, :], logits[1:2, :]
    mx = jnp.maximum(l0, l1)
    e0, e1 = jnp.exp(l0 - mx), jnp.exp(l1 - mx)
    lb = e0 / (e0 + e1)
    nw = nw_ref[...]
    wexp = wexp_ref[...]

    def stage(c):
        slot = c % 2
        r = slice(c * CHUNK, (c + 1) * CHUNK)
        f = jax.nn.sigmoid(hf_ref[r, :])
        g = lb + (1.0 - lb) * f
        lg = jnp.log2(g)
        kk = (1.0 - g).astype(BF16)
        q = (jax.nn.silu(hq_ref[r, :].astype(F32)) * (HG_DK ** -0.5)).astype(BF16)
        lg_hi = lg.astype(BF16)
        lg_lo = (lg - lg_hi.astype(F32)).astype(BF16)
        e_mxu = _dot(wexp, jnp.concatenate([lg_hi, lg_lo], axis=0))
        b = e_mxu[MXU_LEVELS * CHUNK:, :]

        def put(ref, op, val):
            for hh in range(HG_HEADS):
                ref[slot, op, hh] = val[:, hh * HG_DK:(hh + 1) * HG_DK]

        put(qx_ref, OP_DIAG, q)
        put(kx_ref, OP_DIAG, kk)
        for l in range(N_LEVELS):
            e = e_mxu[l * CHUNK:(l + 1) * CHUNK, :] if l < MXU_LEVELS else _level_exponent(b, 1 << l)
            x = jnp.exp2(e).astype(BF16)
            put(qx_ref, l, q * x)
            put(kx_ref, l, kk * x)
        eb = jnp.exp2(b)
        d_ref[slot] = eb[CHUNK - 1:CHUNK, :]
        put(qx_ref, OP_STATE, q * eb.astype(BF16))
        put(kx_ref, OP_STATE, kk * jnp.exp2(b[CHUNK - 1:CHUNK, :] - b).astype(BF16))

    def scores(c, h):
        slot = c % 2
        sc = mask_ref[OP_DIAG] * _dot(qx_ref[slot, OP_DIAG, h], kx_ref[slot, OP_DIAG, h], NT)
        for l in range(N_LEVELS):
            sc = sc + mask_ref[l] * _dot(qx_ref[slot, l, h], kx_ref[slot, l, h], NT)
        return sc.astype(BF16)

    def outputs(c, h, sc):
        slot, sl = c % 2, slice(h * HG_DK, (h + 1) * HG_DK)
        r = slice(c * CHUNK, (c + 1) * CHUNK)
        vh = hi_ref[r, sl]
        st = st_ref[h]
        o = _dot(sc, vh) + _dot(qx_ref[slot, OP_STATE, h], st.astype(BF16), NT)
        st_ref[h] = d_ref[slot, :, sl] * st + _dot(vh, kx_ref[slot, OP_STATE, h], TN)
        y = _rms(o, nw) * jax.nn.silu(hg_ref[r, sl].astype(F32))
        y_ref[r, sl] = y.astype(y_ref.dtype)

    stage(0)
    yield
    for c in range(n_chunks):
        sc_next = scores(c, 0)
        for h in range(HG_HEADS):
            sc = sc_next
            if h + 1 < HG_HEADS:
                sc_next = scores(c, h + 1)
            if h == 0 and c + 1 < n_chunks:
                stage(c + 1)
            outputs(c, h, sc)
            yield


ATT_TK = ATT_TQ + LEFT_CHUNKS * CHUNK
ATT_ROLL = 1024
assert ATT_TQ - 1 + ATT_TK <= ATT_ROLL and ATT_TK - ATT_TQ == 2 * REL_CLIP


def _rel_table(rel_bias):
    rev = rel_bias[:, ::-1].astype(F32) * LOG2E
    edge = jnp.broadcast_to(rev[:, :1], (rel_bias.shape[0], REL_CLIP))
    g = jnp.concatenate([edge, rev[:, :2 * REL_CLIP], edge], axis=1)
    return g.reshape(AT_HEADS // 2, 2, ATT_ROLL)


ATT_PAD = ATT_TK - ATT_TQ
ATT_EDGE = ATT_PAD // ATT_TQ
ATT_RG = 32


def _attn_units(q_ref, k_ref, v_ref, g_ref, o_ref, bias_ref, kp_ref, vp_ref, s_ref, p_ref,
                *, new_head_pair, seq):
    @pl.when(new_head_pair)
    def _():
        qc = lax.broadcasted_iota(jnp.int32, (ATT_TQ, ATT_TK), 0) // CHUNK
        kc = lax.broadcasted_iota(jnp.int32, (ATT_TQ, ATT_TK), 1) // CHUNK
        in_band = (kc >= qc) & (kc <= qc + LEFT_CHUNKS)
        g = g_ref[0]
        for hh in range(2):
            tbl = jnp.broadcast_to(g[hh:hh + 1, :], (ATT_TQ, ATT_ROLL))
            toep = pltpu.roll(tbl, 0, 1, stride=1, stride_axis=0)[:, :ATT_TK]
            rows = slice(hh * ATT_TQ, (hh + 1) * ATT_TQ)
            for n in range(ATT_EDGE + 1):
                first_chunk = (ATT_PAD - n * ATT_TQ) // CHUNK if n < ATT_EDGE else 0
                bias_ref[n, rows, :] = jnp.where(in_band & (kc >= first_chunk), toep, -jnp.inf)

    v_all = v_ref[...]
    head0_all = lax.broadcasted_iota(jnp.int32, v_all.shape, 1) < AT_DH
    ones = jnp.ones_like(v_all)
    pad = jnp.zeros((ATT_PAD, LANES), BF16)
    kp_ref[:ATT_PAD, :] = pad
    kp_ref[ATT_PAD:, :] = k_ref[...]
    vp_ref[0, :ATT_PAD, :] = pad
    vp_ref[1, :ATT_PAD, :] = pad
    vp_ref[0, ATT_PAD:, :] = jnp.where(head0_all, v_all, ones)
    vp_ref[1, ATT_PAD:, :] = jnp.where(head0_all, ones, v_all)

    head0 = lax.broadcasted_iota(jnp.int32, (ATT_TQ, LANES), 1) < AT_DH
    n_blocks = seq // ATT_TQ
    yield

    def scores(i):
        slot = i % 2
        q = q_ref[i * ATT_TQ:(i + 1) * ATT_TQ, :]
        zero = jnp.zeros_like(q)
        qq = jnp.concatenate([jnp.where(head0, q, zero), jnp.where(head0, zero, q)], axis=0)
        s = _dot(qq, kp_ref[i * ATT_TQ:i * ATT_TQ + ATT_TK, :], NT)
        s_ref[slot] = s + bias_ref[min(i, ATT_EDGE)]

    def outputs(i):
        slot = i % 2
        for g in range(2 * ATT_TQ // ATT_RG):
            rows = slice(g * ATT_RG, (g + 1) * ATT_RG)
            s = s_ref[slot, rows, :]
            p_ref[slot, rows, :] = jnp.exp2(s - jnp.max(s, axis=-1, keepdims=True)).astype(BF16)
        keys = slice(i * ATT_TQ, i * ATT_TQ + ATT_TK)
        acc0 = _dot(p_ref[slot, :ATT_TQ, :], vp_ref[0, keys, :])
        acc1 = _dot(p_ref[slot, ATT_TQ:, :], vp_ref[1, keys, :])
        num = jnp.where(head0, acc0, acc1)
        den = pltpu.roll(jnp.where(head0, acc1, acc0), AT_DH, 1)
        o_ref[i * ATT_TQ:(i + 1) * ATT_TQ, :] = (num / den).astype(o_ref.dtype)

    scores(0)
    for i in range(n_blocks):
        if i + 1 < n_blocks:
            scores(i + 1)
        outputs(i)
        yield


def _mixers_kernel(hq_ref, hf_ref, hi_ref, hg_ref, lbl_ref, nw_ref, wexp_ref, mask_ref,
                   q_ref, k_ref, v_ref, g_ref, cast_a_ref, cast_b_ref,
                   y_ref, o_ref, cast_a_out, cast_b_out,
                   st_ref, qx_ref, kx_ref, d_ref, bias_ref, kp_ref, vp_ref, s_ref, p_ref,
                   *, n_chunks, seq, steps_per_pair):
    cast_a_out[...] = cast_a_ref[...].astype(BF16)
    cast_b_out[...] = cast_b_ref[...].astype(BF16)
    step = pl.program_id(0) * pl.num_programs(1) + pl.program_id(1)
    hgrn = _hgrn_units(hq_ref, hf_ref, hi_ref, hg_ref, lbl_ref, nw_ref, wexp_ref, mask_ref,
                       y_ref, st_ref, qx_ref, kx_ref, d_ref,
                       first_block=pl.program_id(1) == 0, n_chunks=n_chunks)
    attn = _attn_units(q_ref, k_ref, v_ref, g_ref, o_ref, bias_ref, kp_ref, vp_ref, s_ref, p_ref,
                       new_head_pair=step % steps_per_pair == 0, seq=seq)
    n_hgrn, n_attn = n_chunks * HG_HEADS, seq // ATT_TQ
    assert n_hgrn % n_attn == 0
    next(hgrn)
    next(attn)
    for _ in range(n_attn):
        next(attn)
        for _ in range(n_hgrn // n_attn):
            next(hgrn)
    assert next(attn, None) is None and next(hgrn, None) is None


def _mixers(z, hf, lb_logits, hg_norm_w, rel_bias, cast_a, cast_b, batch, seq, *, tb=MIX_TB):
    wexp, masks = _hgrn_tables()
    g = _rel_table(rel_bias)
    nblk = seq // tb
    steps = batch * nblk
    assert steps == AT_HEADS // 2 * batch
    col0 = 4 * HG_WIDTH // LANES
    ncol = AT_WIDTH // LANES

    def step_of(b, i):
        return b * nblk + i

    def hspec(col):
        return pl.BlockSpec((tb, HG_WIDTH), lambda b, i, col=col: (b * nblk + i, col))

    def aspec(which):
        return pl.BlockSpec((seq, LANES), lambda b, i, which=which: (
            step_of(b, i) % batch, col0 + which * ncol + step_of(b, i) // batch))

    def const(shape):
        return pl.BlockSpec(shape, lambda b, i: (0,) * len(shape))

    def cast_spec(w):
        return pl.BlockSpec((w.shape[0] // steps, w.shape[1]), lambda b, i: (step_of(b, i), 0))

    return pl.pallas_call(
        functools.partial(_mixers_kernel, n_chunks=tb // CHUNK, seq=seq, steps_per_pair=batch),
        out_shape=(jax.ShapeDtypeStruct((batch * seq, HG_WIDTH), BF16),
                   jax.ShapeDtypeStruct((batch * seq, AT_WIDTH), BF16),
                   jax.ShapeDtypeStruct(cast_a.shape, BF16),
                   jax.ShapeDtypeStruct(cast_b.shape, BF16)),
        grid=(batch, nblk),
        in_specs=[
            hspec(0), hspec(0), hspec(2), hspec(3),
            const((2, HG_WIDTH)), const((1, HG_DV)), const(wexp.shape), const(masks.shape),
            aspec(0), aspec(1), aspec(2),
            pl.BlockSpec((1, 2, ATT_ROLL), lambda b, i: (step_of(b, i) // batch, 0, 0)),
            cast_spec(cast_a), cast_spec(cast_b),
        ],
        out_specs=(
            pl.BlockSpec((tb, HG_WIDTH), lambda b, i: (b * nblk + i, 0)),
            pl.BlockSpec((seq, LANES), lambda b, i: (step_of(b, i) % batch, step_of(b, i) // batch)),
            cast_spec(cast_a), cast_spec(cast_b),
        ),
        scratch_shapes=[pltpu.VMEM((HG_HEADS, HG_DV, HG_DK), F32),
                        pltpu.VMEM((2, N_OPS, HG_HEADS, CHUNK, HG_DK), BF16),
                        pltpu.VMEM((2, N_OPS, HG_HEADS, CHUNK, HG_DK), BF16),
                        pltpu.VMEM((2, 1, HG_WIDTH), F32),
                        pltpu.VMEM((ATT_EDGE + 1, 2 * ATT_TQ, ATT_TK), F32),
                        pltpu.VMEM((ATT_PAD + seq, LANES), BF16),
                        pltpu.VMEM((2, ATT_PAD + seq, LANES), BF16),
                        pltpu.VMEM((2, 2 * ATT_TQ, ATT_TK), F32),
                        pltpu.VMEM((2, 2 * ATT_TQ, ATT_TK), BF16)],
        compiler_params=pltpu.CompilerParams(
            dimension_semantics=("arbitrary", "arbitrary"),
            vmem_limit_bytes=VMEM_LIMIT),
        name="mixers",
    )(z, hf, z, z, lb_logits, hg_norm_w.reshape(1, HG_DV), jnp.asarray(wexp, BF16), jnp.asarray(masks),
      z, z, z, g, cast_a, cast_b)


def _merge_out_kernel(ya_ref, yb_ref, ga0_ref, ga1_ref, gb0_ref, gb1_ref, x_ref,
                      wa_ref, wb_ref, wo_ref, nw_ref, h_ref, u_ref, *, rc):
    for c in range(h_ref.shape[0] // rc):
        r = slice(c * rc, (c + 1) * rc)
        pa = _dot(ya_ref[r, :], wa_ref[...])
        pb = _dot(yb_ref[r, :], wb_ref[...])
        ga = jnp.concatenate([ga0_ref[r, :], ga1_ref[r, :]], axis=1).astype(F32)
        gb = jnp.concatenate([gb0_ref[r, :], gb1_ref[r, :]], axis=1).astype(F32)
        merged = (jax.nn.sigmoid(ga) * pa + jax.nn.sigmoid(gb) * pb).astype(BF16)
        h = x_ref[r, :] + _dot(merged, wo_ref[...])
        h_ref[r, :] = h
        u_ref[r, :] = _rms(h, nw_ref[...]).astype(BF16)


def _merge_out(ya, yb, z, x2, wa_bf, wb_bf, wo_bf, norm_w, *, tm=MERGE_TM, rc=MERGE_RC):
    m, d = x2.shape
    gw = d // 2
    gcol = (4 * HG_WIDTH + 3 * AT_WIDTH) // gw
    assert gcol * gw == 4 * HG_WIDTH + 3 * AT_WIDTH

    def resident(shape):
        return pl.BlockSpec(shape, lambda i: (0, 0), pipeline_mode=pl.Buffered(1))

    def gate(blk):
        return pl.BlockSpec((tm, gw), lambda i, blk=blk: (i, gcol + blk))

    return pl.pallas_call(
        functools.partial(_merge_out_kernel, rc=rc),
        out_shape=(jax.ShapeDtypeStruct((m, d), F32), jax.ShapeDtypeStruct((m, d), BF16)),
        grid=(m // tm,),
        in_specs=[
            pl.BlockSpec((tm, HG_WIDTH), lambda i: (i, 0)),
            pl.BlockSpec((tm, AT_WIDTH), lambda i: (i, 0)),
            gate(0), gate(1), gate(2), gate(3),
            pl.BlockSpec((tm, d), lambda i: (i, 0)),
            resident(wa_bf.shape), resident(wb_bf.shape), resident(wo_bf.shape),
            resident((1, d)),
        ],
        out_specs=(pl.BlockSpec((tm, d), lambda i: (i, 0)), pl.BlockSpec((tm, d), lambda i: (i, 0))),
        compiler_params=pltpu.CompilerParams(
            dimension_semantics=("parallel",),
            vmem_limit_bytes=VMEM_LIMIT),
        name="merge_out",
    )(ya, yb, z, z, z, z, x2, wa_bf, wb_bf, wo_bf, norm_w.reshape(1, d))


MLP_SLAB = 2 * MXU_TILE


def _mlp_kernel(h_ref, u_ref, wu_ref, wd_ref, fw_ref, o_ref):
    f = pl.program_id(1)

    @pl.when(f == 0)
    def _():
        o_ref[...] = jnp.zeros_like(o_ref)

    a = jnp.maximum(_dot(u_ref[...], wu_ref[...]), 0.0)
    a = (a * a).astype(BF16)
    for c in range(o_ref.shape[1] // MLP_SLAB):
        cols = slice(c * MLP_SLAB, (c + 1) * MLP_SLAB)
        o_ref[:, cols] += _dot(a, wd_ref[:, cols])

    @pl.when(f == pl.num_programs(1) - 1)
    def _():
        o_ref[...] = _rms(h_ref[...] + o_ref[...], fw_ref[...])


def _mlp(h, u, wu_bf, wd_bf, final_w, *, tm=MLP_TM, tf=MLP_TF):
    m, d = h.shape
    dff = wu_bf.shape[1]
    return pl.pallas_call(
        _mlp_kernel,
        out_shape=jax.ShapeDtypeStruct((m, d), F32),
        grid=(m // tm, dff // tf),
        in_specs=[
            pl.BlockSpec((tm, d), lambda i, f: (i, 0)),
            pl.BlockSpec((tm, d), lambda i, f: (i, 0)),
            pl.BlockSpec((d, tf), lambda i, f: (0, f)),
            pl.BlockSpec((tf, d), lambda i, f: (f, 0)),
            pl.BlockSpec((1, d), lambda i, f: (0, 0)),
        ],
        out_specs=pl.BlockSpec((tm, d), lambda i, f: (i, 0)),
        compiler_params=pltpu.CompilerParams(
            dimension_semantics=("parallel", "arbitrary"),
            vmem_limit_bytes=VMEM_LIMIT),
        name="mlp",
    )(h, u, wu_bf, wd_bf, final_w.reshape(1, d))


def kernel(x, w_in, lb_logits, hg_norm_w, rel_bias, w_branch_a, w_branch_b, w_out,
           norm_mix_w, norm_mlp_w, w_up, w_down, norm_final_w):
    batch, seq, d = x.shape
    assert d == D_MODEL and seq % ATT_TQ == 0 and w_in.shape[0] == 1
    x2 = x.reshape(batch * seq, d)
    z, hf, (wa_bf, wb_bf, wo_bf) = _in_proj(
        x2, norm_mix_w[0], w_in[0], (w_branch_a[0], w_branch_b[0], w_out[0]))
    ya, yb, wu_bf, wd_bf = _mixers(z, hf, lb_logits, hg_norm_w[0], rel_bias[0], w_up[0],
                                   w_down[0], batch, seq)
    h, u = _merge_out(ya, yb, z, x2, wa_bf, wb_bf, wo_bf, norm_mlp_w[0])
    out = _mlp(h, u, wu_bf, wd_bf, norm_final_w)
    return out.reshape(batch, seq, d)
```

```python
import functools

import jax
import jax.numpy as jnp
import numpy as np
from jax import lax
from jax.experimental import pallas as pl
from jax.experimental.pallas import tpu as pltpu

D_MODEL = 2048
CHUNK = 64
HG_HEADS = 8
HG_DK = 128
HG_DV = 128
HG_WIDTH = HG_HEADS * HG_DV
AT_HEADS = 16
AT_DH = 64
AT_WIDTH = AT_HEADS * AT_DH
LEFT_CHUNKS = 8
REL_CLIP = 256
EPS = 1e-6

LANES = 128
MXU_TILE = 256
N_LEVELS = 6
ATT_TQ = 2 * CHUNK

BF16 = jnp.bfloat16
F32 = jnp.float32
VMEM_LIMIT = 60000 * 1024

IN_TM, IN_TN = 2048, 512
MIX_TB = 4 * CHUNK
MERGE_TM, MERGE_RC = 512, 256
MLP_TM, MLP_TF = 1024, 1024
MLP_VMEM_LIMIT = 62 * 1024 * 1024

NN = (((1,), (0,)), ((), ()))
NT = (((1,), (1,)), ((), ()))
TN = (((0,), (0,)), ((), ()))


def _dot(a, b, dims=NN):
    return lax.dot_general(a, b, dims, preferred_element_type=F32)


def _rms(xf, w):
    return xf * lax.rsqrt(jnp.mean(xf * xf, axis=-1, keepdims=True) + EPS) * w


CAST_BLOCKS = 64
IN_SLAB = MXU_TILE
LOG2E = 1.4426950408889634
AT_QSCALE = AT_DH ** -0.5 * LOG2E
AQ_COLS = (4 * HG_WIDTH, 4 * HG_WIDTH + AT_WIDTH)
ZF_WIDTH = HG_WIDTH + IN_TN


def _in_proj_kernel(x_ref, nw_ref, w_ref, *rest, n_cast):
    cast_in, (z_ref, zf_ref) = rest[:n_cast], rest[n_cast:n_cast + 2]
    cast_out, (u_ref, xbuf_ref, x_sem) = rest[n_cast + 2:2 * n_cast + 2], rest[2 * n_cast + 2:]
    tm, tn = z_ref.shape
    i, j = pl.program_id(0), pl.program_id(1)

    def x_copy(tile):
        return pltpu.make_async_copy(x_ref.at[pl.ds(tile * tm, tm), :], xbuf_ref, x_sem)

    @pl.when((i == 0) & (j == 0))
    def _():
        x_copy(0).start()

    @pl.when(j == 0)
    def _():
        x_copy(i).wait()
        u_ref[...] = _rms(xbuf_ref[...], nw_ref[...]).astype(BF16)

    @pl.when((j == 1) & (i + 1 < pl.num_programs(0)))
    def _():
        x_copy(i + 1).start()

    col0 = pl.program_id(1) * tn
    zscale = jnp.where((col0 >= AQ_COLS[0]) & (col0 < AQ_COLS[1]), AT_QSCALE, 1.0).astype(F32)
    for c in range(tn // IN_SLAB):
        cols = slice(c * IN_SLAB, (c + 1) * IN_SLAB)
        acc = _dot(u_ref[...], w_ref[:, cols].astype(BF16))
        z_ref[:, cols] = (acc * zscale).astype(z_ref.dtype)
        zf_ref[:, cols] = acc

    for src, dst in zip(cast_in, cast_out):
        dst[...] = src[...].astype(BF16)


def _in_proj(x2, norm_w, w, later_weights, *, tm=IN_TM, tn=IN_TN):
    m, k = x2.shape
    n = w.shape[1]
    nj = n // tn
    assert (m // tm) * nj >= CAST_BLOCKS and HG_WIDTH % tn == 0
    assert AQ_COLS[0] % tn == 0 and AQ_COLS[1] % tn == 0
    hf_first, n_hf = HG_WIDTH // tn, HG_WIDTH // tn

    def cast_spec(wl):
        rows, cols = wl.shape
        return pl.BlockSpec((rows // CAST_BLOCKS, cols),
                            lambda i, j: (jnp.minimum(i * nj + j, CAST_BLOCKS - 1), 0))

    cast_specs = [cast_spec(wl) for wl in later_weights]
    outs = pl.pallas_call(
        functools.partial(_in_proj_kernel, n_cast=len(later_weights)),
        out_shape=(jax.ShapeDtypeStruct((m, n), BF16),
                   jax.ShapeDtypeStruct((m, ZF_WIDTH), F32),
                   *[jax.ShapeDtypeStruct(wl.shape, BF16) for wl in later_weights]),
        grid=(m // tm, nj),
        in_specs=[
            pl.BlockSpec(memory_space=pl.ANY),
            pl.BlockSpec((1, k), lambda i, j: (0, 0)),
            pl.BlockSpec((k, tn), lambda i, j: (0, j)),
            *cast_specs,
        ],
        out_specs=(
            pl.BlockSpec((tm, tn), lambda i, j: (i, j)),
            pl.BlockSpec((tm, tn), lambda i, j: (i, jnp.clip(j - hf_first, 0, n_hf))),
            *cast_specs,
        ),
        scratch_shapes=[pltpu.VMEM((tm, k), BF16), pltpu.VMEM((tm, k), F32), pltpu.SemaphoreType.DMA],
        compiler_params=pltpu.CompilerParams(
            dimension_semantics=("arbitrary", "arbitrary"),
            vmem_limit_bytes=VMEM_LIMIT),
        name="in_proj",
    )(x2, norm_w.reshape(1, k), w, *later_weights)
    return outs[0], outs[1], outs[2:]


MXU_LEVELS = 3


def _hgrn_tables():
    t = np.arange(CHUNK)[:, None]
    s = np.arange(CHUNK)[None, :]
    mats, masks = [], []
    for l in range(N_LEVELS):
        h = 1 << l
        start = (t // (2 * h)) * (2 * h)
        ref = start + h - 1
        is_q = (t - start) >= h
        if l < MXU_LEVELS:
            mats.append(np.where(is_q, (s > ref) & (s <= t), (s > t) & (s <= ref)))
        s_start = (s // (2 * h)) * (2 * h)
        masks.append((start == s_start) & is_q & ((s - s_start) < h))
    mats.append(s <= t)
    masks.append(t == s)
    w = np.concatenate(mats, axis=0).astype(np.float32)
    w2 = np.concatenate([w, w], axis=1)
    return w2, np.stack(masks).astype(np.float32)


def _level_exponent(b, h):
    parts = []
    for p in range(b.shape[0] // (2 * h)):
        lo = p * 2 * h
        ref = b[lo + h - 1:lo + h, :]
        parts += [ref - b[lo:lo + h, :], b[lo + h:lo + 2 * h, :] - ref]
    return jnp.concatenate(parts, axis=0)


N_OPS = N_LEVELS + 2
OP_DIAG, OP_STATE = N_LEVELS, N_LEVELS + 1


def _hgrn_units(hq_ref, hf_ref, hi_ref, hg_ref, lbl_ref, nw_ref, wexp_ref, mask_ref, y_ref,
                st_ref, qx_ref, kx_ref, d_ref, *, first_block, n_chunks):
    @pl.when(first_block)
    def _():
        st_ref[...] = jnp.zeros_like(st_ref)

    logits = lbl_ref[...]
    l0, l1 = logits[0:1, :], logits[1:2, :]
    mx = jnp.maximum(l0, l1)
    e0, e1 = jnp.exp(l0 - mx), jnp.exp(l1 - mx)
    lb = e0 / (e0 + e1)
    nw = nw_ref[...]
    wexp = wexp_ref[...]

    def stage(c):
        slot = c % 2
        r = slice(c * CHUNK, (c + 1) * CHUNK)
        f = jax.nn.sigmoid(hf_ref[r, :])
        g = lb + (1.0 - lb) * f
        lg = jnp.log2(g)
        kk = (1.0 - g).astype(BF16)
        q = (jax.nn.silu(hq_ref[r, :].astype(F32)) * (HG_DK ** -0.5)).astype(BF16)
        lg_hi = lg.astype(BF16)
        lg_lo = (lg - lg_hi.astype(F32)).astype(BF16)
        e_mxu = _dot(wexp, jnp.concatenate([lg_hi, lg_lo], axis=0))
        b = e_mxu[MXU_LEVELS * CHUNK:, :]

        def put(ref, op, val):
            for hh in range(HG_HEADS):
                ref[slot, op, hh] = val[:, hh * HG_DK:(hh + 1) * HG_DK]

        put(qx_ref, OP_DIAG, q)
        put(kx_ref, OP_DIAG, kk)
        for l in range(N_LEVELS):
            e = e_mxu[l * CHUNK:(l + 1) * CHUNK, :] if l < MXU_LEVELS else _level_exponent(b, 1 << l)
            x = jnp.exp2(e).astype(BF16)
            put(qx_ref, l, q * x)
            put(kx_ref, l, kk * x)
        eb = jnp.exp2(b)
        d_ref[slot] = eb[CHUNK - 1:CHUNK, :]
        put(qx_ref, OP_STATE, q * eb.astype(BF16))
        put(kx_ref, OP_STATE, kk * jnp.exp2(b[CHUNK - 1:CHUNK, :] - b).astype(BF16))

    def scores(c, h):
        slot = c % 2
        sc = mask_ref[OP_DIAG] * _dot(qx_ref[slot, OP_DIAG, h], kx_ref[slot, OP_DIAG, h], NT)
        for l in range(N_LEVELS):
            sc = sc + mask_ref[l] * _dot(qx_ref[slot, l, h], kx_ref[slot, l, h], NT)
        return sc.astype(BF16)

    def outputs(c, h, sc):
        slot, sl = c % 2, slice(h * HG_DK, (h + 1) * HG_DK)
        r = slice(c * CHUNK, (c + 1) * CHUNK)
        vh = hi_ref[r, sl]
        st = st_ref[h]
        o = _dot(sc, vh) + _dot(qx_ref[slot, OP_STATE, h], st.astype(BF16), NT)
        st_ref[h] = d_ref[slot, :, sl] * st + _dot(vh, kx_ref[slot, OP_STATE, h], TN)
        y = _rms(o, nw) * jax.nn.silu(hg_ref[r, sl].astype(F32))
        y_ref[r, sl] = y.astype(y_ref.dtype)

    stage(0)
    yield
    for c in range(n_chunks):
        sc_next = scores(c, 0)
        for h in range(HG_HEADS):
            sc = sc_next
            if h + 1 < HG_HEADS:
                sc_next = scores(c, h + 1)
            if h == 0 and c + 1 < n_chunks:
                stage(c + 1)
            outputs(c, h, sc)
            yield


ATT_TK = ATT_TQ + LEFT_CHUNKS * CHUNK
ATT_ROLL = 1024
assert ATT_TQ - 1 + ATT_TK <= ATT_ROLL and ATT_TK - ATT_TQ == 2 * REL_CLIP


def _rel_table(rel_bias):
    rev = rel_bias[:, ::-1].astype(F32) * LOG2E
    edge = jnp.broadcast_to(rev[:, :1], (rel_bias.shape[0], REL_CLIP))
    g = jnp.concatenate([edge, rev[:, :2 * REL_CLIP], edge], axis=1)
    return g.reshape(AT_HEADS // 2, 2, ATT_ROLL)


ATT_PAD = ATT_TK - ATT_TQ
ATT_EDGE = ATT_PAD // ATT_TQ
ATT_RG = 32


def _attn_units(q_ref, k_ref, v_ref, g_ref, o_ref, bias_ref, kp_ref, vp_ref, s_ref, p_ref,
                *, new_head_pair, seq):
    @pl.when(new_head_pair)
    def _():
        qc = lax.broadcasted_iota(jnp.int32, (ATT_TQ, ATT_TK), 0) // CHUNK
        kc = lax.broadcasted_iota(jnp.int32, (ATT_TQ, ATT_TK), 1) // CHUNK
        in_band = (kc >= qc) & (kc <= qc + LEFT_CHUNKS)
        g = g_ref[0]
        for hh in range(2):
            tbl = jnp.broadcast_to(g[hh:hh + 1, :], (ATT_TQ, ATT_ROLL))
            toep = pltpu.roll(tbl, 0, 1, stride=1, stride_axis=0)[:, :ATT_TK]
            rows = slice(hh * ATT_TQ, (hh + 1) * ATT_TQ)
            for n in range(ATT_EDGE + 1):
                first_chunk = (ATT_PAD - n * ATT_TQ) // CHUNK if n < ATT_EDGE else 0
                bias_ref[n, rows, :] = jnp.where(in_band & (kc >= first_chunk), toep, -jnp.inf)

    v_all = v_ref[...]
    head0_all = lax.broadcasted_iota(jnp.int32, v_all.shape, 1) < AT_DH
    ones = jnp.ones_like(v_all)
    pad = jnp.zeros((ATT_PAD, LANES), BF16)
    kp_ref[:ATT_PAD, :] = pad
    kp_ref[ATT_PAD:, :] = k_ref[...]
    vp_ref[0, :ATT_PAD, :] = pad
    vp_ref[1, :ATT_PAD, :] = pad
    vp_ref[0, ATT_PAD:, :] = jnp.where(head0_all, v_all, ones)
    vp_ref[1, ATT_PAD:, :] = jnp.where(head0_all, ones, v_all)

    head0 = lax.broadcasted_iota(jnp.int32, (ATT_TQ, LANES), 1) < AT_DH
    n_blocks = seq // ATT_TQ
    yield

    def scores(i):
        slot = i % 2
        q = q_ref[i * ATT_TQ:(i + 1) * ATT_TQ, :]
        zero = jnp.zeros_like(q)
        qq = jnp.concatenate([jnp.where(head0, q, zero), jnp.where(head0, zero, q)], axis=0)
        s = _dot(qq, kp_ref[i * ATT_TQ:i * ATT_TQ + ATT_TK, :], NT)
        s_ref[slot] = s + bias_ref[min(i, ATT_EDGE)]

    def outputs(i):
        slot = i % 2
        for g in range(2 * ATT_TQ // ATT_RG):
            rows = slice(g * ATT_RG, (g + 1) * ATT_RG)
            s = s_ref[slot, rows, :]
            p_ref[slot, rows, :] = jnp.exp2(s - jnp.max(s, axis=-1, keepdims=True)).astype(BF16)
        keys = slice(i * ATT_TQ, i * ATT_TQ + ATT_TK)
        acc0 = _dot(p_ref[slot, :ATT_TQ, :], vp_ref[0, keys, :])
        acc1 = _dot(p_ref[slot, ATT_TQ:, :], vp_ref[1, keys, :])
        num = jnp.where(head0, acc0, acc1)
        den = pltpu.roll(jnp.where(head0, acc1, acc0), AT_DH, 1)
        o_ref[i * ATT_TQ:(i + 1) * ATT_TQ, :] = (num / den).astype(o_ref.dtype)

    scores(0)
    for i in range(n_blocks):
        if i + 1 < n_blocks:
            scores(i + 1)
        outputs(i)
        yield


def _mixers_kernel(hq_ref, hf_ref, hi_ref, hg_ref, lbl_ref, nw_ref, wexp_ref, mask_ref,
                   q_ref, k_ref, v_ref, g_ref, cast_a_ref, cast_b_ref,
                   y_ref, o_ref, cast_a_out, cast_b_out,
                   st_ref, qx_ref, kx_ref, d_ref, bias_ref, kp_ref, vp_ref, s_ref, p_ref,
                   *, n_chunks, seq, steps_per_pair):
    cast_a_out[...] = cast_a_ref[...].astype(BF16)
    cast_b_out[...] = cast_b_ref[...].astype(BF16)
    step = pl.program_id(0) * pl.num_programs(1) + pl.program_id(1)
    hgrn = _hgrn_units(hq_ref, hf_ref, hi_ref, hg_ref, lbl_ref, nw_ref, wexp_ref, mask_ref,
                       y_ref, st_ref, qx_ref, kx_ref, d_ref,
                       first_block=pl.program_id(1) == 0, n_chunks=n_chunks)
    attn = _attn_units(q_ref, k_ref, v_ref, g_ref, o_ref, bias_ref, kp_ref, vp_ref, s_ref, p_ref,
                       new_head_pair=step % steps_per_pair == 0, seq=seq)
    n_hgrn, n_attn = n_chunks * HG_HEADS, seq // ATT_TQ
    assert n_hgrn % n_attn == 0
    next(hgrn)
    next(attn)
    for _ in range(n_attn):
        next(attn)
        for _ in range(n_hgrn // n_attn):
            next(hgrn)
    assert next(attn, None) is None and next(hgrn, None) is None


def _mixers(z, hf, lb_logits, hg_norm_w, rel_bias, cast_a, cast_b, batch, seq, *, tb=MIX_TB):
    wexp, masks = _hgrn_tables()
    g = _rel_table(rel_bias)
    nblk = seq // tb
    steps = batch * nblk
    assert steps == AT_HEADS // 2 * batch
    col0 = 4 * HG_WIDTH // LANES
    ncol = AT_WIDTH // LANES

    def step_of(b, i):
        return b * nblk + i

    def hspec(col):
        return pl.BlockSpec((tb, HG_WIDTH), lambda b, i, col=col: (b * nblk + i, col))

    def aspec(which):
        return pl.BlockSpec((seq, LANES), lambda b, i, which=which: (
            step_of(b, i) % batch, col0 + which * ncol + step_of(b, i) // batch))

    def const(shape):
        return pl.BlockSpec(shape, lambda b, i: (0,) * len(shape))

    def cast_spec(w):
        return pl.BlockSpec((w.shape[0] // steps, w.shape[1]), lambda b, i: (step_of(b, i), 0))

    return pl.pallas_call(
        functools.partial(_mixers_kernel, n_chunks=tb // CHUNK, seq=seq, steps_per_pair=batch),
        out_shape=(jax.ShapeDtypeStruct((batch * seq, HG_WIDTH), BF16),
                   jax.ShapeDtypeStruct((batch * seq, AT_WIDTH), BF16),
                   jax.ShapeDtypeStruct(cast_a.shape, BF16),
                   jax.ShapeDtypeStruct(cast_b.shape, BF16)),
        grid=(batch, nblk),
        in_specs=[
            hspec(0), hspec(0), hspec(2), hspec(3),
            const((2, HG_WIDTH)), const((1, HG_DV)), const(wexp.shape), const(masks.shape),
            aspec(0), aspec(1), aspec(2),
            pl.BlockSpec((1, 2, ATT_ROLL), lambda b, i: (step_of(b, i) // batch, 0, 0)),
            cast_spec(cast_a), cast_spec(cast_b),
        ],
        out_specs=(
            pl.BlockSpec((tb, HG_WIDTH), lambda b, i: (b * nblk + i, 0)),
            pl.BlockSpec((seq, LANES), lambda b, i: (step_of(b, i) % batch, step_of(b, i) // batch)),
            cast_spec(cast_a), cast_spec(cast_b),
        ),
        scratch_shapes=[pltpu.VMEM((HG_HEADS, HG_DV, HG_DK), F32),
                        pltpu.VMEM((2, N_OPS, HG_HEADS, CHUNK, HG_DK), BF16),
                        pltpu.VMEM((2, N_OPS, HG_HEADS, CHUNK, HG_DK), BF16),
                        pltpu.VMEM((2, 1, HG_WIDTH), F32),
                        pltpu.VMEM((ATT_EDGE + 1, 2 * ATT_TQ, ATT_TK), F32),
                        pltpu.VMEM((ATT_PAD + seq, LANES), BF16),
                        pltpu.VMEM((2, ATT_PAD + seq, LANES), BF16),
                        pltpu.VMEM((2, 2 * ATT_TQ, ATT_TK), F32),
                        pltpu.VMEM((2, 2 * ATT_TQ, ATT_TK), BF16)],
        compiler_params=pltpu.CompilerParams(
            dimension_semantics=("arbitrary", "arbitrary"),
            vmem_limit_bytes=VMEM_LIMIT),
        name="mixers",
    )(z, hf, z, z, lb_logits, hg_norm_w.reshape(1, HG_DV), jnp.asarray(wexp, BF16), jnp.asarray(masks),
      z, z, z, g, cast_a, cast_b)


def _merge_out_kernel(ya_ref, yb_ref, ga0_ref, ga1_ref, gb0_ref, gb1_ref, x_ref,
                      wa_ref, wb_ref, wo_ref, nw_ref, h_ref, u_ref, *, rc):
    for c in range(h_ref.shape[0] // rc):
        r = slice(c * rc, (c + 1) * rc)
        pa = _dot(ya_ref[r, :], wa_ref[...])
        pb = _dot(yb_ref[r, :], wb_ref[...])
        ga = jnp.concatenate([ga0_ref[r, :], ga1_ref[r, :]], axis=1).astype(F32)
        gb = jnp.concatenate([gb0_ref[r, :], gb1_ref[r, :]], axis=1).astype(F32)
        merged = (jax.nn.sigmoid(ga) * pa + jax.nn.sigmoid(gb) * pb).astype(BF16)
        h = x_ref[r, :] + _dot(merged, wo_ref[...])
        h_ref[r, :] = h
        u_ref[r, :] = _rms(h, nw_ref[...]).astype(BF16)


def _merge_out(ya, yb, z, x2, wa_bf, wb_bf, wo_bf, norm_w, *, tm=MERGE_TM, rc=MERGE_RC):
    m, d = x2.shape
    gw = d // 2
    gcol = (4 * HG_WIDTH + 3 * AT_WIDTH) // gw
    assert gcol * gw == 4 * HG_WIDTH + 3 * AT_WIDTH

    def resident(shape):
        return pl.BlockSpec(shape, lambda i: (0, 0), pipeline_mode=pl.Buffered(1))

    def gate(blk):
        return pl.BlockSpec((tm, gw), lambda i, blk=blk: (i, gcol + blk))

    return pl.pallas_call(
        functools.partial(_merge_out_kernel, rc=rc),
        out_shape=(jax.ShapeDtypeStruct((m, d), F32), jax.ShapeDtypeStruct((m, d), BF16)),
        grid=(m // tm,),
        in_specs=[
            pl.BlockSpec((tm, HG_WIDTH), lambda i: (i, 0)),
            pl.BlockSpec((tm, AT_WIDTH), lambda i: (i, 0)),
            gate(0), gate(1), gate(2), gate(3),
            pl.BlockSpec((tm, d), lambda i: (i, 0)),
            resident(wa_bf.shape), resident(wb_bf.shape), resident(wo_bf.shape),
            resident((1, d)),
        ],
        out_specs=(pl.BlockSpec((tm, d), lambda i: (i, 0)), pl.BlockSpec((tm, d), lambda i: (i, 0))),
        compiler_params=pltpu.CompilerParams(
            dimension_semantics=("parallel",),
            vmem_limit_bytes=VMEM_LIMIT),
        name="merge_out",
    )(ya, yb, z, z, z, z, x2, wa_bf, wb_bf, wo_bf, norm_w.reshape(1, d))


MLP_SLAB = 2 * MXU_TILE


def _mlp_kernel(h_ref, u_ref, wu_ref, wd_ref, fw_ref, o_ref, hbuf_ref, h_sem):
    i, f = pl.program_id(0), pl.program_id(1)
    tm = o_ref.shape[0]

    def h_copy():
        return pltpu.make_async_copy(h_ref.at[pl.ds(i * tm, tm), :], hbuf_ref, h_sem)

    @pl.when(f == 0)
    def _():
        h_copy().start()
        o_ref[...] = jnp.zeros_like(o_ref)

    a = []
    for c in range(wu_ref.shape[1] // MLP_SLAB):
        cols = slice(c * MLP_SLAB, (c + 1) * MLP_SLAB)
        t = jnp.maximum(_dot(u_ref[...], wu_ref[:, cols]), 0.0)
        a.append((t * t).astype(BF16))
    a = jnp.concatenate(a, axis=1)
    for c in range(o_ref.shape[1] // MLP_SLAB):
        cols = slice(c * MLP_SLAB, (c + 1) * MLP_SLAB)
        o_ref[:, cols] += _dot(a, wd_ref[:, cols])

    @pl.when(f == pl.num_programs(1) - 1)
    def _():
        h_copy().wait()
        o_ref[...] = _rms(hbuf_ref[...] + o_ref[...], fw_ref[...])


def _mlp(h, u, wu_bf, wd_bf, final_w, *, tm=MLP_TM, tf=MLP_TF):
    m, d = h.shape
    dff = wu_bf.shape[1]
    return pl.pallas_call(
        _mlp_kernel,
        out_shape=jax.ShapeDtypeStruct((m, d), F32),
        grid=(m // tm, dff // tf),
        in_specs=[
            pl.BlockSpec(memory_space=pl.ANY),
            pl.BlockSpec((tm, d), lambda i, f: (i, 0)),
            pl.BlockSpec((d, tf), lambda i, f: (0, f)),
            pl.BlockSpec((tf, d), lambda i, f: (f, 0)),
            pl.BlockSpec((1, d), lambda i, f: (0, 0)),
        ],
        out_specs=pl.BlockSpec((tm, d), lambda i, f: (i, 0)),
        scratch_shapes=[pltpu.VMEM((tm, d), F32), pltpu.SemaphoreType.DMA],
        compiler_params=pltpu.CompilerParams(
            dimension_semantics=("arbitrary", "arbitrary"),
            vmem_limit_bytes=MLP_VMEM_LIMIT),
        name="mlp",
    )(h, u, wu_bf, wd_bf, final_w.reshape(1, d))


def kernel(x, w_in, lb_logits, hg_norm_w, rel_bias, w_branch_a, w_branch_b, w_out,
           norm_mix_w, norm_mlp_w, w_up, w_down, norm_final_w):
    batch, seq, d = x.shape
    assert d == D_MODEL and seq % ATT_TQ == 0 and w_in.shape[0] == 1
    x2 = x.reshape(batch * seq, d)
    z, hf, (wa_bf, wb_bf, wo_bf) = _in_proj(
        x2, norm_mix_w[0], w_in[0], (w_branch_a[0], w_branch_b[0], w_out[0]))
    ya, yb, wu_bf, wd_bf = _mixers(z, hf, lb_logits, hg_norm_w[0], rel_bias[0], w_up[0],
                                   w_down[0], batch, seq)
    h, u = _merge_out(ya, yb, z, x2, wa_bf, wb_bf, wo_bf, norm_mlp_w[0])
    out = _mlp(h, u, wu_bf, wd_bf, norm_final_w)
    return out.reshape(batch, seq, d)
```

```python
import functools

import jax
import jax.numpy as jnp
import numpy as np
from jax import lax
from jax.experimental import pallas as pl
from jax.experimental.pallas import tpu as pltpu

D_MODEL = 2048
CHUNK = 64
HG_HEADS = 8
HG_DK = 128
HG_DV = 128
HG_WIDTH = HG_HEADS * HG_DV
AT_HEADS = 16
AT_DH = 64
AT_WIDTH = AT_HEADS * AT_DH
LEFT_CHUNKS = 8
REL_CLIP = 256
EPS = 1e-6

LANES = 128
MXU_TILE = 256
N_LEVELS = 6
ATT_TQ = 2 * CHUNK

BF16 = jnp.bfloat16
F32 = jnp.float32
VMEM_LIMIT = 60000 * 1024

IN_TM, IN_TN = 2048, 1024
MIX_TB = 4 * CHUNK
MERGE_TM, MERGE_RC = 512, 256
MLP_TM, MLP_TF = 1024, 1024
BIG_VMEM_LIMIT = 62 * 1024 * 1024

NN = (((1,), (0,)), ((), ()))
NT = (((1,), (1,)), ((), ()))
TN = (((0,), (0,)), ((), ()))


def _dot(a, b, dims=NN):
    return lax.dot_general(a, b, dims, preferred_element_type=F32)


def _rms(xf, w):
    return xf * lax.rsqrt(jnp.mean(xf * xf, axis=-1, keepdims=True) + EPS) * w


IN_SLAB = MXU_TILE
LOG2E = 1.4426950408889634
AT_QSCALE = AT_DH ** -0.5 * LOG2E
AQ_COLS = (4 * HG_WIDTH, 4 * HG_WIDTH + AT_WIDTH)
HF_TILE = 1


def _in_col_tile(j, nj):
    return jnp.where(j == nj - 1, HF_TILE, jnp.where(j < HF_TILE, j, j + 1))


def _in_proj_kernel(x_ref, nw_ref, w_ref, z_ref, zf_ref, u_ref, xbuf_ref, accf_ref, x_sem, zf_sem):
    tm, tn = z_ref.shape
    i, j = pl.program_id(0), pl.program_id(1)
    ni, nj = pl.num_programs(0), pl.num_programs(1)

    def x_copy(tile):
        return pltpu.make_async_copy(x_ref.at[pl.ds(tile * tm, tm), :], xbuf_ref, x_sem)

    def zf_copy(tile):
        return pltpu.make_async_copy(accf_ref, zf_ref.at[pl.ds(tile * tm, tm), :], zf_sem)

    @pl.when((i == 0) & (j == 0))
    def _():
        x_copy(0).start()

    @pl.when(j == 0)
    def _():
        x_copy(i).wait()
        u_ref[...] = _rms(xbuf_ref[...], nw_ref[...]).astype(BF16)

        @pl.when(i > 0)
        def _():
            zf_copy(i - 1).wait()

    @pl.when((j == 1) & (i + 1 < ni))
    def _():
        x_copy(i + 1).start()

    col0 = _in_col_tile(j, nj) * tn
    zscale = jnp.where((col0 >= AQ_COLS[0]) & (col0 < AQ_COLS[1]), AT_QSCALE, 1.0).astype(F32)
    for c in range(tn // IN_SLAB):
        cols = slice(c * IN_SLAB, (c + 1) * IN_SLAB)
        acc = _dot(u_ref[...], w_ref[:, cols].astype(BF16))
        z_ref[:, cols] = (acc * zscale).astype(z_ref.dtype)
        accf_ref[:, cols] = acc

    @pl.when(j == nj - 1)
    def _():
        zf_copy(i).start()

        @pl.when(i == ni - 1)
        def _():
            zf_copy(i).wait()


def _in_proj(x2, norm_w, w, *, tm=IN_TM, tn=IN_TN):
    m, k = x2.shape
    n = w.shape[1]
    nj = n // tn
    assert tn == HG_WIDTH
    assert AQ_COLS[0] % tn == 0 and AQ_COLS[1] % tn == 0
    return pl.pallas_call(
        _in_proj_kernel,
        out_shape=(jax.ShapeDtypeStruct((m, n), BF16), jax.ShapeDtypeStruct((m, HG_WIDTH), F32)),
        grid=(m // tm, nj),
        in_specs=[
            pl.BlockSpec(memory_space=pl.ANY),
            pl.BlockSpec((1, k), lambda i, j: (0, 0)),
            pl.BlockSpec((k, tn), lambda i, j: (0, _in_col_tile(j, nj))),
        ],
        out_specs=(
            pl.BlockSpec((tm, tn), lambda i, j: (i, _in_col_tile(j, nj))),
            pl.BlockSpec(memory_space=pl.ANY),
        ),
        scratch_shapes=[pltpu.VMEM((tm, k), BF16), pltpu.VMEM((tm, k), F32), pltpu.VMEM((tm, tn), F32),
                        pltpu.SemaphoreType.DMA, pltpu.SemaphoreType.DMA],
        compiler_params=pltpu.CompilerParams(
            dimension_semantics=("arbitrary", "arbitrary"),
            vmem_limit_bytes=BIG_VMEM_LIMIT),
        name="in_proj",
    )(x2, norm_w.reshape(1, k), w)


MXU_LEVELS = 3


def _hgrn_tables():
    t = np.arange(CHUNK)[:, None]
    s = np.arange(CHUNK)[None, :]
    mats, masks = [], []
    for l in range(N_LEVELS):
        h = 1 << l
        start = (t // (2 * h)) * (2 * h)
        ref = start + h - 1
        is_q = (t - start) >= h
        if l < MXU_LEVELS:
            mats.append(np.where(is_q, (s > ref) & (s <= t), (s > t) & (s <= ref)))
        s_start = (s // (2 * h)) * (2 * h)
        masks.append((start == s_start) & is_q & ((s - s_start) < h))
    mats.append(s <= t)
    masks.append(t == s)
    w = np.concatenate(mats, axis=0).astype(np.float32)
    w2 = np.concatenate([w, w], axis=1)
    return w2, np.stack(masks).astype(np.float32)


def _level_exponent(b, h):
    parts = []
    for p in range(b.shape[0] // (2 * h)):
        lo = p * 2 * h
        ref = b[lo + h - 1:lo + h, :]
        parts += [ref - b[lo:lo + h, :], b[lo + h:lo + 2 * h, :] - ref]
    return jnp.concatenate(parts, axis=0)


N_OPS = N_LEVELS + 2
OP_DIAG, OP_STATE = N_LEVELS, N_LEVELS + 1


def _hgrn_units(hq_ref, hf_ref, hi_ref, hg_ref, lbl_ref, nw_ref, wexp_ref, mask_ref, y_ref,
                st_ref, qx_ref, kx_ref, d_ref, *, first_block, n_chunks):
    @pl.when(first_block)
    def _():
        st_ref[...] = jnp.zeros_like(st_ref)

    logits = lbl_ref[...]
    l0, l1 = logits[0:1, :], logits[1:2, :]
    mx = jnp.maximum(l0, l1)
    e0, e1 = jnp.exp(l0 - mx), jnp.exp(l1 - mx)
    lb = e0 / (e0 + e1)
    nw = nw_ref[...]
    wexp = wexp_ref[...]

    def stage(c):
        slot = c % 2
        r = slice(c * CHUNK, (c + 1) * CHUNK)
        f = jax.nn.sigmoid(hf_ref[r, :])
        g = lb + (1.0 - lb) * f
        lg = jnp.log2(g)
        kk = (1.0 - g).astype(BF16)
        q = (jax.nn.silu(hq_ref[r, :].astype(F32)) * (HG_DK ** -0.5)).astype(BF16)
        lg_hi = lg.astype(BF16)
        lg_lo = (lg - lg_hi.astype(F32)).astype(BF16)
        e_mxu = _dot(wexp, jnp.concatenate([lg_hi, lg_lo], axis=0))
        b = e_mxu[MXU_LEVELS * CHUNK:, :]

        def put(ref, op, val):
            for hh in range(HG_HEADS):
                ref[slot, op, hh] = val[:, hh * HG_DK:(hh + 1) * HG_DK]

        put(qx_ref, OP_DIAG, q)
        put(kx_ref, OP_DIAG, kk)
        for l in range(N_LEVELS):
            e = e_mxu[l * CHUNK:(l + 1) * CHUNK, :] if l < MXU_LEVELS else _level_exponent(b, 1 << l)
            x = jnp.exp2(e).astype(BF16)
            put(qx_ref, l, q * x)
            put(kx_ref, l, kk * x)
        eb = jnp.exp2(b)
        d_ref[slot] = eb[CHUNK - 1:CHUNK, :]
        put(qx_ref, OP_STATE, q * eb.astype(BF16))
        put(kx_ref, OP_STATE, kk * jnp.exp2(b[CHUNK - 1:CHUNK, :] - b).astype(BF16))

    def scores(c, h):
        slot = c % 2
        sc = mask_ref[OP_DIAG] * _dot(qx_ref[slot, OP_DIAG, h], kx_ref[slot, OP_DIAG, h], NT)
        for l in range(N_LEVELS):
            sc = sc + mask_ref[l] * _dot(qx_ref[slot, l, h], kx_ref[slot, l, h], NT)
        return sc.astype(BF16)

    def outputs(c, h, sc):
        slot, sl = c % 2, slice(h * HG_DK, (h + 1) * HG_DK)
        r = slice(c * CHUNK, (c + 1) * CHUNK)
        vh = hi_ref[r, sl]
        st = st_ref[h]
        o = _dot(sc, vh) + _dot(qx_ref[slot, OP_STATE, h], st.astype(BF16), NT)
        st_ref[h] = d_ref[slot, :, sl] * st + _dot(vh, kx_ref[slot, OP_STATE, h], TN)
        y = _rms(o, nw) * jax.nn.silu(hg_ref[r, sl].astype(F32))
        y_ref[r, sl] = y.astype(y_ref.dtype)

    stage(0)
    yield
    for c in range(n_chunks):
        sc_next = scores(c, 0)
        for h in range(HG_HEADS):
            sc = sc_next
            if h + 1 < HG_HEADS:
                sc_next = scores(c, h + 1)
            if h == 0 and c + 1 < n_chunks:
                stage(c + 1)
            outputs(c, h, sc)
            yield


ATT_TK = ATT_TQ + LEFT_CHUNKS * CHUNK
ATT_ROLL = 1024
assert ATT_TQ - 1 + ATT_TK <= ATT_ROLL and ATT_TK - ATT_TQ == 2 * REL_CLIP


def _rel_table(rel_bias):
    rev = rel_bias[:, ::-1].astype(F32) * LOG2E
    edge = jnp.broadcast_to(rev[:, :1], (rel_bias.shape[0], REL_CLIP))
    g = jnp.concatenate([edge, rev[:, :2 * REL_CLIP], edge], axis=1)
    return g.reshape(AT_HEADS // 2, 2, ATT_ROLL)


ATT_PAD = ATT_TK - ATT_TQ
ATT_EDGE = ATT_PAD // ATT_TQ
ATT_RG = 32


def _attn_units(q_ref, k_ref, v_ref, g_ref, o_ref, bias_ref, kp_ref, vp_ref, s_ref, p_ref,
                *, new_head_pair, seq):
    @pl.when(new_head_pair)
    def _():
        qc = lax.broadcasted_iota(jnp.int32, (ATT_TQ, ATT_TK), 0) // CHUNK
        kc = lax.broadcasted_iota(jnp.int32, (ATT_TQ, ATT_TK), 1) // CHUNK
        in_band = (kc >= qc) & (kc <= qc + LEFT_CHUNKS)
        g = g_ref[0]
        for hh in range(2):
            tbl = jnp.broadcast_to(g[hh:hh + 1, :], (ATT_TQ, ATT_ROLL))
            toep = pltpu.roll(tbl, 0, 1, stride=1, stride_axis=0)[:, :ATT_TK]
            rows = slice(hh * ATT_TQ, (hh + 1) * ATT_TQ)
            for n in range(ATT_EDGE + 1):
                first_chunk = (ATT_PAD - n * ATT_TQ) // CHUNK if n < ATT_EDGE else 0
                bias_ref[n, rows, :] = jnp.where(in_band & (kc >= first_chunk), toep, -jnp.inf)

    v_all = v_ref[...]
    head0_all = lax.broadcasted_iota(jnp.int32, v_all.shape, 1) < AT_DH
    ones = jnp.ones_like(v_all)
    pad = jnp.zeros((ATT_PAD, LANES), BF16)
    kp_ref[:ATT_PAD, :] = pad
    kp_ref[ATT_PAD:, :] = k_ref[...]
    vp_ref[0, :ATT_PAD, :] = pad
    vp_ref[1, :ATT_PAD, :] = pad
    vp_ref[0, ATT_PAD:, :] = jnp.where(head0_all, v_all, ones)
    vp_ref[1, ATT_PAD:, :] = jnp.where(head0_all, ones, v_all)

    head0 = lax.broadcasted_iota(jnp.int32, (ATT_TQ, LANES), 1) < AT_DH
    n_blocks = seq // ATT_TQ
    yield

    def scores(i):
        slot = i % 2
        q = q_ref[i * ATT_TQ:(i + 1) * ATT_TQ, :]
        zero = jnp.zeros_like(q)
        qq = jnp.concatenate([jnp.where(head0, q, zero), jnp.where(head0, zero, q)], axis=0)
        s = _dot(qq, kp_ref[i * ATT_TQ:i * ATT_TQ + ATT_TK, :], NT)
        s_ref[slot] = s + bias_ref[min(i, ATT_EDGE)]

    def outputs(i):
        slot = i % 2
        for g in range(2 * ATT_TQ // ATT_RG):
            rows = slice(g * ATT_RG, (g + 1) * ATT_RG)
            s = s_ref[slot, rows, :]
            p_ref[slot, rows, :] = jnp.exp2(s - jnp.max(s, axis=-1, keepdims=True)).astype(BF16)
        keys = slice(i * ATT_TQ, i * ATT_TQ + ATT_TK)
        acc0 = _dot(p_ref[slot, :ATT_TQ, :], vp_ref[0, keys, :])
        acc1 = _dot(p_ref[slot, ATT_TQ:, :], vp_ref[1, keys, :])
        num = jnp.where(head0, acc0, acc1)
        den = pltpu.roll(jnp.where(head0, acc1, acc0), AT_DH, 1)
        o_ref[i * ATT_TQ:(i + 1) * ATT_TQ, :] = (num / den).astype(o_ref.dtype)

    scores(0)
    for i in range(n_blocks):
        if i + 1 < n_blocks:
            scores(i + 1)
        outputs(i)
        yield


def _mixers_kernel(hq_ref, hf_ref, hi_ref, hg_ref, lbl_ref, nw_ref, wexp_ref, mask_ref,
                   q_ref, k_ref, v_ref, g_ref, *rest, n_cast, n_chunks, seq, steps_per_pair):
    cast_in, (y_ref, o_ref), cast_out = rest[:n_cast], rest[n_cast:n_cast + 2], rest[n_cast + 2:2 * n_cast + 2]
    st_ref, qx_ref, kx_ref, d_ref, bias_ref, kp_ref, vp_ref, s_ref, p_ref = rest[2 * n_cast + 2:]
    for src, dst in zip(cast_in, cast_out):
        dst[...] = src[...].astype(BF16)
    step = pl.program_id(0) * pl.num_programs(1) + pl.program_id(1)
    hgrn = _hgrn_units(hq_ref, hf_ref, hi_ref, hg_ref, lbl_ref, nw_ref, wexp_ref, mask_ref,
                       y_ref, st_ref, qx_ref, kx_ref, d_ref,
                       first_block=pl.program_id(1) == 0, n_chunks=n_chunks)
    attn = _attn_units(q_ref, k_ref, v_ref, g_ref, o_ref, bias_ref, kp_ref, vp_ref, s_ref, p_ref,
                       new_head_pair=step % steps_per_pair == 0, seq=seq)
    n_hgrn, n_attn = n_chunks * HG_HEADS, seq // ATT_TQ
    assert n_hgrn % n_attn == 0
    next(hgrn)
    next(attn)
    for _ in range(n_attn):
        next(attn)
        for _ in range(n_hgrn // n_attn):
            next(hgrn)
    assert next(attn, None) is None and next(hgrn, None) is None


def _mixers(z, hf, lb_logits, hg_norm_w, rel_bias, later_weights, batch, seq, *, tb=MIX_TB):
    wexp, masks = _hgrn_tables()
    g = _rel_table(rel_bias)
    nblk = seq // tb
    steps = batch * nblk
    assert steps == AT_HEADS // 2 * batch
    col0 = 4 * HG_WIDTH // LANES
    ncol = AT_WIDTH // LANES

    def step_of(b, i):
        return b * nblk + i

    def hspec(col):
        return pl.BlockSpec((tb, HG_WIDTH), lambda b, i, col=col: (b * nblk + i, col))

    def aspec(which):
        return pl.BlockSpec((seq, LANES), lambda b, i, which=which: (
            step_of(b, i) % batch, col0 + which * ncol + step_of(b, i) // batch))

    def const(shape):
        return pl.BlockSpec(shape, lambda b, i: (0,) * len(shape))

    def cast_spec(w):
        return pl.BlockSpec((w.shape[0] // steps, w.shape[1]), lambda b, i: (step_of(b, i), 0))

    cast_specs = [cast_spec(w) for w in later_weights]
    outs = pl.pallas_call(
        functools.partial(_mixers_kernel, n_cast=len(later_weights), n_chunks=tb // CHUNK, seq=seq,
                          steps_per_pair=batch),
        out_shape=(jax.ShapeDtypeStruct((batch * seq, HG_WIDTH), BF16),
                   jax.ShapeDtypeStruct((batch * seq, AT_WIDTH), BF16),
                   *[jax.ShapeDtypeStruct(w.shape, BF16) for w in later_weights]),
        grid=(batch, nblk),
        in_specs=[
            hspec(0), hspec(0), hspec(2), hspec(3),
            const((2, HG_WIDTH)), const((1, HG_DV)), const(wexp.shape), const(masks.shape),
            aspec(0), aspec(1), aspec(2),
            pl.BlockSpec((1, 2, ATT_ROLL), lambda b, i: (step_of(b, i) // batch, 0, 0)),
            *cast_specs,
        ],
        out_specs=(
            pl.BlockSpec((tb, HG_WIDTH), lambda b, i: (b * nblk + i, 0)),
            pl.BlockSpec((seq, LANES), lambda b, i: (step_of(b, i) % batch, step_of(b, i) // batch)),
            *cast_specs,
        ),
        scratch_shapes=[pltpu.VMEM((HG_HEADS, HG_DV, HG_DK), F32),
                        pltpu.VMEM((2, N_OPS, HG_HEADS, CHUNK, HG_DK), BF16),
                        pltpu.VMEM((2, N_OPS, HG_HEADS, CHUNK, HG_DK), BF16),
                        pltpu.VMEM((2, 1, HG_WIDTH), F32),
                        pltpu.VMEM((ATT_EDGE + 1, 2 * ATT_TQ, ATT_TK), F32),
                        pltpu.VMEM((ATT_PAD + seq, LANES), BF16),
                        pltpu.VMEM((2, ATT_PAD + seq, LANES), BF16),
                        pltpu.VMEM((2, 2 * ATT_TQ, ATT_TK), F32),
                        pltpu.VMEM((2, 2 * ATT_TQ, ATT_TK), BF16)],
        compiler_params=pltpu.CompilerParams(
            dimension_semantics=("arbitrary", "arbitrary"),
            vmem_limit_bytes=VMEM_LIMIT),
        name="mixers",
    )(z, hf, z, z, lb_logits, hg_norm_w.reshape(1, HG_DV), jnp.asarray(wexp, BF16), jnp.asarray(masks),
      z, z, z, g, *later_weights)
    return outs[0], outs[1], outs[2:]


def _merge_out_kernel(ya_ref, yb_ref, ga0_ref, ga1_ref, gb0_ref, gb1_ref, x_ref,
                      wa_ref, wb_ref, wo_ref, nw_ref, h_ref, u_ref, *, rc):
    for c in range(h_ref.shape[0] // rc):
        r = slice(c * rc, (c + 1) * rc)
        pa = _dot(ya_ref[r, :], wa_ref[...])
        pb = _dot(yb_ref[r, :], wb_ref[...])
        ga = jnp.concatenate([ga0_ref[r, :], ga1_ref[r, :]], axis=1).astype(F32)
        gb = jnp.concatenate([gb0_ref[r, :], gb1_ref[r, :]], axis=1).astype(F32)
        merged = (jax.nn.sigmoid(ga) * pa + jax.nn.sigmoid(gb) * pb).astype(BF16)
        h = x_ref[r, :] + _dot(merged, wo_ref[...])
        h_ref[r, :] = h
        u_ref[r, :] = _rms(h, nw_ref[...]).astype(BF16)


def _merge_out(ya, yb, z, x2, wa_bf, wb_bf, wo_bf, norm_w, *, tm=MERGE_TM, rc=MERGE_RC):
    m, d = x2.shape
    gw = d // 2
    gcol = (4 * HG_WIDTH + 3 * AT_WIDTH) // gw
    assert gcol * gw == 4 * HG_WIDTH + 3 * AT_WIDTH

    def resident(shape):
        return pl.BlockSpec(shape, lambda i: (0, 0), pipeline_mode=pl.Buffered(1))

    def gate(blk):
        return pl.BlockSpec((tm, gw), lambda i, blk=blk: (i, gcol + blk))

    return pl.pallas_call(
        functools.partial(_merge_out_kernel, rc=rc),
        out_shape=(jax.ShapeDtypeStruct((m, d), F32), jax.ShapeDtypeStruct((m, d), BF16)),
        grid=(m // tm,),
        in_specs=[
            pl.BlockSpec((tm, HG_WIDTH), lambda i: (i, 0)),
            pl.BlockSpec((tm, AT_WIDTH), lambda i: (i, 0)),
            gate(0), gate(1), gate(2), gate(3),
            pl.BlockSpec((tm, d), lambda i: (i, 0)),
            resident(wa_bf.shape), resident(wb_bf.shape), resident(wo_bf.shape),
            resident((1, d)),
        ],
        out_specs=(pl.BlockSpec((tm, d), lambda i: (i, 0)), pl.BlockSpec((tm, d), lambda i: (i, 0))),
        compiler_params=pltpu.CompilerParams(
            dimension_semantics=("parallel",),
            vmem_limit_bytes=VMEM_LIMIT),
        name="merge_out",
    )(ya, yb, z, z, z, z, x2, wa_bf, wb_bf, wo_bf, norm_w.reshape(1, d))


MLP_SLAB = 2 * MXU_TILE


def _mlp_kernel(h_ref, u_ref, wu_ref, wd_ref, fw_ref, o_ref, hbuf_ref, h_sem):
    i, f = pl.program_id(0), pl.program_id(1)
    tm = o_ref.shape[0]

    def h_copy():
        return pltpu.make_async_copy(h_ref.at[pl.ds(i * tm, tm), :], hbuf_ref, h_sem)

    @pl.when(f == 0)
    def _():
        h_copy().start()
        o_ref[...] = jnp.zeros_like(o_ref)

    a = []
    for c in range(wu_ref.shape[1] // MLP_SLAB):
        cols = slice(c * MLP_SLAB, (c + 1) * MLP_SLAB)
        t = jnp.maximum(_dot(u_ref[...], wu_ref[:, cols]), 0.0)
        a.append((t * t).astype(BF16))
    a = jnp.concatenate(a, axis=1)
    for c in range(o_ref.shape[1] // MLP_SLAB):
        cols = slice(c * MLP_SLAB, (c + 1) * MLP_SLAB)
        o_ref[:, cols] += _dot(a, wd_ref[:, cols])

    @pl.when(f == pl.num_programs(1) - 1)
    def _():
        h_copy().wait()
        o_ref[...] = _rms(hbuf_ref[...] + o_ref[...], fw_ref[...])


def _mlp(h, u, wu_bf, wd_bf, final_w, *, tm=MLP_TM, tf=MLP_TF):
    m, d = h.shape
    dff = wu_bf.shape[1]
    return pl.pallas_call(
        _mlp_kernel,
        out_shape=jax.ShapeDtypeStruct((m, d), F32),
        grid=(m // tm, dff // tf),
        in_specs=[
            pl.BlockSpec(memory_space=pl.ANY),
            pl.BlockSpec((tm, d), lambda i, f: (i, 0)),
            pl.BlockSpec((d, tf), lambda i, f: (0, f)),
            pl.BlockSpec((tf, d), lambda i, f: (f, 0)),
            pl.BlockSpec((1, d), lambda i, f: (0, 0)),
        ],
        out_specs=pl.BlockSpec((tm, d), lambda i, f: (i, 0)),
        scratch_shapes=[pltpu.VMEM((tm, d), F32), pltpu.SemaphoreType.DMA],
        compiler_params=pltpu.CompilerParams(
            dimension_semantics=("arbitrary", "arbitrary"),
            vmem_limit_bytes=BIG_VMEM_LIMIT),
        name="mlp",
    )(h, u, wu_bf, wd_bf, final_w.reshape(1, d))


def kernel(x, w_in, lb_logits, hg_norm_w, rel_bias, w_branch_a, w_branch_b, w_out,
           norm_mix_w, norm_mlp_w, w_up, w_down, norm_final_w):
    batch, seq, d = x.shape
    assert d == D_MODEL and seq % ATT_TQ == 0 and w_in.shape[0] == 1
    x2 = x.reshape(batch * seq, d)
    z, hf = _in_proj(x2, norm_mix_w[0], w_in[0])
    ya, yb, (wa_bf, wb_bf, wo_bf, wu_bf, wd_bf) = _mixers(
        z, hf, lb_logits, hg_norm_w[0], rel_bias[0],
        (w_branch_a[0], w_branch_b[0], w_out[0], w_up[0], w_down[0]), batch, seq)
    h, u = _merge_out(ya, yb, z, x2, wa_bf, wb_bf, wo_bf, norm_mlp_w[0])
    out = _mlp(h, u, wu_bf, wd_bf, norm_final_w)
    return out.reshape(batch, seq, d)
```

```python
import functools

import jax
import jax.numpy as jnp
import numpy as np
from jax import lax
from jax.experimental import pallas as pl
from jax.experimental.pallas import tpu as pltpu

D_MODEL = 2048
CHUNK = 64
HG_HEADS = 8
HG_DK = 128
HG_DV = 128
HG_WIDTH = HG_HEADS * HG_DV
AT_HEADS = 16
AT_DH = 64
AT_WIDTH = AT_HEADS * AT_DH
LEFT_CHUNKS = 8
REL_CLIP = 256
EPS = 1e-6

LANES = 128
MXU_TILE = 256
N_LEVELS = 6
ATT_TQ = 2 * CHUNK

BF16 = jnp.bfloat16
F32 = jnp.float32
VMEM_LIMIT = 60000 * 1024

IN_TM, IN_TN = 2048, 1024
MIX_TB = 4 * CHUNK
MERGE_TM, MERGE_RC = 512, 256
MLP_TM, MLP_TF = 1024, 1024
BIG_VMEM_LIMIT = 62 * 1024 * 1024

NN = (((1,), (0,)), ((), ()))
NT = (((1,), (1,)), ((), ()))
TN = (((0,), (0,)), ((), ()))


def _dot(a, b, dims=NN):
    return lax.dot_general(a, b, dims, preferred_element_type=F32)


def _rms(xf, w):
    return xf * lax.rsqrt(jnp.mean(xf * xf, axis=-1, keepdims=True) + EPS) * w


IN_SLAB = MXU_TILE
LOG2E = 1.4426950408889634
AT_QSCALE = AT_DH ** -0.5 * LOG2E
AQ_COLS = (4 * HG_WIDTH, 4 * HG_WIDTH + AT_WIDTH)
HF_TILE = HG_WIDTH // IN_TN


def _in_col_tile(j, nj):
    return jnp.where(j == nj - 1, HF_TILE, jnp.where(j < HF_TILE, j, j + 1))


def _in_proj_kernel(x_ref, nw_ref, w_ref, z_ref, zf_ref, u_ref, xbuf_ref, accf_ref, x_sem, zf_sem):
    tm, tn = z_ref.shape
    i, j = pl.program_id(0), pl.program_id(1)
    ni, nj = pl.num_programs(0), pl.num_programs(1)

    def x_copy(tile):
        return pltpu.make_async_copy(x_ref.at[pl.ds(tile * tm, tm), :], xbuf_ref, x_sem)

    def zf_copy(tile):
        return pltpu.make_async_copy(accf_ref, zf_ref.at[pl.ds(tile * tm, tm), :], zf_sem)

    @pl.when((i == 0) & (j == 0))
    def _():
        x_copy(0).start()

    @pl.when(j == 0)
    def _():
        x_copy(i).wait()
        u_ref[...] = _rms(xbuf_ref[...], nw_ref[...]).astype(BF16)

        @pl.when(i > 0)
        def _():
            zf_copy(i - 1).wait()

    @pl.when((j == 1) & (i + 1 < ni))
    def _():
        x_copy(i + 1).start()

    col0 = _in_col_tile(j, nj) * tn
    zscale = jnp.where((col0 >= AQ_COLS[0]) & (col0 < AQ_COLS[1]), AT_QSCALE, 1.0).astype(F32)
    for c in range(tn // IN_SLAB):
        cols = slice(c * IN_SLAB, (c + 1) * IN_SLAB)
        acc = _dot(u_ref[...], w_ref[:, cols].astype(BF16))
        z_ref[:, cols] = (acc * zscale).astype(z_ref.dtype)
        accf_ref[:, cols] = acc

    @pl.when(j == nj - 1)
    def _():
        zf_copy(i).start()

        @pl.when(i == ni - 1)
        def _():
            zf_copy(i).wait()


def _in_proj(x2, norm_w, w, *, tm=IN_TM, tn=IN_TN):
    m, k = x2.shape
    n = w.shape[1]
    nj = n // tn
    assert tn == HG_WIDTH
    assert AQ_COLS[0] % tn == 0 and AQ_COLS[1] % tn == 0
    return pl.pallas_call(
        _in_proj_kernel,
        out_shape=(jax.ShapeDtypeStruct((m, n), BF16), jax.ShapeDtypeStruct((m, HG_WIDTH), F32)),
        grid=(m // tm, nj),
        in_specs=[
            pl.BlockSpec(memory_space=pl.ANY),
            pl.BlockSpec((1, k), lambda i, j: (0, 0)),
            pl.BlockSpec((k, tn), lambda i, j: (0, _in_col_tile(j, nj))),
        ],
        out_specs=(
            pl.BlockSpec((tm, tn), lambda i, j: (i, _in_col_tile(j, nj))),
            pl.BlockSpec(memory_space=pl.ANY),
        ),
        scratch_shapes=[pltpu.VMEM((tm, k), BF16), pltpu.VMEM((tm, k), F32), pltpu.VMEM((tm, tn), F32),
                        pltpu.SemaphoreType.DMA, pltpu.SemaphoreType.DMA],
        compiler_params=pltpu.CompilerParams(
            dimension_semantics=("arbitrary", "arbitrary"),
            vmem_limit_bytes=BIG_VMEM_LIMIT),
        name="in_proj",
    )(x2, norm_w.reshape(1, k), w)


MXU_LEVELS = 3


def _hgrn_tables():
    t = np.arange(CHUNK)[:, None]
    s = np.arange(CHUNK)[None, :]
    mats, masks = [], []
    for l in range(N_LEVELS):
        h = 1 << l
        start = (t // (2 * h)) * (2 * h)
        ref = start + h - 1
        is_q = (t - start) >= h
        if l < MXU_LEVELS:
            mats.append(np.where(is_q, (s > ref) & (s <= t), (s > t) & (s <= ref)))
        s_start = (s // (2 * h)) * (2 * h)
        masks.append((start == s_start) & is_q & ((s - s_start) < h))
    mats.append(s <= t)
    masks.append(t == s)
    w = np.concatenate(mats, axis=0).astype(np.float32)
    w2 = np.concatenate([w, w], axis=1)
    return w2, np.stack(masks).astype(np.float32)


def _level_exponent(b, h):
    parts = []
    for p in range(b.shape[0] // (2 * h)):
        lo = p * 2 * h
        ref = b[lo + h - 1:lo + h, :]
        parts += [ref - b[lo:lo + h, :], b[lo + h:lo + 2 * h, :] - ref]
    return jnp.concatenate(parts, axis=0)


N_OPS = N_LEVELS + 2
OP_DIAG, OP_STATE = N_LEVELS, N_LEVELS + 1


def _hgrn_units(hq_ref, hf_ref, hi_ref, hg_ref, lbl_ref, nw_ref, wexp_ref, mask_ref, y_ref,
                st_ref, qx_ref, kx_ref, d_ref, *, first_block, n_chunks):
    @pl.when(first_block)
    def _():
        st_ref[...] = jnp.zeros_like(st_ref)

    logits = lbl_ref[...]
    l0, l1 = logits[0:1, :], logits[1:2, :]
    mx = jnp.maximum(l0, l1)
    e0, e1 = jnp.exp(l0 - mx), jnp.exp(l1 - mx)
    lb = e0 / (e0 + e1)
    nw = nw_ref[...]
    wexp = wexp_ref[...]

    def stage(c):
        slot = c % 2
        r = slice(c * CHUNK, (c + 1) * CHUNK)
        f = jax.nn.sigmoid(hf_ref[r, :])
        g = lb + (1.0 - lb) * f
        lg = jnp.log2(g)
        kk = (1.0 - g).astype(BF16)
        q = (jax.nn.silu(hq_ref[r, :].astype(F32)) * (HG_DK ** -0.5)).astype(BF16)
        lg_hi = lg.astype(BF16)
        lg_lo = (lg - lg_hi.astype(F32)).astype(BF16)
        e_mxu = _dot(wexp, jnp.concatenate([lg_hi, lg_lo], axis=0))
        b = e_mxu[MXU_LEVELS * CHUNK:, :]

        def put(ref, op, val):
            for hh in range(HG_HEADS):
                ref[slot, op, hh] = val[:, hh * HG_DK:(hh + 1) * HG_DK]

        put(qx_ref, OP_DIAG, q)
        put(kx_ref, OP_DIAG, kk)
        for l in range(N_LEVELS):
            e = e_mxu[l * CHUNK:(l + 1) * CHUNK, :] if l < MXU_LEVELS else _level_exponent(b, 1 << l)
            x = jnp.exp2(e).astype(BF16)
            put(qx_ref, l, q * x)
            put(kx_ref, l, kk * x)
        eb = jnp.exp2(b)
        d_ref[slot] = eb[CHUNK - 1:CHUNK, :]
        put(qx_ref, OP_STATE, q * eb.astype(BF16))
        put(kx_ref, OP_STATE, kk * jnp.exp2(b[CHUNK - 1:CHUNK, :] - b).astype(BF16))

    def scores(c, h):
        slot = c % 2
        sc = mask_ref[OP_DIAG] * _dot(qx_ref[slot, OP_DIAG, h], kx_ref[slot, OP_DIAG, h], NT)
        for l in range(N_LEVELS):
            sc = sc + mask_ref[l] * _dot(qx_ref[slot, l, h], kx_ref[slot, l, h], NT)
        return sc.astype(BF16)

    def outputs(c, h, sc):
        slot, sl = c % 2, slice(h * HG_DK, (h + 1) * HG_DK)
        r = slice(c * CHUNK, (c + 1) * CHUNK)
        vh = hi_ref[r, sl]
        st = st_ref[h]
        o = _dot(sc, vh) + _dot(qx_ref[slot, OP_STATE, h], st.astype(BF16), NT)
        st_ref[h] = d_ref[slot, :, sl] * st + _dot(vh, kx_ref[slot, OP_STATE, h], TN)
        y = _rms(o, nw) * jax.nn.silu(hg_ref[r, sl].astype(F32))
        y_ref[r, sl] = y.astype(y_ref.dtype)

    stage(0)
    yield
    for c in range(n_chunks):
        sc_next = scores(c, 0)
        for h in range(HG_HEADS):
            sc = sc_next
            if h + 1 < HG_HEADS:
                sc_next = scores(c, h + 1)
            if h == 0 and c + 1 < n_chunks:
                stage(c + 1)
            outputs(c, h, sc)
            yield


ATT_TK = ATT_TQ + LEFT_CHUNKS * CHUNK
ATT_ROLL = 1024
assert ATT_TQ - 1 + ATT_TK <= ATT_ROLL and ATT_TK - ATT_TQ == 2 * REL_CLIP


def _rel_table(rel_bias):
    rev = rel_bias[:, ::-1].astype(F32) * LOG2E
    edge = jnp.broadcast_to(rev[:, :1], (rel_bias.shape[0], REL_CLIP))
    g = jnp.concatenate([edge, rev[:, :2 * REL_CLIP], edge], axis=1)
    return g.reshape(AT_HEADS // 2, 2, ATT_ROLL)


ATT_PAD = ATT_TK - ATT_TQ
ATT_EDGE = ATT_PAD // ATT_TQ
ATT_RG = 32


def _attn_units(q_ref, k_ref, v_ref, g_ref, o_ref, bias_ref, kp_ref, vp_ref, s_ref, p_ref,
                *, new_head_pair, seq):
    @pl.when(new_head_pair)
    def _():
        qc = lax.broadcasted_iota(jnp.int32, (ATT_TQ, ATT_TK), 0) // CHUNK
        kc = lax.broadcasted_iota(jnp.int32, (ATT_TQ, ATT_TK), 1) // CHUNK
        in_band = (kc >= qc) & (kc <= qc + LEFT_CHUNKS)
        g = g_ref[0]
        for hh in range(2):
            tbl = jnp.broadcast_to(g[hh:hh + 1, :], (ATT_TQ, ATT_ROLL))
            toep = pltpu.roll(tbl, 0, 1, stride=1, stride_axis=0)[:, :ATT_TK]
            rows = slice(hh * ATT_TQ, (hh + 1) * ATT_TQ)
            for n in range(ATT_EDGE + 1):
                first_chunk = (ATT_PAD - n * ATT_TQ) // CHUNK if n < ATT_EDGE else 0
                bias_ref[n, rows, :] = jnp.where(in_band & (kc >= first_chunk), toep, -jnp.inf)

    v_all = v_ref[...]
    head0_all = lax.broadcasted_iota(jnp.int32, v_all.shape, 1) < AT_DH
    ones = jnp.ones_like(v_all)
    pad = jnp.zeros((ATT_PAD, LANES), BF16)
    kp_ref[:ATT_PAD, :] = pad
    kp_ref[ATT_PAD:, :] = k_ref[...]
    vp_ref[0, :ATT_PAD, :] = pad
    vp_ref[1, :ATT_PAD, :] = pad
    vp_ref[0, ATT_PAD:, :] = jnp.where(head0_all, v_all, ones)
    vp_ref[1, ATT_PAD:, :] = jnp.where(head0_all, ones, v_all)

    head0 = lax.broadcasted_iota(jnp.int32, (ATT_TQ, LANES), 1) < AT_DH
    n_blocks = seq // ATT_TQ
    yield

    def scores(i):
        slot = i % 2
        q = q_ref[i * ATT_TQ:(i + 1) * ATT_TQ, :]
        zero = jnp.zeros_like(q)
        qq = jnp.concatenate([jnp.where(head0, q, zero), jnp.where(head0, zero, q)], axis=0)
        s = _dot(qq, kp_ref[i * ATT_TQ:i * ATT_TQ + ATT_TK, :], NT)
        s_ref[slot] = s + bias_ref[min(i, ATT_EDGE)]

    def outputs(i):
        slot = i % 2
        for g in range(2 * ATT_TQ // ATT_RG):
            rows = slice(g * ATT_RG, (g + 1) * ATT_RG)
            s = s_ref[slot, rows, :]
            p_ref[slot, rows, :] = jnp.exp2(s - jnp.max(s, axis=-1, keepdims=True)).astype(BF16)
        keys = slice(i * ATT_TQ, i * ATT_TQ + ATT_TK)
        acc0 = _dot(p_ref[slot, :ATT_TQ, :], vp_ref[0, keys, :])
        acc1 = _dot(p_ref[slot, ATT_TQ:, :], vp_ref[1, keys, :])
        num = jnp.where(head0, acc0, acc1)
        den = pltpu.roll(jnp.where(head0, acc1, acc0), AT_DH, 1)
        o_ref[i * ATT_TQ:(i + 1) * ATT_TQ, :] = (num / den).astype(o_ref.dtype)

    scores(0)
    for i in range(n_blocks):
        if i + 1 < n_blocks:
            scores(i + 1)
        outputs(i)
        yield


def _mixers_kernel(hq_ref, hf_ref, hi_ref, hg_ref, lbl_ref, nw_ref, wexp_ref, mask_ref,
                   q_ref, k_ref, v_ref, g_ref, *rest, n_cast, n_chunks, seq, steps_per_pair):
    cast_in, (y_ref, o_ref), cast_out = rest[:n_cast], rest[n_cast:n_cast + 2], rest[n_cast + 2:2 * n_cast + 2]
    st_ref, qx_ref, kx_ref, d_ref, bias_ref, kp_ref, vp_ref, s_ref, p_ref = rest[2 * n_cast + 2:]
    for src, dst in zip(cast_in, cast_out):
        dst[...] = src[...].astype(BF16)
    step = pl.program_id(0) * pl.num_programs(1) + pl.program_id(1)
    hgrn = _hgrn_units(hq_ref, hf_ref, hi_ref, hg_ref, lbl_ref, nw_ref, wexp_ref, mask_ref,
                       y_ref, st_ref, qx_ref, kx_ref, d_ref,
                       first_block=pl.program_id(1) == 0, n_chunks=n_chunks)
    attn = _attn_units(q_ref, k_ref, v_ref, g_ref, o_ref, bias_ref, kp_ref, vp_ref, s_ref, p_ref,
                       new_head_pair=step % steps_per_pair == 0, seq=seq)
    n_hgrn, n_attn = n_chunks * HG_HEADS, seq // ATT_TQ
    assert n_hgrn % n_attn == 0
    next(hgrn)
    next(attn)
    for _ in range(n_attn):
        next(attn)
        for _ in range(n_hgrn // n_attn):
            next(hgrn)
    assert next(attn, None) is None and next(hgrn, None) is None


def _mixers(z, hf, lb_logits, hg_norm_w, rel_bias, later_weights, batch, seq, *, tb=MIX_TB):
    wexp, masks = _hgrn_tables()
    g = _rel_table(rel_bias)
    nblk = seq // tb
    steps = batch * nblk
    assert steps == AT_HEADS // 2 * batch
    col0 = 4 * HG_WIDTH // LANES
    ncol = AT_WIDTH // LANES

    def step_of(b, i):
        return b * nblk + i

    def hspec(col):
        return pl.BlockSpec((tb, HG_WIDTH), lambda b, i, col=col: (b * nblk + i, col))

    def aspec(which):
        return pl.BlockSpec((seq, LANES), lambda b, i, which=which: (
            step_of(b, i) % batch, col0 + which * ncol + step_of(b, i) // batch))

    def const(shape):
        return pl.BlockSpec(shape, lambda b, i: (0,) * len(shape))

    def cast_spec(w):
        return pl.BlockSpec((w.shape[0] // steps, w.shape[1]), lambda b, i: (step_of(b, i), 0))

    cast_specs = [cast_spec(w) for w in later_weights]
    outs = pl.pallas_call(
        functools.partial(_mixers_kernel, n_cast=len(later_weights), n_chunks=tb // CHUNK, seq=seq,
                          steps_per_pair=batch),
        out_shape=(jax.ShapeDtypeStruct((batch * seq, HG_WIDTH), BF16),
                   jax.ShapeDtypeStruct((batch * seq, AT_WIDTH), BF16),
                   *[jax.ShapeDtypeStruct(w.shape, BF16) for w in later_weights]),
        grid=(batch, nblk),
        in_specs=[
            hspec(0), hspec(0), hspec(2), hspec(3),
            const((2, HG_WIDTH)), const((1, HG_DV)), const(wexp.shape), const(masks.shape),
            aspec(0), aspec(1), aspec(2),
            pl.BlockSpec((1, 2, ATT_ROLL), lambda b, i: (step_of(b, i) // batch, 0, 0)),
            *cast_specs,
        ],
        out_specs=(
            pl.BlockSpec((tb, HG_WIDTH), lambda b, i: (b * nblk + i, 0)),
            pl.BlockSpec((seq, LANES), lambda b, i: (step_of(b, i) % batch, step_of(b, i) // batch)),
            *cast_specs,
        ),
        scratch_shapes=[pltpu.VMEM((HG_HEADS, HG_DV, HG_DK), F32),
                        pltpu.VMEM((2, N_OPS, HG_HEADS, CHUNK, HG_DK), BF16),
                        pltpu.VMEM((2, N_OPS, HG_HEADS, CHUNK, HG_DK), BF16),
                        pltpu.VMEM((2, 1, HG_WIDTH), F32),
                        pltpu.VMEM((ATT_EDGE + 1, 2 * ATT_TQ, ATT_TK), F32),
                        pltpu.VMEM((ATT_PAD + seq, LANES), BF16),
                        pltpu.VMEM((2, ATT_PAD + seq, LANES), BF16),
                        pltpu.VMEM((2, 2 * ATT_TQ, ATT_TK), F32),
                        pltpu.VMEM((2, 2 * ATT_TQ, ATT_TK), BF16)],
        compiler_params=pltpu.CompilerParams(
            dimension_semantics=("arbitrary", "arbitrary"),
            vmem_limit_bytes=VMEM_LIMIT),
        name="mixers",
    )(z, hf, z, z, lb_logits, hg_norm_w.reshape(1, HG_DV), jnp.asarray(wexp, BF16), jnp.asarray(masks),
      z, z, z, g, *later_weights)
    return outs[0], outs[1], outs[2:]


def _merge_out_kernel(ya_ref, yb_ref, ga0_ref, ga1_ref, gb0_ref, gb1_ref, x_ref,
                      wa_ref, wb_ref, wo_ref, nw_ref, h_ref, u_ref, *, rc):
    for c in range(h_ref.shape[0] // rc):
        r = slice(c * rc, (c + 1) * rc)
        pa = _dot(ya_ref[r, :], wa_ref[...])
        pb = _dot(yb_ref[r, :], wb_ref[...])
        ga = jnp.concatenate([ga0_ref[r, :], ga1_ref[r, :]], axis=1).astype(F32)
        gb = jnp.concatenate([gb0_ref[r, :], gb1_ref[r, :]], axis=1).astype(F32)
        merged = (jax.nn.sigmoid(ga) * pa + jax.nn.sigmoid(gb) * pb).astype(BF16)
        h = x_ref[r, :] + _dot(merged, wo_ref[...])
        h_ref[r, :] = h
        u_ref[r, :] = _rms(h, nw_ref[...]).astype(BF16)


def _merge_out(ya, yb, z, x2, wa_bf, wb_bf, wo_bf, norm_w, *, tm=MERGE_TM, rc=MERGE_RC):
    m, d = x2.shape
    gw = d // 2
    gcol = (4 * HG_WIDTH + 3 * AT_WIDTH) // gw
    assert gcol * gw == 4 * HG_WIDTH + 3 * AT_WIDTH

    def resident(shape):
        return pl.BlockSpec(shape, lambda i: (0, 0), pipeline_mode=pl.Buffered(1))

    def gate(blk):
        return pl.BlockSpec((tm, gw), lambda i, blk=blk: (i, gcol + blk))

    return pl.pallas_call(
        functools.partial(_merge_out_kernel, rc=rc),
        out_shape=(jax.ShapeDtypeStruct((m, d), F32), jax.ShapeDtypeStruct((m, d), BF16)),
        grid=(m // tm,),
        in_specs=[
            pl.BlockSpec((tm, HG_WIDTH), lambda i: (i, 0)),
            pl.BlockSpec((tm, AT_WIDTH), lambda i: (i, 0)),
            gate(0), gate(1), gate(2), gate(3),
            pl.BlockSpec((tm, d), lambda i: (i, 0)),
            resident(wa_bf.shape), resident(wb_bf.shape), resident(wo_bf.shape),
            resident((1, d)),
        ],
        out_specs=(pl.BlockSpec((tm, d), lambda i: (i, 0)), pl.BlockSpec((tm, d), lambda i: (i, 0))),
        compiler_params=pltpu.CompilerParams(
            dimension_semantics=("parallel",),
            vmem_limit_bytes=VMEM_LIMIT),
        name="merge_out",
    )(ya, yb, z, z, z, z, x2, wa_bf, wb_bf, wo_bf, norm_w.reshape(1, d))


MLP_SLAB = 2 * MXU_TILE


def _mlp_kernel(h_ref, u_ref, wu_ref, wd_ref, fw_ref, o_ref, hbuf_ref, h_sem):
    i, f = pl.program_id(0), pl.program_id(1)
    tm = o_ref.shape[0]

    def h_copy():
        return pltpu.make_async_copy(h_ref.at[pl.ds(i * tm, tm), :], hbuf_ref, h_sem)

    @pl.when(f == 0)
    def _():
        h_copy().start()
        o_ref[...] = jnp.zeros_like(o_ref)

    a = []
    for c in range(wu_ref.shape[1] // MLP_SLAB):
        cols = slice(c * MLP_SLAB, (c + 1) * MLP_SLAB)
        t = jnp.maximum(_dot(u_ref[...], wu_ref[:, cols]), 0.0)
        a.append((t * t).astype(BF16))
    a = jnp.concatenate(a, axis=1)
    for c in range(o_ref.shape[1] // MLP_SLAB):
        cols = slice(c * MLP_SLAB, (c + 1) * MLP_SLAB)
        o_ref[:, cols] += _dot(a, wd_ref[:, cols])

    @pl.when(f == pl.num_programs(1) - 1)
    def _():
        h_copy().wait()
        o_ref[...] = _rms(hbuf_ref[...] + o_ref[...], fw_ref[...])


def _mlp(h, u, wu_bf, wd_bf, final_w, *, tm=MLP_TM, tf=MLP_TF):
    m, d = h.shape
    dff = wu_bf.shape[1]
    return pl.pallas_call(
        _mlp_kernel,
        out_shape=jax.ShapeDtypeStruct((m, d), F32),
        grid=(m // tm, dff // tf),
        in_specs=[
            pl.BlockSpec(memory_space=pl.ANY),
            pl.BlockSpec((tm, d), lambda i, f: (i, 0)),
            pl.BlockSpec((d, tf), lambda i, f: (0, f)),
            pl.BlockSpec((tf, d), lambda i, f: (f, 0)),
            pl.BlockSpec((1, d), lambda i, f: (0, 0)),
        ],
        out_specs=pl.BlockSpec((tm, d), lambda i, f: (i, 0)),
        scratch_shapes=[pltpu.VMEM((tm, d), F32), pltpu.SemaphoreType.DMA],
        compiler_params=pltpu.CompilerParams(
            dimension_semantics=("arbitrary", "arbitrary"),
            vmem_limit_bytes=BIG_VMEM_LIMIT),
        name="mlp",
    )(h, u, wu_bf, wd_bf, final_w.reshape(1, d))


def kernel(x, w_in, lb_logits, hg_norm_w, rel_bias, w_branch_a, w_branch_b, w_out,
           norm_mix_w, norm_mlp_w, w_up, w_down, norm_final_w):
    batch, seq, d = x.shape
    assert d == D_MODEL and seq % ATT_TQ == 0 and w_in.shape[0] == 1
    x2 = x.reshape(batch * seq, d)
    z, hf = _in_proj(x2, norm_mix_w[0], w_in[0])
    ya, yb, (wa_bf, wb_bf, wo_bf, wu_bf, wd_bf) = _mixers(
        z, hf, lb_logits, hg_norm_w[0], rel_bias[0],
        (w_branch_a[0], w_branch_b[0], w_out[0], w_up[0], w_down[0]), batch, seq)
    h, u = _merge_out(ya, yb, z, x2, wa_bf, wb_bf, wo_bf, norm_mlp_w[0])
    out = _mlp(h, u, wu_bf, wd_bf, norm_final_w)
    return out.reshape(batch, seq, d)
```

```python
import functools

import jax
import jax.numpy as jnp
import numpy as np
from jax import lax
from jax.experimental import pallas as pl
from jax.experimental.pallas import tpu as pltpu

D_MODEL = 2048
CHUNK = 64
HG_HEADS = 8
HG_DK = 128
HG_DV = 128
HG_WIDTH = HG_HEADS * HG_DV
AT_HEADS = 16
AT_DH = 64
AT_WIDTH = AT_HEADS * AT_DH
LEFT_CHUNKS = 8
REL_CLIP = 256
EPS = 1e-6

LANES = 128
MXU_TILE = 256
N_LEVELS = 6
ATT_TQ = 2 * CHUNK

BF16 = jnp.bfloat16
F32 = jnp.float32
VMEM_LIMIT = 60000 * 1024

IN_TM, IN_TN = 2048, 1024
MIX_TB = 4 * CHUNK
MERGE_TM, MERGE_RC = 512, 256
MLP_TM, MLP_TF = 1024, 1024
BIG_VMEM_LIMIT = 62 * 1024 * 1024

NN = (((1,), (0,)), ((), ()))
NT = (((1,), (1,)), ((), ()))
TN = (((0,), (0,)), ((), ()))


def _dot(a, b, dims=NN):
    return lax.dot_general(a, b, dims, preferred_element_type=F32)


def _rms(xf, w):
    return xf * lax.rsqrt(jnp.mean(xf * xf, axis=-1, keepdims=True) + EPS) * w


IN_SLAB = MXU_TILE
LOG2E = 1.4426950408889634
AT_QSCALE = AT_DH ** -0.5 * LOG2E
AQ_COLS = (4 * HG_WIDTH, 4 * HG_WIDTH + AT_WIDTH)
HF_TILE = HG_WIDTH // IN_TN


def _in_col_tile(j, nj):
    return jnp.where(j == nj - 1, HF_TILE, jnp.where(j < HF_TILE, j, j + 1))


def _in_proj_kernel(x_ref, nw_ref, w_ref, z_ref, zf_ref, u_ref, xbuf_ref, accf_ref, x_sem, zf_sem):
    tm, tn = z_ref.shape
    i, j = pl.program_id(0), pl.program_id(1)
    ni, nj = pl.num_programs(0), pl.num_programs(1)

    def x_copy(tile):
        return pltpu.make_async_copy(x_ref.at[pl.ds(tile * tm, tm), :], xbuf_ref, x_sem)

    def zf_copy(tile):
        return pltpu.make_async_copy(accf_ref, zf_ref.at[pl.ds(tile * tm, tm), :], zf_sem)

    @pl.when((i == 0) & (j == 0))
    def _():
        x_copy(0).start()

    @pl.when(j == 0)
    def _():
        x_copy(i).wait()
        u_ref[...] = _rms(xbuf_ref[...], nw_ref[...]).astype(BF16)

        @pl.when(i > 0)
        def _():
            zf_copy(i - 1).wait()

    @pl.when((j == 1) & (i + 1 < ni))
    def _():
        x_copy(i + 1).start()

    col0 = _in_col_tile(j, nj) * tn
    zscale = jnp.where((col0 >= AQ_COLS[0]) & (col0 < AQ_COLS[1]), AT_QSCALE, 1.0).astype(F32)
    for c in range(tn // IN_SLAB):
        cols = slice(c * IN_SLAB, (c + 1) * IN_SLAB)
        acc = _dot(u_ref[...], w_ref[:, cols].astype(BF16))
        z_ref[:, cols] = (acc * zscale).astype(z_ref.dtype)
        accf_ref[:, cols] = acc

    @pl.when(j == nj - 1)
    def _():
        zf_copy(i).start()

        @pl.when(i == ni - 1)
        def _():
            zf_copy(i).wait()


def _in_proj(x2, norm_w, w, *, tm=IN_TM, tn=IN_TN):
    m, k = x2.shape
    n = w.shape[1]
    nj = n // tn
    assert tn == HG_WIDTH
    assert AQ_COLS[0] % tn == 0 and AQ_COLS[1] % tn == 0
    return pl.pallas_call(
        _in_proj_kernel,
        out_shape=(jax.ShapeDtypeStruct((m, n), BF16), jax.ShapeDtypeStruct((m, HG_WIDTH), F32)),
        grid=(m // tm, nj),
        in_specs=[
            pl.BlockSpec(memory_space=pl.ANY),
            pl.BlockSpec((1, k), lambda i, j: (0, 0)),
            pl.BlockSpec((k, tn), lambda i, j: (0, _in_col_tile(j, nj))),
        ],
        out_specs=(
            pl.BlockSpec((tm, tn), lambda i, j: (i, _in_col_tile(j, nj))),
            pl.BlockSpec(memory_space=pl.ANY),
        ),
        scratch_shapes=[pltpu.VMEM((tm, k), BF16), pltpu.VMEM((tm, k), F32), pltpu.VMEM((tm, tn), F32),
                        pltpu.SemaphoreType.DMA, pltpu.SemaphoreType.DMA],
        compiler_params=pltpu.CompilerParams(
            dimension_semantics=("arbitrary", "arbitrary"),
            vmem_limit_bytes=BIG_VMEM_LIMIT),
        name="in_proj",
    )(x2, norm_w.reshape(1, k), w)


MXU_LEVELS = 3


def _hgrn_tables():
    t = np.arange(CHUNK)[:, None]
    s = np.arange(CHUNK)[None, :]
    mats, masks = [], []
    for l in range(N_LEVELS):
        h = 1 << l
        start = (t // (2 * h)) * (2 * h)
        ref = start + h - 1
        is_q = (t - start) >= h
        if l < MXU_LEVELS:
            mats.append(np.where(is_q, (s > ref) & (s <= t), (s > t) & (s <= ref)))
        s_start = (s // (2 * h)) * (2 * h)
        masks.append((start == s_start) & is_q & ((s - s_start) < h))
    mats.append(s <= t)
    masks.append(t == s)
    w = np.concatenate(mats, axis=0).astype(np.float32)
    w2 = np.concatenate([w, w], axis=1)
    return w2, np.stack(masks).astype(np.float32)


def _level_exponent(b, h):
    parts = []
    for p in range(b.shape[0] // (2 * h)):
        lo = p * 2 * h
        ref = b[lo + h - 1:lo + h, :]
        parts += [ref - b[lo:lo + h, :], b[lo + h:lo + 2 * h, :] - ref]
    return jnp.concatenate(parts, axis=0)


N_OPS = N_LEVELS + 2
OP_DIAG, OP_STATE = N_LEVELS, N_LEVELS + 1


def _hgrn_units(hq_ref, hf_ref, hi_ref, hg_ref, lbl_ref, nw_ref, wexp_ref, mask_ref, y_ref,
                st_ref, qx_ref, kx_ref, d_ref, *, first_block, n_chunks):
    @pl.when(first_block)
    def _():
        st_ref[...] = jnp.zeros_like(st_ref)

    logits = lbl_ref[...]
    l0, l1 = logits[0:1, :], logits[1:2, :]
    mx = jnp.maximum(l0, l1)
    e0, e1 = jnp.exp(l0 - mx), jnp.exp(l1 - mx)
    lb = e0 / (e0 + e1)
    nw = nw_ref[...]
    wexp = wexp_ref[...]

    def stage(c):
        slot = c % 2
        r = slice(c * CHUNK, (c + 1) * CHUNK)
        f = jax.nn.sigmoid(hf_ref[r, :])
        g = lb + (1.0 - lb) * f
        lg = jnp.log2(g)
        kk = (1.0 - g).astype(BF16)
        q = (jax.nn.silu(hq_ref[r, :].astype(F32)) * (HG_DK ** -0.5)).astype(BF16)
        lg_hi = lg.astype(BF16)
        lg_lo = (lg - lg_hi.astype(F32)).astype(BF16)
        e_mxu = _dot(wexp, jnp.concatenate([lg_hi, lg_lo], axis=0))
        b = e_mxu[MXU_LEVELS * CHUNK:, :]

        def put(ref, op, val):
            for hh in range(HG_HEADS):
                ref[slot, op, hh] = val[:, hh * HG_DK:(hh + 1) * HG_DK]

        put(qx_ref, OP_DIAG, q)
        put(kx_ref, OP_DIAG, kk)
        for l in range(N_LEVELS):
            e = e_mxu[l * CHUNK:(l + 1) * CHUNK, :] if l < MXU_LEVELS else _level_exponent(b, 1 << l)
            x = jnp.exp2(e).astype(BF16)
            put(qx_ref, l, q * x)
            put(kx_ref, l, kk * x)
        eb = jnp.exp2(b)
        d_ref[slot] = eb[CHUNK - 1:CHUNK, :]
        put(qx_ref, OP_STATE, q * eb.astype(BF16))
        put(kx_ref, OP_STATE, kk * jnp.exp2(b[CHUNK - 1:CHUNK, :] - b).astype(BF16))

    def scores(c, h):
        slot = c % 2
        sc = mask_ref[OP_DIAG] * _dot(qx_ref[slot, OP_DIAG, h], kx_ref[slot, OP_DIAG, h], NT)
        for l in range(N_LEVELS):
            sc = sc + mask_ref[l] * _dot(qx_ref[slot, l, h], kx_ref[slot, l, h], NT)
        return sc.astype(BF16)

    def outputs(c, h, sc):
        slot, sl = c % 2, slice(h * HG_DK, (h + 1) * HG_DK)
        r = slice(c * CHUNK, (c + 1) * CHUNK)
        vh = hi_ref[r, sl]
        st = st_ref[h]
        o = _dot(sc, vh) + _dot(qx_ref[slot, OP_STATE, h], st.astype(BF16), NT)
        st_ref[h] = d_ref[slot, :, sl] * st + _dot(vh, kx_ref[slot, OP_STATE, h], TN)
        y = _rms(o, nw) * jax.nn.silu(hg_ref[r, sl].astype(F32))
        y_ref[r, sl] = y.astype(y_ref.dtype)

    stage(0)
    yield
    for c in range(n_chunks):
        sc_next = scores(c, 0)
        for h in range(HG_HEADS):
            sc = sc_next
            if h + 1 < HG_HEADS:
                sc_next = scores(c, h + 1)
            if h == 0 and c + 1 < n_chunks:
                stage(c + 1)
            outputs(c, h, sc)
            yield


ATT_TK = ATT_TQ + LEFT_CHUNKS * CHUNK
ATT_ROLL = 1024
assert ATT_TQ - 1 + ATT_TK <= ATT_ROLL and ATT_TK - ATT_TQ == 2 * REL_CLIP


def _rel_table(rel_bias):
    rev = rel_bias[:, ::-1].astype(F32) * LOG2E
    edge = jnp.broadcast_to(rev[:, :1], (rel_bias.shape[0], REL_CLIP))
    g = jnp.concatenate([edge, rev[:, :2 * REL_CLIP], edge], axis=1)
    return g.reshape(AT_HEADS // 2, 2, ATT_ROLL)


ATT_PAD = ATT_TK - ATT_TQ
ATT_EDGE = ATT_PAD // ATT_TQ
ATT_RG = 32


def _attn_units(q_ref, k_ref, v_ref, g_ref, o_ref, bias_ref, kp_ref, vp_ref, s_ref, p_ref,
                *, new_head_pair, seq):
    @pl.when(new_head_pair)
    def _():
        qc = lax.broadcasted_iota(jnp.int32, (ATT_TQ, ATT_TK), 0) // CHUNK
        kc = lax.broadcasted_iota(jnp.int32, (ATT_TQ, ATT_TK), 1) // CHUNK
        in_band = (kc >= qc) & (kc <= qc + LEFT_CHUNKS)
        g = g_ref[0]
        for hh in range(2):
            tbl = jnp.broadcast_to(g[hh:hh + 1, :], (ATT_TQ, ATT_ROLL))
            toep = pltpu.roll(tbl, 0, 1, stride=1, stride_axis=0)[:, :ATT_TK]
            rows = slice(hh * ATT_TQ, (hh + 1) * ATT_TQ)
            for n in range(ATT_EDGE + 1):
                first_chunk = (ATT_PAD - n * ATT_TQ) // CHUNK if n < ATT_EDGE else 0
                bias_ref[n, rows, :] = jnp.where(in_band & (kc >= first_chunk), toep, -jnp.inf)

    pad = jnp.zeros((ATT_PAD, LANES), BF16)
    kp_ref[:ATT_PAD, :] = pad
    kp_ref[ATT_PAD:, :] = k_ref[...]
    vp_ref[:ATT_PAD, :] = pad
    vp_ref[ATT_PAD:, :] = v_ref[...]

    head0 = lax.broadcasted_iota(jnp.int32, (ATT_TQ, LANES), 1) < AT_DH
    n_blocks = seq // ATT_TQ
    yield

    def scores(i):
        slot = i % 2
        q = q_ref[i * ATT_TQ:(i + 1) * ATT_TQ, :]
        zero = jnp.zeros_like(q)
        qq = jnp.concatenate([jnp.where(head0, q, zero), jnp.where(head0, zero, q)], axis=0)
        s = _dot(qq, kp_ref[i * ATT_TQ:i * ATT_TQ + ATT_TK, :], NT)
        s_ref[slot] = s + bias_ref[min(i, ATT_EDGE)]

    def outputs(i):
        slot = i % 2
        sums = []
        for g in range(2 * ATT_TQ // ATT_RG):
            rows = slice(g * ATT_RG, (g + 1) * ATT_RG)
            s = s_ref[slot, rows, :]
            p = jnp.exp2(s - jnp.max(s, axis=-1, keepdims=True))
            sums.append(jnp.sum(p, axis=-1, keepdims=True))
            p_ref[slot, rows, :] = p.astype(BF16)
        sums = jnp.concatenate(sums, axis=0)
        acc = _dot(p_ref[slot], vp_ref[i * ATT_TQ:i * ATT_TQ + ATT_TK, :])
        num = jnp.where(head0, acc[:ATT_TQ], acc[ATT_TQ:])
        den = jnp.where(head0, sums[:ATT_TQ], sums[ATT_TQ:])
        o_ref[i * ATT_TQ:(i + 1) * ATT_TQ, :] = (num / den).astype(o_ref.dtype)

    scores(0)
    for i in range(n_blocks):
        if i + 1 < n_blocks:
            scores(i + 1)
        outputs(i)
        yield


def _mixers_kernel(hq_ref, hf_ref, hi_ref, hg_ref, lbl_ref, nw_ref, wexp_ref, mask_ref,
                   q_ref, k_ref, v_ref, g_ref, *rest, n_cast, n_chunks, seq, steps_per_pair):
    cast_in, (y_ref, o_ref), cast_out = rest[:n_cast], rest[n_cast:n_cast + 2], rest[n_cast + 2:2 * n_cast + 2]
    st_ref, qx_ref, kx_ref, d_ref, bias_ref, kp_ref, vp_ref, s_ref, p_ref = rest[2 * n_cast + 2:]
    for src, dst in zip(cast_in, cast_out):
        dst[...] = src[...].astype(BF16)
    step = pl.program_id(0) * pl.num_programs(1) + pl.program_id(1)
    hgrn = _hgrn_units(hq_ref, hf_ref, hi_ref, hg_ref, lbl_ref, nw_ref, wexp_ref, mask_ref,
                       y_ref, st_ref, qx_ref, kx_ref, d_ref,
                       first_block=pl.program_id(1) == 0, n_chunks=n_chunks)
    attn = _attn_units(q_ref, k_ref, v_ref, g_ref, o_ref, bias_ref, kp_ref, vp_ref, s_ref, p_ref,
                       new_head_pair=step % steps_per_pair == 0, seq=seq)
    n_hgrn, n_attn = n_chunks * HG_HEADS, seq // ATT_TQ
    assert n_hgrn % n_attn == 0
    next(hgrn)
    next(attn)
    for _ in range(n_attn):
        next(attn)
        for _ in range(n_hgrn // n_attn):
            next(hgrn)
    assert next(attn, None) is None and next(hgrn, None) is None


def _mixers(z, hf, lb_logits, hg_norm_w, rel_bias, later_weights, batch, seq, *, tb=MIX_TB):
    wexp, masks = _hgrn_tables()
    g = _rel_table(rel_bias)
    nblk = seq // tb
    steps = batch * nblk
    assert steps == AT_HEADS // 2 * batch
    col0 = 4 * HG_WIDTH // LANES
    ncol = AT_WIDTH // LANES

    def step_of(b, i):
        return b * nblk + i

    def hspec(col):
        return pl.BlockSpec((tb, HG_WIDTH), lambda b, i, col=col: (b * nblk + i, col))

    def aspec(which):
        return pl.BlockSpec((seq, LANES), lambda b, i, which=which: (
            step_of(b, i) % batch, col0 + which * ncol + step_of(b, i) // batch))

    def const(shape):
        return pl.BlockSpec(shape, lambda b, i: (0,) * len(shape))

    def cast_spec(w):
        return pl.BlockSpec((w.shape[0] // steps, w.shape[1]), lambda b, i: (step_of(b, i), 0))

    cast_specs = [cast_spec(w) for w in later_weights]
    outs = pl.pallas_call(
        functools.partial(_mixers_kernel, n_cast=len(later_weights), n_chunks=tb // CHUNK, seq=seq,
                          steps_per_pair=batch),
        out_shape=(jax.ShapeDtypeStruct((batch * seq, HG_WIDTH), BF16),
                   jax.ShapeDtypeStruct((batch * seq, AT_WIDTH), BF16),
                   *[jax.ShapeDtypeStruct(w.shape, BF16) for w in later_weights]),
        grid=(batch, nblk),
        in_specs=[
            hspec(0), hspec(0), hspec(2), hspec(3),
            const((2, HG_WIDTH)), const((1, HG_DV)), const(wexp.shape), const(masks.shape),
            aspec(0), aspec(1), aspec(2),
            pl.BlockSpec((1, 2, ATT_ROLL), lambda b, i: (step_of(b, i) // batch, 0, 0)),
            *cast_specs,
        ],
        out_specs=(
            pl.BlockSpec((tb, HG_WIDTH), lambda b, i: (b * nblk + i, 0)),
            pl.BlockSpec((seq, LANES), lambda b, i: (step_of(b, i) % batch, step_of(b, i) // batch)),
            *cast_specs,
        ),
        scratch_shapes=[pltpu.VMEM((HG_HEADS, HG_DV, HG_DK), F32),
                        pltpu.VMEM((2, N_OPS, HG_HEADS, CHUNK, HG_DK), BF16),
                        pltpu.VMEM((2, N_OPS, HG_HEADS, CHUNK, HG_DK), BF16),
                        pltpu.VMEM((2, 1, HG_WIDTH), F32),
                        pltpu.VMEM((ATT_EDGE + 1, 2 * ATT_TQ, ATT_TK), F32),
                        pltpu.VMEM((ATT_PAD + seq, LANES), BF16),
                        pltpu.VMEM((ATT_PAD + seq, LANES), BF16),
                        pltpu.VMEM((2, 2 * ATT_TQ, ATT_TK), F32),
                        pltpu.VMEM((2, 2 * ATT_TQ, ATT_TK), BF16)],
        compiler_params=pltpu.CompilerParams(
            dimension_semantics=("arbitrary", "arbitrary"),
            vmem_limit_bytes=VMEM_LIMIT),
        name="mixers",
    )(z, hf, z, z, lb_logits, hg_norm_w.reshape(1, HG_DV), jnp.asarray(wexp, BF16), jnp.asarray(masks),
      z, z, z, g, *later_weights)
    return outs[0], outs[1], outs[2:]


def _merge_out_kernel(ya_ref, yb_ref, ga0_ref, ga1_ref, gb0_ref, gb1_ref, x_ref,
                      wa_ref, wb_ref, wo_ref, nw_ref, h_ref, u_ref, *, rc):
    for c in range(h_ref.shape[0] // rc):
        r = slice(c * rc, (c + 1) * rc)
        pa = _dot(ya_ref[r, :], wa_ref[...])
        pb = _dot(yb_ref[r, :], wb_ref[...])
        ga = jnp.concatenate([ga0_ref[r, :], ga1_ref[r, :]], axis=1).astype(F32)
        gb = jnp.concatenate([gb0_ref[r, :], gb1_ref[r, :]], axis=1).astype(F32)
        merged = (jax.nn.sigmoid(ga) * pa + jax.nn.sigmoid(gb) * pb).astype(BF16)
        h = x_ref[r, :] + _dot(merged, wo_ref[...])
        h_ref[r, :] = h
        u_ref[r, :] = _rms(h, nw_ref[...]).astype(BF16)


def _merge_out(ya, yb, z, x2, wa_bf, wb_bf, wo_bf, norm_w, *, tm=MERGE_TM, rc=MERGE_RC):
    m, d = x2.shape
    gw = d // 2
    gcol = (4 * HG_WIDTH + 3 * AT_WIDTH) // gw
    assert gcol * gw == 4 * HG_WIDTH + 3 * AT_WIDTH

    def resident(shape):
        return pl.BlockSpec(shape, lambda i: (0, 0), pipeline_mode=pl.Buffered(1))

    def gate(blk):
        return pl.BlockSpec((tm, gw), lambda i, blk=blk: (i, gcol + blk))

    return pl.pallas_call(
        functools.partial(_merge_out_kernel, rc=rc),
        out_shape=(jax.ShapeDtypeStruct((m, d), F32), jax.ShapeDtypeStruct((m, d), BF16)),
        grid=(m // tm,),
        in_specs=[
            pl.BlockSpec((tm, HG_WIDTH), lambda i: (i, 0)),
            pl.BlockSpec((tm, AT_WIDTH), lambda i: (i, 0)),
            gate(0), gate(1), gate(2), gate(3),
            pl.BlockSpec((tm, d), lambda i: (i, 0)),
            resident(wa_bf.shape), resident(wb_bf.shape), resident(wo_bf.shape),
            resident((1, d)),
        ],
        out_specs=(pl.BlockSpec((tm, d), lambda i: (i, 0)), pl.BlockSpec((tm, d), lambda i: (i, 0))),
        compiler_params=pltpu.CompilerParams(
            dimension_semantics=("parallel",),
            vmem_limit_bytes=VMEM_LIMIT),
        name="merge_out",
    )(ya, yb, z, z, z, z, x2, wa_bf, wb_bf, wo_bf, norm_w.reshape(1, d))


MLP_SLAB = 2 * MXU_TILE


def _mlp_kernel(h_ref, u_ref, wu_ref, wd_ref, fw_ref, o_ref, hbuf_ref, h_sem):
    i, f = pl.program_id(0), pl.program_id(1)
    tm = o_ref.shape[0]

    def h_copy():
        return pltpu.make_async_copy(h_ref.at[pl.ds(i * tm, tm), :], hbuf_ref, h_sem)

    @pl.when(f == 0)
    def _():
        h_copy().start()
        o_ref[...] = jnp.zeros_like(o_ref)

    a = []
    for c in range(wu_ref.shape[1] // MLP_SLAB):
        cols = slice(c * MLP_SLAB, (c + 1) * MLP_SLAB)
        t = jnp.maximum(_dot(u_ref[...], wu_ref[:, cols]), 0.0)
        a.append((t * t).astype(BF16))
    a = jnp.concatenate(a, axis=1)
    for c in range(o_ref.shape[1] // MLP_SLAB):
        cols = slice(c * MLP_SLAB, (c + 1) * MLP_SLAB)
        o_ref[:, cols] += _dot(a, wd_ref[:, cols])

    @pl.when(f == pl.num_programs(1) - 1)
    def _():
        h_copy().wait()
        o_ref[...] = _rms(hbuf_ref[...] + o_ref[...], fw_ref[...])


def _mlp(h, u, wu_bf, wd_bf, final_w, *, tm=MLP_TM, tf=MLP_TF):
    m, d = h.shape
    dff = wu_bf.shape[1]
    return pl.pallas_call(
        _mlp_kernel,
        out_shape=jax.ShapeDtypeStruct((m, d), F32),
        grid=(m // tm, dff // tf),
        in_specs=[
            pl.BlockSpec(memory_space=pl.ANY),
            pl.BlockSpec((tm, d), lambda i, f: (i, 0)),
            pl.BlockSpec((d, tf), lambda i, f: (0, f)),
            pl.BlockSpec((tf, d), lambda i, f: (f, 0)),
            pl.BlockSpec((1, d), lambda i, f: (0, 0)),
        ],
        out_specs=pl.BlockSpec((tm, d), lambda i, f: (i, 0)),
        scratch_shapes=[pltpu.VMEM((tm, d), F32), pltpu.SemaphoreType.DMA],
        compiler_params=pltpu.CompilerParams(
            dimension_semantics=("arbitrary", "arbitrary"),
            vmem_limit_bytes=BIG_VMEM_LIMIT),
        name="mlp",
    )(h, u, wu_bf, wd_bf, final_w.reshape(1, d))


def kernel(x, w_in, lb_logits, hg_norm_w, rel_bias, w_branch_a, w_branch_b, w_out,
           norm_mix_w, norm_mlp_w, w_up, w_down, norm_final_w):
    batch, seq, d = x.shape
    assert d == D_MODEL and seq % ATT_TQ == 0 and w_in.shape[0] == 1
    x2 = x.reshape(batch * seq, d)
    z, hf = _in_proj(x2, norm_mix_w[0], w_in[0])
    ya, yb, (wa_bf, wb_bf, wo_bf, wu_bf, wd_bf) = _mixers(
        z, hf, lb_logits, hg_norm_w[0], rel_bias[0],
        (w_branch_a[0], w_branch_b[0], w_out[0], w_up[0], w_down[0]), batch, seq)
    h, u = _merge_out(ya, yb, z, x2, wa_bf, wb_bf, wo_bf, norm_mlp_w[0])
    out = _mlp(h, u, wu_bf, wd_bf, norm_final_w)
    return out.reshape(batch, seq, d)
```
